```python
import math
import jax
import jax.numpy as jnp
from jax import lax
import numpy as np

D_MODEL = 1024
BATCH = 8
SEQ = 8192
DEPTH = 2

CTX_LEN = 256
GRID_W = 64

S5_W = D_MODEL // 4
S5_GC = 16
S5_G = S5_W // S5_GC
S5_P = 64

MLA_H = D_MODEL // 128
MLA_NOPE = 64
MLA_ROPE = 32
MLA_V = 64
MLA_QR = D_MODEL // 4
MLA_KVR = D_MODEL // 8
MLA_W = MLA_H * MLA_V
MLA_SCALE = (MLA_NOPE + MLA_ROPE) ** -0.5
ROPE_AXIS = MLA_ROPE // 2
ROPE_BASE = 10000.0
Q_BLOCK = 128

ML_H = D_MODEL // 256
ML_D = 64
ML_W = ML_H * ML_D
ML_CHUNK = 64
ML_CONV = 3

MIX_W = S5_W + MLA_W + ML_W
IN_SIZES = (S5_W, MLA_QR, MLA_KVR, MLA_ROPE, ML_W, ML_W, ML_W, ML_W, 4 * ML_H)
N_IN = sum(IN_SIZES)

N_EXPERTS = 64
TOP_K = 8
N_EXPERT_GROUPS = 8
TOP_GROUPS = 4
EPG = N_EXPERTS // N_EXPERT_GROUPS
EXPERT_F = D_MODEL // 4
SHARED_F = D_MODEL // 4
ROUTED_SCALE = 2.5
ROUTE_BLOCK = 128

DEEPNORM_ALPHA = (2 * DEPTH) ** 0.25
DEEPNORM_BETA = (8 * DEPTH) ** -0.25
LN_EPS = 1e-5

kernel_name = 'hybrid_s5_mla_mlstm_moe_diffusion_trunk'


def _layer_norm(x, gain=None, bias=None):
    xf = x.astype(jnp.float32)
    mu = jnp.mean(xf, -1, keepdims=True)
    var = jnp.mean(jnp.square(xf - mu), -1, keepdims=True)
    y = (xf - mu) * lax.rsqrt(var + LN_EPS)
    if gain is not None:
        y = y * gain.astype(jnp.float32) + bias.astype(jnp.float32)
    return y.astype(x.dtype)


def _rms_norm(x, gain):
    xf = x.astype(jnp.float32)
    y = xf * lax.rsqrt(jnp.mean(xf * xf, -1, keepdims=True) + 1e-6) * gain.astype(jnp.float32)
    return y.astype(x.dtype)


def _split_cols(z):
    parts, off = [], 0
    for size in IN_SIZES:
        parts.append(z[..., off:off + size])
        off += size
    return parts


def _axial_angles(n_tokens):
    rows = n_tokens // GRID_W
    row_ids = jnp.repeat(jnp.arange(rows), GRID_W).astype(jnp.float32)
    col_ids = jnp.tile(jnp.arange(GRID_W), rows).astype(jnp.float32)
    half = ROPE_AXIS // 2
    inv_freq = ROPE_BASE ** (-jnp.arange(half, dtype=jnp.float32) / half)
    ang_r = row_ids[:, None] * inv_freq
    ang_c = col_ids[:, None] * inv_freq
    return jnp.cos(ang_r), jnp.sin(ang_r), jnp.cos(ang_c), jnp.sin(ang_c)


def _rot_half(v, cos, sin):
    v1, v2 = jnp.split(v, 2, axis=-1)
    return jnp.concatenate([v1 * cos - v2 * sin, v1 * sin + v2 * cos], axis=-1)


def _rope2d(v, cr, sr, cc, sc):
    vr, vc = jnp.split(v, 2, axis=-1)
    out = jnp.concatenate([_rot_half(vr, cr, sr), _rot_half(vc, cc, sc)], axis=-1)
    return out.astype(v.dtype)


def _linear_recurrence_combine(left, right):
    a_l, b_l = left
    a_r, b_r = right
    return a_r * a_l, a_r * b_l + b_r


def _s5_discretise(lam_re, lam_im, log_step, b_re, b_im, c_re, c_im):
    lam = lax.complex(lam_re.astype(jnp.float32), lam_im.astype(jnp.float32))
    step = jnp.exp(log_step.astype(jnp.float32))[..., None]
    a_bar = jnp.exp(lam * step)
    b_mat = lax.complex(b_re.astype(jnp.float32), b_im.astype(jnp.float32))
    b_bar = ((a_bar - 1.0) / lam)[..., None] * b_mat
    c_mat = lax.complex(c_re.astype(jnp.float32), c_im.astype(jnp.float32))
    return a_bar, b_bar, c_mat


def _s5_direction(ug, a_bar, b_bar, s0, reverse):
    bu = jnp.einsum('blgc,gpc->blgp', ug.astype(jnp.complex64), b_bar)
    if reverse:
        bu = bu[:, ::-1]
    bu = bu.at[:, 0].add(a_bar * s0)
    decay = jnp.broadcast_to(a_bar, (1,) + bu.shape[1:])
    _, states = lax.associative_scan(_linear_recurrence_combine, (decay, bu), axis=1)
    final = states[:, -1]
    if reverse:
        states = states[:, ::-1]
    return states, final


def _s5_readout(ug, st_f, st_b, c_mat, d_skip, glu_w, glu_b, dtype):
    y = (jnp.einsum('blgp,gcp->blgc', st_f, c_mat[0]).real
         + jnp.einsum('blgp,gcp->blgc', st_b, c_mat[1]).real
         + ug * d_skip.astype(jnp.float32).reshape(S5_G, S5_GC))
    y = y.reshape(y.shape[0], y.shape[1], S5_W)
    g = jax.nn.gelu(y)
    out = g * jax.nn.sigmoid(g @ glu_w.astype(jnp.float32) + glu_b.astype(jnp.float32))
    return out.astype(dtype)


def _s5_mixer(u_ctx, u_lat, lam_re, lam_im, log_step, b_re, b_im, c_re, c_im, d_skip, glu_w, glu_b, with_ctx_out):
    a_bar, b_bar, c_mat = _s5_discretise(lam_re, lam_im, log_step, b_re, b_im, c_re, c_im)
    bsz, n_ctx, _ = u_ctx.shape
    n_lat = u_lat.shape[1]
    zero = jnp.zeros((bsz, S5_G, S5_P), jnp.complex64)
    uc = u_ctx.astype(jnp.float32).reshape(bsz, n_ctx, S5_G, S5_GC)
    ul = u_lat.astype(jnp.float32).reshape(bsz, n_lat, S5_G, S5_GC)
    st_cf, fin_cf = _s5_direction(uc, a_bar[0], b_bar[0], zero, False)
    st_cb, fin_cb = _s5_direction(uc, a_bar[1], b_bar[1], zero, True)
    st_lf, _ = _s5_direction(ul, a_bar[0], b_bar[0], fin_cf, False)
    st_lb, _ = _s5_direction(ul, a_bar[1], b_bar[1], fin_cb, True)
    out_lat = _s5_readout(ul, st_lf, st_lb, c_mat, d_skip, glu_w, glu_b, u_lat.dtype)
    out_ctx = _s5_readout(uc, st_cf, st_cb, c_mat, d_skip, glu_w, glu_b, u_ctx.dtype) if with_ctx_out else None
    return out_lat, out_ctx


def _mla_queries(q_c, q_norm_g, w_q_up):
    q = _rms_norm(q_c, q_norm_g) @ w_q_up
    q = q.reshape(q.shape[0], q.shape[1], MLA_H, MLA_NOPE + MLA_ROPE)
    return q[..., :MLA_NOPE], q[..., MLA_NOPE:]


def _mla_keys_values(kv_c, kv_norm_g, w_kv_up):
    kv = _rms_norm(kv_c, kv_norm_g) @ w_kv_up
    kv = kv.reshape(kv.shape[0], kv.shape[1], MLA_H, MLA_NOPE + MLA_V)
    return kv[..., :MLA_NOPE], kv[..., MLA_NOPE:]


def _attend(q_nope, q_rope, k_nope, k_rope, v):
    s = (jnp.einsum('bqhd,bkhd->bhqk', q_nope, k_nope)
         + jnp.einsum('bqhr,bkr->bhqk', q_rope, k_rope))
    p = jax.nn.softmax(s.astype(jnp.float32) * MLA_SCALE, axis=-1).astype(v.dtype)
    return jnp.einsum('bhqk,bkhd->bqhd', p, v)


def _mla_mixer(qc_ctx, kvc_ctx, kr_ctx, qc_lat, kvc_lat, kr_lat, q_norm_g, w_q_up, kv_norm_g, w_kv_up, with_ctx_out):
    bsz, n_lat, _ = qc_lat.shape
    n_ctx = kvc_ctx.shape[1]
    cr, sr, cc, sc = _axial_angles(n_lat)
    kn_c, v_c = _mla_keys_values(kvc_ctx, kv_norm_g, w_kv_up)
    kn_l, v_l = _mla_keys_values(kvc_lat, kv_norm_g, w_kv_up)
    kr_l = _rope2d(kr_lat, cr, sr, cc, sc)
    qn_l, qr_l = _mla_queries(qc_lat, q_norm_g, w_q_up)
    qr_l = _rope2d(qr_l, cr[:, None], sr[:, None], cc[:, None], sc[:, None])
    k_nope = jnp.concatenate([kn_c, kn_l], axis=1)
    k_rope = jnp.concatenate([kr_ctx, kr_l], axis=1)
    v_all = jnp.concatenate([v_c, v_l], axis=1)
    n_blocks = n_lat // Q_BLOCK

    def to_blocks(t):
        return jnp.moveaxis(t.reshape(bsz, n_blocks, Q_BLOCK, *t.shape[2:]), 1, 0)

    o = lax.map(lambda qs: _attend(qs[0], qs[1], k_nope, k_rope, v_all), (to_blocks(qn_l), to_blocks(qr_l)))
    out_lat = jnp.moveaxis(o, 0, 1).reshape(bsz, n_lat, MLA_W)
    if with_ctx_out:
        qn_c, qr_c = _mla_queries(qc_ctx, q_norm_g, w_q_up)
        out_ctx = _attend(qn_c, qr_c, kn_c, kr_ctx, v_c).reshape(bsz, n_ctx, MLA_W)
    else:
        out_ctx = None
    return out_lat, out_ctx


def _centred_dwconv(x, w, b):
    n = x.shape[1]
    half = ML_CONV // 2
    xp = jnp.pad(x, ((0, 0), (half, half), (0, 0)))
    y = b
    for j in range(ML_CONV):
        y = y + xp[:, j:j + n] * w[j]
    return y


def _mlstm_inputs(q_in, k_in, v_in, g_in, conv_w, conv_b, gate_b):
    bsz, n, _ = q_in.shape
    qk = jax.nn.silu(_centred_dwconv(jnp.concatenate([q_in, k_in], -1), conv_w, conv_b).astype(jnp.float32))
    q = qk[..., :ML_W].reshape(bsz, n, ML_H, ML_D)
    k = qk[..., ML_W:].reshape(bsz, n, ML_H, ML_D) * (ML_D ** -0.5)
    v = v_in.astype(jnp.float32).reshape(bsz, n, ML_H, ML_D)
    g = (g_in.astype(jnp.float32) + gate_b.astype(jnp.float32).reshape(4 * ML_H)).reshape(bsz, n, 4, ML_H)
    fwd = (g[:, :, 0], jax.nn.log_sigmoid(g[:, :, 2]))
    bwd = (g[:, :, 1], jax.nn.log_sigmoid(g[:, :, 3]))
    return q, k, v, fwd, bwd


def _to_chunks(t):
    bsz, n = t.shape[:2]
    t = t.reshape(bsz, n // ML_CHUNK, ML_CHUNK, *t.shape[2:])
    return jnp.moveaxis(t, 3, 1)


def _mlstm_chunk_step(carry, inp):
    c_prev, n_prev, m_prev = carry
    c_loc, n_loc, m_loc, b_last = inp
    m_new = jnp.maximum(b_last + m_prev, m_loc)
    a = jnp.exp(b_last + m_prev - m_new)
    s = jnp.exp(m_loc - m_new)
    c_new = a[..., None, None] * c_prev + s[..., None, None] * c_loc
    n_new = a[..., None] * n_prev + s[..., None] * n_loc
    return (c_new, n_new, m_new), (c_prev, n_prev, m_prev)


def _mlstm_direction(q, k, v, log_i, log_f, state, reverse, with_out):
    if reverse:
        q, k, v, log_i, log_f = (t[:, ::-1] for t in (q, k, v, log_i, log_f))
    kc, vc, ic, fc = (_to_chunks(t) for t in (k, v, log_i, log_f))
    b = jnp.cumsum(fc, axis=-1)
    b_last = b[..., -1]
    w_end = b_last[..., None] - b + ic
    m_loc = jnp.max(w_end, axis=-1)
    e = jnp.exp(w_end - m_loc[..., None])
    c_loc = jnp.einsum('bhcs,bhcsd,bhcse->bhcde', e, vc, kc)
    n_loc = jnp.einsum('bhcs,bhcse->bhce', e, kc)
    final, incoming = lax.scan(_mlstm_chunk_step, state,
                               tuple(jnp.moveaxis(t, 2, 0) for t in (c_loc, n_loc, m_loc, b_last)))
    if not with_out:
        return None, final
    c_in, n_in, m_in = (jnp.moveaxis(t, 0, 2) for t in incoming)
    qc = _to_chunks(q)
    order = jnp.tril(jnp.ones((ML_CHUNK, ML_CHUNK), bool))
    log_d = jnp.where(order, b[..., :, None] - b[..., None, :] + ic[..., None, :], -jnp.inf)
    log_inter = b + m_in[..., None]
    m_t = jnp.maximum(log_inter, jnp.max(log_d, axis=-1))
    dw = jnp.exp(log_d - m_t[..., None])
    w_inter = jnp.exp(log_inter - m_t)
    s = jnp.einsum('bhctd,bhcsd->bhcts', qc, kc) * dw
    num = (jnp.einsum('bhcts,bhcsd->bhctd', s, vc)
           + w_inter[..., None] * jnp.einsum('bhcde,bhcte->bhctd', c_in, qc))
    den = jnp.sum(s, axis=-1) + w_inter * jnp.einsum('bhce,bhcte->bhct', n_in, qc)
    h = num / jnp.maximum(jnp.abs(den), jnp.exp(-m_t))[..., None]
    h = jnp.moveaxis(h, 1, 3).reshape(q.shape)
    if reverse:
        h = h[:, ::-1]
    return h, final


def _mlstm_output(h_f, h_b, o_in, norm_g, dtype):
    bsz, n = o_in.shape[:2]
    h = jax.nn.sigmoid(o_in.astype(jnp.float32)).reshape(bsz, n, ML_H, ML_D) * (h_f + h_b)
    mu = jnp.mean(h, -1, keepdims=True)
    var = jnp.mean(jnp.square(h - mu), -1, keepdims=True)
    h = ((h - mu) * lax.rsqrt(var + LN_EPS)).reshape(bsz, n, ML_W) * norm_g.astype(jnp.float32)
    return h.astype(dtype)


def _mlstm_mixer(parts_ctx, parts_lat, conv_w, conv_b, gate_b, norm_g, with_ctx_out):
    qc, kc, vc, g_fc, g_bc = _mlstm_inputs(parts_ctx[0], parts_ctx[1], parts_ctx[2], parts_ctx[4], conv_w, conv_b, gate_b)
    ql, kl, vl, g_fl, g_bl = _mlstm_inputs(parts_lat[0], parts_lat[1], parts_lat[2], parts_lat[4], conv_w, conv_b, gate_b)
    bsz = ql.shape[0]
    zero = (jnp.zeros((bsz, ML_H, ML_D, ML_D), jnp.float32),
            jnp.zeros((bsz, ML_H, ML_D), jnp.float32),
            jnp.zeros((bsz, ML_H), jnp.float32))
    h_cf, st_f = _mlstm_direction(qc, kc, vc, g_fc[0], g_fc[1], zero, False, with_ctx_out)
    h_cb, st_b = _mlstm_direction(qc, kc, vc, g_bc[0], g_bc[1], zero, True, with_ctx_out)
    h_lf, _ = _mlstm_direction(ql, kl, vl, g_fl[0], g_fl[1], st_f, False, True)
    h_lb, _ = _mlstm_direction(ql, kl, vl, g_bl[0], g_bl[1], st_b, True, True)
    out_lat = _mlstm_output(h_lf, h_lb, parts_lat[3], norm_g, parts_lat[3].dtype)
    out_ctx = _mlstm_output(h_cf, h_cb, parts_ctx[3], norm_g, parts_ctx[3].dtype) if with_ctx_out else None
    return out_lat, out_ctx


def _moe(h, router_w, router_bias, w_gate, w_up, w_down, sw_gate, sw_up, sw_down):
    n_tok = h.shape[0]
    scores = jax.nn.sigmoid((h @ router_w).astype(jnp.float32))
    biased = scores + router_bias.astype(jnp.float32)
    group_score = jnp.sum(lax.top_k(biased.reshape(n_tok, N_EXPERT_GROUPS, EPG), 2)[0], -1)
    kth = lax.top_k(group_score, TOP_GROUPS)[0][:, -1:]
    allowed = jnp.repeat(group_score >= kth, EPG, axis=1)
    _, top_e = lax.top_k(jnp.where(allowed, biased, -jnp.inf), TOP_K)
    gate = jnp.take_along_axis(scores, top_e, axis=1)
    gate = gate / jnp.sum(gate, -1, keepdims=True) * ROUTED_SCALE
    n_assign = n_tok * TOP_K
    e_flat = top_e.reshape(n_assign)
    tok_flat = jnp.repeat(jnp.arange(n_tok, dtype=jnp.int32), TOP_K)
    g_flat = gate.reshape(n_assign).astype(h.dtype)
    order = jnp.argsort(e_flat)
    e_sorted = e_flat[order]
    counts = jnp.bincount(e_flat, length=N_EXPERTS)
    starts = jnp.cumsum(counts) - counts
    padded = (counts + ROUTE_BLOCK - 1) // ROUTE_BLOCK * ROUTE_BLOCK
    pad_end = jnp.cumsum(padded)
    pad_start = pad_end - padded
    dest = pad_start[e_sorted] + jnp.arange(n_assign) - starts[e_sorted]
    n_rows = -(-(n_assign + N_EXPERTS * (ROUTE_BLOCK - 1)) // ROUTE_BLOCK) * ROUTE_BLOCK
    n_blocks = n_rows // ROUTE_BLOCK
    tok_rows = jnp.zeros((n_rows,), jnp.int32).at[dest].set(tok_flat[order])
    gate_rows = jnp.zeros((n_rows,), h.dtype).at[dest].set(g_flat[order])
    blk_expert = jnp.minimum(jnp.searchsorted(pad_end, jnp.arange(n_blocks) * ROUTE_BLOCK, side='right'), N_EXPERTS - 1)

    def expert_block(acc, blk):
        tok, g, e = blk
        xb = h[tok]
        ob = (jax.nn.silu(xb @ w_gate[e]) * (xb @ w_up[e])) @ w_down[e]
        return acc.at[tok].add(ob * g[:, None]), None

    routed, _ = lax.scan(expert_block, jnp.zeros_like(h),
                         (tok_rows.reshape(n_blocks, ROUTE_BLOCK), gate_rows.reshape(n_blocks, ROUTE_BLOCK), blk_expert))
    shared = (jax.nn.silu(h @ sw_gate) * (h @ sw_up)) @ sw_down
    return routed + shared


def setup_inputs(seed: int = 0) -> dict:
    key = jax.random.key(seed)
    ks = iter(jax.random.split(key, 48))
    f32 = jnp.float32
    L = DEPTH
    D = D_MODEL

    def nrm(shape, scale):
        return jax.random.normal(next(ks), shape, f32) * scale

    x = nrm((BATCH, SEQ, D), 1.0)
    c = nrm((BATCH, D), 1.0)
    ctx = nrm((BATCH, CTX_LEN, D), 1.0)
    c_ctx = nrm((D,), 1.0)
    ada_w = nrm((L, D, 6 * D), 0.5 * D ** -0.5)
    ada_b = nrm((L, 6 * D), 0.01)
    w_in = nrm((L, D, N_IN), D ** -0.5)
    s5_lambda_re = -0.5 + nrm((L, 2, S5_G, S5_P), 0.01)
    s5_lambda_im = math.pi * jnp.arange(S5_P, dtype=f32) + nrm((L, 2, S5_G, S5_P), 0.01)
    s5_log_step = jax.random.uniform(next(ks), (L, 2, S5_G), f32, math.log(1e-3), math.log(1e-1))
    s5_b_re = nrm((L, 2, S5_G, S5_P, S5_GC), (2 * S5_GC) ** -0.5)
    s5_b_im = nrm((L, 2, S5_G, S5_P, S5_GC), (2 * S5_GC) ** -0.5)
    s5_c_re = nrm((L, 2, S5_G, S5_GC, S5_P), (2 * S5_P) ** -0.5)
    s5_c_im = nrm((L, 2, S5_G, S5_GC, S5_P), (2 * S5_P) ** -0.5)
    s5_d = nrm((L, S5_W), 1.0)
    s5_glu_w = nrm((L, S5_W, S5_W), S5_W ** -0.5)
    s5_glu_b = nrm((L, S5_W), 0.01)
    mla_q_norm = 1.0 + nrm((L, MLA_QR), 0.01)
    mla_w_q_up = nrm((L, MLA_QR, MLA_H * (MLA_NOPE + MLA_ROPE)), MLA_QR ** -0.5)
    mla_kv_norm = 1.0 + nrm((L, MLA_KVR), 0.01)
    mla_w_kv_up = nrm((L, MLA_KVR, MLA_H * (MLA_NOPE + MLA_V)), MLA_KVR ** -0.5)
    ml_conv_w = nrm((L, ML_CONV, 2 * ML_W), ML_CONV ** -0.5)
    ml_conv_b = nrm((L, 2 * ML_W), 0.01)
    ml_i_bias = nrm((L, 2, ML_H), 0.1)
    ml_f_bias = jnp.linspace(3.0, 6.0, ML_H, dtype=f32) + nrm((L, 2, ML_H), 0.1)
    ml_gate_b = jnp.concatenate([ml_i_bias, ml_f_bias], axis=1)
    ml_norm_g = 1.0 + nrm((L, ML_W), 0.01)
    w_out = nrm((L, MIX_W, D), MIX_W ** -0.5 * DEEPNORM_BETA)
    ln1_g = 1.0 + nrm((L, D), 0.01)
    ln1_b = nrm((L, D), 0.01)
    ln2_g = 1.0 + nrm((L, D), 0.01)
    ln2_b = nrm((L, D), 0.01)
    router_w = nrm((L, D, N_EXPERTS), D ** -0.5)
    router_bias = nrm((L, N_EXPERTS), 0.01)
    exp_w_gate = nrm((L, N_EXPERTS, D, EXPERT_F), D ** -0.5)
    exp_w_up = nrm((L, N_EXPERTS, D, EXPERT_F), D ** -0.5)
    exp_w_down = nrm((L, N_EXPERTS, EXPERT_F, D), EXPERT_F ** -0.5 * DEEPNORM_BETA)
    sh_w_gate = nrm((L, D, SHARED_F), D ** -0.5)
    sh_w_up = nrm((L, D, SHARED_F), D ** -0.5)
    sh_w_down = nrm((L, SHARED_F, D), SHARED_F ** -0.5 * DEEPNORM_BETA)
    return {'x': x, 'c': c, 'ctx': ctx, 'c_ctx': c_ctx, 'ada_w': ada_w, 'ada_b': ada_b, 'w_in': w_in,
            's5_lambda_re': s5_lambda_re, 's5_lambda_im': s5_lambda_im, 's5_log_step': s5_log_step,
            's5_b_re': s5_b_re, 's5_b_im': s5_b_im, 's5_c_re': s5_c_re, 's5_c_im': s5_c_im, 's5_d': s5_d,
            's5_glu_w': s5_glu_w, 's5_glu_b': s5_glu_b, 'mla_q_norm': mla_q_norm, 'mla_w_q_up': mla_w_q_up,
            'mla_kv_norm': mla_kv_norm, 'mla_w_kv_up': mla_w_kv_up, 'ml_conv_w': ml_conv_w, 'ml_conv_b': ml_conv_b,
            'ml_gate_b': ml_gate_b, 'ml_norm_g': ml_norm_g, 'w_out': w_out, 'ln1_g': ln1_g, 'ln1_b': ln1_b,
            'ln2_g': ln2_g, 'ln2_b': ln2_b, 'router_w': router_w, 'router_bias': router_bias,
            'exp_w_gate': exp_w_gate, 'exp_w_up': exp_w_up, 'exp_w_down': exp_w_down,
            'sh_w_gate': sh_w_gate, 'sh_w_up': sh_w_up, 'sh_w_down': sh_w_down}


def reference(x, c, ctx, c_ctx, ada_w, ada_b, w_in, s5_lambda_re, s5_lambda_im, s5_log_step, s5_b_re, s5_b_im,
              s5_c_re, s5_c_im, s5_d, s5_glu_w, s5_glu_b, mla_q_norm, mla_w_q_up, mla_kv_norm, mla_w_kv_up,
              ml_conv_w, ml_conv_b, ml_gate_b, ml_norm_g, w_out, ln1_g, ln1_b, ln2_g, ln2_b, router_w, router_bias,
              exp_w_gate, exp_w_up, exp_w_down, sh_w_gate, sh_w_up, sh_w_down):
    x_lat, x_ctx = x, ctx
    d = x.shape[-1]
    n_lat_tok = x.shape[0] * x.shape[1]
    for layer in range(DEPTH):
        last = layer == DEPTH - 1
        mod_lat = jax.nn.silu(c) @ ada_w[layer] + ada_b[layer]
        mod_ctx = jax.nn.silu(c_ctx) @ ada_w[layer] + ada_b[layer]
        sh1, sc1, g1, sh2, sc2, g2 = jnp.split(mod_lat[:, None, :], 6, axis=-1)
        csh1, csc1, cg1, csh2, csc2, cg2 = jnp.split(mod_ctx, 6, axis=-1)

        h_lat = _layer_norm(x_lat) * (1 + sc1) + sh1
        h_ctx = _layer_norm(x_ctx) * (1 + csc1) + csh1
        z_lat = _split_cols(h_lat @ w_in[layer])
        z_ctx = _split_cols(h_ctx @ w_in[layer])
        s5_lat, s5_ctx = _s5_mixer(z_ctx[0], z_lat[0], s5_lambda_re[layer], s5_lambda_im[layer], s5_log_step[layer],
                                   s5_b_re[layer], s5_b_im[layer], s5_c_re[layer], s5_c_im[layer], s5_d[layer],
                                   s5_glu_w[layer], s5_glu_b[layer], not last)
        mla_lat, mla_ctx = _mla_mixer(z_ctx[1], z_ctx[2], z_ctx[3], z_lat[1], z_lat[2], z_lat[3],
                                      mla_q_norm[layer], mla_w_q_up[layer], mla_kv_norm[layer], mla_w_kv_up[layer],
                                      not last)
        ml_lat, ml_ctx = _mlstm_mixer(z_ctx[4:9], z_lat[4:9], ml_conv_w[layer], ml_conv_b[layer], ml_gate_b[layer],
                                      ml_norm_g[layer], not last)
        y_lat = jnp.concatenate([s5_lat, mla_lat, ml_lat], axis=-1) @ w_out[layer]
        x_lat = _layer_norm(DEEPNORM_ALPHA * x_lat + g1 * y_lat, ln1_g[layer], ln1_b[layer])

        f_lat = _layer_norm(x_lat) * (1 + sc2) + sh2
        if last:
            ffn_lat = _moe(f_lat.reshape(-1, d), router_w[layer], router_bias[layer], exp_w_gate[layer],
                           exp_w_up[layer], exp_w_down[layer], sh_w_gate[layer], sh_w_up[layer],
                           sh_w_down[layer]).reshape(x_lat.shape)
        else:
            y_ctx = jnp.concatenate([s5_ctx, mla_ctx, ml_ctx], axis=-1) @ w_out[layer]
            x_ctx = _layer_norm(DEEPNORM_ALPHA * x_ctx + cg1 * y_ctx, ln1_g[layer], ln1_b[layer])
            f_ctx = _layer_norm(x_ctx) * (1 + csc2) + csh2
            tokens = jnp.concatenate([f_lat.reshape(-1, d), f_ctx.reshape(-1, d)], axis=0)
            ffn = _moe(tokens, router_w[layer], router_bias[layer], exp_w_gate[layer], exp_w_up[layer],
                       exp_w_down[layer], sh_w_gate[layer], sh_w_up[layer], sh_w_down[layer])
            ffn_lat = ffn[:n_lat_tok].reshape(x_lat.shape)
            ffn_ctx = ffn[n_lat_tok:].reshape(x_ctx.shape)
            x_ctx = _layer_norm(DEEPNORM_ALPHA * x_ctx + cg2 * ffn_ctx, ln2_g[layer], ln2_b[layer])
        x_lat = _layer_norm(DEEPNORM_ALPHA * x_lat + g2 * ffn_lat, ln2_g[layer], ln2_b[layer])
    return x_lat
```

```python
import functools
import math

import jax
import jax.numpy as jnp
from jax import lax
from jax.experimental import pallas as pl
from jax.experimental.pallas import tpu as pltpu

F32 = jnp.float32
BF16 = jnp.bfloat16

D_MODEL = 1024
GRID_W = 64
S5_W = 256
S5_GC = 16
S5_G = 16
S5_P = 64
S5_STATE = S5_G * S5_P
MLA_H = 8
MLA_NOPE = 64
MLA_ROPE = 32
MLA_V = 64
MLA_QR = 256
MLA_KVR = 128
MLA_W = MLA_H * MLA_V
MLA_SCALE = (MLA_NOPE + MLA_ROPE) ** -0.5
ROPE_AXIS = MLA_ROPE // 2
ROPE_BASE = 10000.0
HEAD_PAD = 128
ML_H = 4
ML_D = 64
ML_W = 256
ML_AUG = ML_W + 128
N_EXPERTS = 64
TOP_K = 8
N_EXPERT_GROUPS = 8
TOP_GROUPS = 4
EPG = 8
EXPERT_F = 256
ROUTED_SCALE = 2.5
DEPTH = 2
DEEPNORM_ALPHA = (2 * DEPTH) ** 0.25
LN_EPS = 1e-5
SUBLANES = 8

Z_U, Z_QC, Z_MLQ, Z_MLK, Z_MLV, Z_MLO, Z_KVC, Z_MISC = 0, 256, 512, 768, 1024, 1280, 1536, 1664
Z_W = 1792
MISC_GATES = 0
MISC_ROPE = 64

VMEM_LIMIT = 48 * 1024 * 1024
ROW_TILE = 512
S5_STEPS = 64
ML_CHUNK = 256
ATT_TQ = 512
ATT_TK = 512
MOE_BLOCK = 256


def _cparams(sem):
    return pltpu.CompilerParams(dimension_semantics=sem, vmem_limit_bytes=VMEM_LIMIT)


def _ln(x):
    mu = jnp.mean(x, -1, keepdims=True)
    xc = x - mu
    var = jnp.mean(xc * xc, -1, keepdims=True)
    return xc * lax.rsqrt(var + LN_EPS)


def _modulate(h, sc, sh):
    tm, d = h.shape
    h3 = h.reshape(tm // SUBLANES, SUBLANES, d)
    return (h3 * (1.0 + sc)[None] + sh[None]).reshape(tm, d)


def _per_batch(v, g):
    tm, d = v.shape
    return (v.reshape(tm // SUBLANES, SUBLANES, d) * g[None]).reshape(tm, d)


def _ada_kernel(c_ref, w_ref, b_ref, o_ref):
    c = c_ref[...]
    s = c * jax.nn.sigmoid(c)
    o_ref[...] = jnp.dot(s.astype(BF16), w_ref[...].astype(BF16), preferred_element_type=F32) + b_ref[...]


def _ada_mod(cc, w, b):
    n = w.shape[1]
    tn = 1536
    return pl.pallas_call(
        _ada_kernel,
        grid=(n // tn,),
        in_specs=[pl.BlockSpec((16, D_MODEL), lambda j: (0, 0)),
                  pl.BlockSpec((D_MODEL, tn), lambda j: (0, j)),
                  pl.BlockSpec((1, tn), lambda j: (0, j))],
        out_specs=pl.BlockSpec((16, tn), lambda j: (0, j)),
        out_shape=jax.ShapeDtypeStruct((16, n), F32),
        compiler_params=_cparams(("arbitrary",)),
        name="ada_mod",
    )(cc, w, b.reshape(1, n))


def _inproj_kernel(x_ref, sc_ref, sh_ref, w_ref, z_ref):
    h = _modulate(_ln(x_ref[...]), sc_ref[0], sh_ref[0])
    z_ref[...] = jnp.dot(h.astype(BF16), w_ref[...], preferred_element_type=F32)


def _inproj(x, mod, w_pad, n_lat_tiles):
    n = x.shape[0]
    tm = ROW_TILE
    sel = lambda i: (i >= n_lat_tiles).astype(jnp.int32)
    return pl.pallas_call(
        _inproj_kernel,
        grid=(n // tm,),
        in_specs=[pl.BlockSpec((tm, D_MODEL), lambda i: (i, 0)),
                  pl.BlockSpec((1, SUBLANES, D_MODEL), lambda i: (sel(i), 0, 1)),
                  pl.BlockSpec((1, SUBLANES, D_MODEL), lambda i: (sel(i), 0, 0)),
                  pl.BlockSpec((D_MODEL, Z_W), lambda i: (0, 0))],
        out_specs=pl.BlockSpec((tm, Z_W), lambda i: (i, 0)),
        out_shape=jax.ShapeDtypeStruct((n, Z_W), F32),
        compiler_params=_cparams(("parallel",)),
        name="inproj",
    )(x, mod, mod, w_pad)


def _rope_block(x, tab_ref, base):
    tm = x.shape[0]

    def tab(i):
        t = tab_ref[base + i]
        return jnp.broadcast_to(t[:, None, :], (tm // SUBLANES, SUBLANES, HEAD_PAD)).reshape(tm, HEAD_PAD)

    return (x * tab(0) + pltpu.roll(x, HEAD_PAD - ROPE_AXIS // 2, 1) * tab(1)
            + pltpu.roll(x, ROPE_AXIS // 2, 1) * tab(2))


def _mla_prep_kernel(qc_ref, kvc_ref, misc_ref, tab_ref, qg_ref, wq_ref, kg_ref, wk_ref, wv_ref,
                     q_ref, k_ref, v_ref):
    qc = qc_ref[...]
    qn = qc * lax.rsqrt(jnp.mean(qc * qc, -1, keepdims=True) + 1e-6) * qg_ref[...]
    q = jnp.dot(qn.astype(BF16), wq_ref[...], preferred_element_type=F32)
    kvc = kvc_ref[...]
    kvn = (kvc * lax.rsqrt(jnp.mean(kvc * kvc, -1, keepdims=True) + 1e-6) * kg_ref[...]).astype(BF16)
    k = jnp.dot(kvn, wk_ref[...], preferred_element_type=F32)
    v_ref[...] = jnp.dot(kvn, wv_ref[...], preferred_element_type=F32).astype(BF16)
    kr = _rope_block(misc_ref[...], tab_ref, 3)
    for h in range(MLA_H):
        sl = slice(h * HEAD_PAD, (h + 1) * HEAD_PAD)
        q_ref[:, sl] = _rope_block(q[:, sl], tab_ref, 0).astype(BF16)
        k_ref[:, sl] = (k[:, sl] + kr).astype(BF16)


def _mla_prep(z, tabs, qg, wq_pad, kg, wk_pad, wv):
    n = z.shape[0]
    tm = ROW_TILE
    tt = tm // SUBLANES
    const = lambda i: (0, 0)
    return pl.pallas_call(
        _mla_prep_kernel,
        grid=(n // tm,),
        in_specs=[pl.BlockSpec((tm, MLA_QR), lambda i: (i, Z_QC // MLA_QR)),
                  pl.BlockSpec((tm, MLA_KVR), lambda i: (i, Z_KVC // MLA_KVR)),
                  pl.BlockSpec((tm, 128), lambda i: (i, Z_MISC // 128)),
                  pl.BlockSpec((6, tt, HEAD_PAD), lambda i: (0, i, 0)),
                  pl.BlockSpec((1, MLA_QR), const),
                  pl.BlockSpec((MLA_QR, MLA_H * HEAD_PAD), const),
                  pl.BlockSpec((1, MLA_KVR), const),
                  pl.BlockSpec((MLA_KVR, MLA_H * HEAD_PAD), const),
                  pl.BlockSpec((MLA_KVR, MLA_W), const)],
        out_specs=[pl.BlockSpec((tm, MLA_H * HEAD_PAD), lambda i: (i, 0)),
                   pl.BlockSpec((tm, MLA_H * HEAD_PAD), lambda i: (i, 0)),
                   pl.BlockSpec((tm, MLA_W), lambda i: (i, 0))],
        out_shape=[jax.ShapeDtypeStruct((n, MLA_H * HEAD_PAD), BF16),
                   jax.ShapeDtypeStruct((n, MLA_H * HEAD_PAD), BF16),
                   jax.ShapeDtypeStruct((n, MLA_W), BF16)],
        compiler_params=_cparams(("parallel",)),
        name="mla_prep",
    )(z, z, z, tabs, qg, wq_pad, kg, wk_pad, wv)


def _flash_kernel(q_ref, k_ref, v_ref, o_ref, *, n_main, tk, tail):
    tq = q_ref.shape[0]
    outs = []
    for h in range(2):
        q = q_ref[:, h * HEAD_PAD:(h + 1) * HEAD_PAD]

        def step(start, size, carry, h=h, q=q):
            m, l, acc = carry
            k = k_ref[pl.ds(start, size), h * HEAD_PAD:(h + 1) * HEAD_PAD]
            v = v_ref[pl.ds(start, size), h * MLA_V:(h + 1) * MLA_V]
            s = lax.dot_general(q, k, (((1,), (1,)), ((), ())), preferred_element_type=F32)
            m_new = jnp.maximum(m, jnp.max(s, -1, keepdims=True))
            p = jnp.exp(s - m_new)
            alpha = jnp.exp(m - m_new)
            l = alpha * l + jnp.sum(p, -1, keepdims=True)
            acc = alpha * acc + jnp.dot(p.astype(BF16), v, preferred_element_type=F32)
            return m_new, l, acc

        carry = (jnp.full((tq, 1), -jnp.inf, F32), jnp.zeros((tq, 1), F32), jnp.zeros((tq, MLA_V), F32))
        if n_main:
            carry = lax.fori_loop(
                0, n_main, lambda j, c: step(pl.multiple_of(j * tk, tk), tk, c), carry)
        if tail:
            carry = step(n_main * tk, tail, carry)
        _, l, acc = carry
        outs.append(acc / l)
    o_ref[...] = jnp.concatenate(outs, -1).astype(BF16)


def _flash(q2, k2, v2, *, n_q_tiles, tq, q_tile0, n_keys, key_block):
    lt = q2.shape[0]
    n_main, tail = divmod(n_keys, ATT_TK)
    pairs = MLA_H // 2
    kern = functools.partial(_flash_kernel, n_main=n_main, tk=ATT_TK, tail=tail)
    return pl.pallas_call(
        kern,
        grid=(SUBLANES, pairs, n_q_tiles),
        in_specs=[pl.BlockSpec((tq, 2 * HEAD_PAD), lambda b, p, i: (q_tile0 + i, b * pairs + p)),
                  pl.BlockSpec((n_keys, 2 * HEAD_PAD), lambda b, p, i: (key_block, b * pairs + p)),
                  pl.BlockSpec((n_keys, 2 * MLA_V), lambda b, p, i: (key_block, b * pairs + p))],
        out_specs=pl.BlockSpec((tq, 2 * MLA_V), lambda b, p, i: (i, b * pairs + p)),
        out_shape=jax.ShapeDtypeStruct((n_q_tiles * tq, SUBLANES * MLA_W), BF16),
        compiler_params=_cparams(("parallel", "parallel", "arbitrary")),
        name="mla_flash",
    )(q2, k2, v2)


def _s5_kernel(u_ref, bb_ref, cc_ref, a_ref, y_ref, st_ref, bu_ref, *, steps):
    d = pl.program_id(0)

    @pl.when(pl.program_id(1) == 0)
    def _():
        st_ref[...] = jnp.zeros_like(st_ref)

    bu_ref[...] = jnp.dot(u_ref[...].astype(BF16), bb_ref[0], preferred_element_type=F32)
    ar = jnp.broadcast_to(a_ref[0, 0:1, :], (SUBLANES, S5_STATE))
    ai = jnp.broadcast_to(a_ref[0, 1:2, :], (SUBLANES, S5_STATE))

    def body(i, carry):
        re, im = carry
        t = jnp.where(d == 0, i, steps - 1 - i)
        row = pl.multiple_of(t * SUBLANES, SUBLANES)
        bur = bu_ref[pl.ds(row, SUBLANES), 0:S5_STATE]
        bui = bu_ref[pl.ds(row, SUBLANES), S5_STATE:2 * S5_STATE]
        nre = ar * re - ai * im + bur
        nim = ar * im + ai * re + bui
        bu_ref[pl.ds(row, SUBLANES), 0:S5_STATE] = nre
        bu_ref[pl.ds(row, SUBLANES), S5_STATE:2 * S5_STATE] = nim
        return nre, nim

    re, im = lax.fori_loop(0, steps, body, (st_ref[0], st_ref[1]))
    st_ref[0] = re
    st_ref[1] = im
    y_ref[0] = jnp.dot(bu_ref[...].astype(BF16), cc_ref[0], preferred_element_type=F32)


def _s5_scan(z, bblk, cblk, a2, n_lat_steps, n_steps):
    n = z.shape[0]
    rows = S5_STEPS * SUBLANES
    n_chunks = n_steps // S5_STEPS
    n_lat_chunks = n_lat_steps // S5_STEPS

    def blk(d, s):
        return jnp.where(d == 0, (s + n_lat_chunks) % n_chunks, n_chunks - 1 - s)

    return pl.pallas_call(
        functools.partial(_s5_kernel, steps=S5_STEPS),
        grid=(2, n_chunks),
        in_specs=[pl.BlockSpec((rows, S5_W), lambda d, s: (blk(d, s), Z_U // S5_W)),
                  pl.BlockSpec((1, S5_W, 2 * S5_STATE), lambda d, s: (d, 0, 0)),
                  pl.BlockSpec((1, 2 * S5_STATE, S5_W), lambda d, s: (d, 0, 0)),
                  pl.BlockSpec((1, 2, S5_STATE), lambda d, s: (d, 0, 0))],
        out_specs=pl.BlockSpec((1, rows, S5_W), lambda d, s: (d, blk(d, s), 0)),
        out_shape=jax.ShapeDtypeStruct((2, n, S5_W), F32),
        scratch_shapes=[pltpu.VMEM((2, SUBLANES, S5_STATE), F32),
                        pltpu.VMEM((rows, 2 * S5_STATE), F32)],
        compiler_params=_cparams(("arbitrary", "arbitrary")),
        name="s5_scan",
    )(z, bblk, cblk, a2)


def _shift_rows(x, sh, rev):
    z = jnp.zeros((sh, x.shape[1]), x.dtype)
    if rev:
        return jnp.concatenate([x[sh:], z], 0)
    return jnp.concatenate([z, x[:-sh]], 0)


def _mlstm_kernel(q_ref, k_ref, v_ref, g_ref, qp_ref, qn_ref, kp_ref, kn_ref, cw_ref, cb_ref, gb_ref,
                  h_ref, cst_ref, mst_ref, qk_ref, qs_ref, ks_ref, vs_ref, hs_ref, bc_ref, *, rev, n_lat_chunks, n_chunks):
    T = ML_CHUNK
    rows = T * SUBLANES
    s = pl.program_id(0)
    c = (n_chunks - 1 - s) if rev else (s + n_lat_chunks) % n_chunks

    @pl.when(s == 0)
    def _():
        cst_ref[...] = jnp.zeros_like(cst_ref)
        mst_ref[...] = jnp.zeros_like(mst_ref)

    first = jnp.logical_or(c == 0, c == n_lat_chunks)
    last = jnp.logical_or(c == n_lat_chunks - 1, c == n_chunks - 1)
    keep_p = jnp.where(first, 0.0, 1.0)
    keep_n = jnp.where(last, 0.0, 1.0)
    qk_ref[0:SUBLANES, 0:ML_W] = qp_ref[...] * keep_p
    qk_ref[0:SUBLANES, ML_W:2 * ML_W] = kp_ref[...] * keep_p
    qk_ref[SUBLANES:SUBLANES + rows, 0:ML_W] = q_ref[...]
    qk_ref[SUBLANES:SUBLANES + rows, ML_W:2 * ML_W] = k_ref[...]
    qk_ref[SUBLANES + rows:2 * SUBLANES + rows, 0:ML_W] = qn_ref[...] * keep_n
    qk_ref[SUBLANES + rows:2 * SUBLANES + rows, ML_W:2 * ML_W] = kn_ref[...] * keep_n
    cw = cw_ref[...]
    conv = (cb_ref[...] + qk_ref[0:rows, :] * cw[0:1] + qk_ref[SUBLANES:SUBLANES + rows, :] * cw[1:2]
            + qk_ref[2 * SUBLANES:2 * SUBLANES + rows, :] * cw[2:3])
    conv = conv * jax.nn.sigmoid(conv)
    for j in range(2):
        qs_ref[j] = conv[:, j * 128:(j + 1) * 128]
        ks_ref[j] = conv[:, ML_W + j * 128:ML_W + (j + 1) * 128] * (ML_D ** -0.5)
        vs_ref[j] = v_ref[:, j * 128:(j + 1) * 128]

    g = g_ref[...] + gb_ref[...]
    lane = lax.broadcasted_iota(jnp.int32, g.shape, 1)
    is_f = jnp.logical_and(lane >= 2 * ML_H, lane < 4 * ML_H)
    gm = jnp.where(is_f, jax.nn.log_sigmoid(g), g)
    cum = jnp.where(is_f, gm, 0.0)
    sh = SUBLANES
    while sh < rows:
        cum = cum + _shift_rows(cum, sh, rev)
        sh *= 2
    bc_ref[0] = gm
    bc_ref[1] = cum

    i_lane0 = ML_H if rev else 0
    f_lane0 = 3 * ML_H if rev else 2 * ML_H
    end_row = 0 if rev else T - 1
    ti = lax.broadcasted_iota(jnp.int32, (T, T), 0)
    si = lax.broadcasted_iota(jnp.int32, (T, T), 1)
    order = (si >= ti) if rev else (si <= ti)
    lane_w = lax.broadcasted_iota(jnp.int32, (1, ML_W), 1) // ML_D
    rowhead = lax.broadcasted_iota(jnp.int32, (ML_W, 1), 0) // ML_D
    colhead = lax.broadcasted_iota(jnp.int32, (1, ML_AUG), 1)
    colhead = jnp.where(colhead < ML_W, colhead // ML_D, colhead - ML_W)
    blockmask = rowhead == colhead
    ones_aug = jnp.ones((T, ML_AUG - ML_W), F32)

    def per_batch(b, _):
        sl = pl.ds(b, T, stride=SUBLANES)
        qb = jnp.concatenate([qs_ref[0, sl, :], qs_ref[1, sl, :]], -1)
        kb = jnp.concatenate([ks_ref[0, sl, :], ks_ref[1, sl, :]], -1)
        vb = jnp.concatenate([vs_ref[0, sl, :], vs_ref[1, sl, :]], -1)
        gcol = bc_ref[0, sl, :]
        ccol = bc_ref[1, sl, :]
        grow = gcol.T
        crow = ccol.T
        cst = cst_ref[b]
        qc = jnp.dot(qb.astype(BF16), cst.astype(BF16), preferred_element_type=F32)
        vaug = jnp.concatenate([vb, ones_aug], -1).astype(BF16)
        kb16 = kb.astype(BF16)
        num = jnp.zeros((T, ML_W), F32)
        den = jnp.zeros((T, ML_W), F32)
        stab = jnp.zeros((T, ML_W), F32)
        ke_scale = jnp.zeros((T, ML_W), F32)
        a_col = jnp.zeros((ML_W, 1), F32)
        s_col = jnp.zeros((ML_W, 1), F32)
        for h in range(ML_H):
            hm = lane_w == h
            il, fl = i_lane0 + h, f_lane0 + h
            b_col = ccol[:, fl:fl + 1]
            i_col = gcol[:, il:il + 1]
            b_row = crow[fl:fl + 1, :]
            i_row = grow[il:il + 1, :]
            b_tot = ccol[end_row:end_row + 1, fl:fl + 1]
            m_in = mst_ref[b, h][0:1, 0:1]
            ld = jnp.where(order, b_col - b_row + i_row, -jnp.inf)
            m_t = jnp.maximum(b_col + m_in, jnp.max(ld, -1, keepdims=True))
            dw = jnp.exp(ld - m_t)
            w_inter = jnp.exp(b_col + m_in - m_t)
            qh = jnp.where(hm, qb, 0.0).astype(BF16)
            sc = lax.dot_general(qh, kb16, (((1,), (1,)), ((), ())), preferred_element_type=F32) * dw
            pv = jnp.dot(sc.astype(BF16), vaug[:, 0:ML_W], preferred_element_type=F32)
            den_h = jnp.sum(sc, -1, keepdims=True) + w_inter * qc[:, ML_W + h:ML_W + h + 1]
            num = num + jnp.where(hm, pv + w_inter * qc[:, 0:ML_W], 0.0)
            den = den + jnp.where(hm, den_h, 0.0)
            stab = stab + jnp.where(hm, jnp.exp(-m_t), 0.0)
            w_end = b_tot - b_col + i_col
            m_loc = jnp.max(w_end, 0, keepdims=True)
            m_new = jnp.maximum(b_tot + m_in, m_loc)
            ke_scale = ke_scale + jnp.where(hm, jnp.exp(w_end - m_loc), 0.0)
            a_col = a_col + jnp.where(rowhead == h, jnp.exp(b_tot + m_in - m_new), 0.0)
            s_col = s_col + jnp.where(rowhead == h, jnp.exp(m_loc - m_new), 0.0)
            mst_ref[b, h] = jnp.broadcast_to(m_new, (SUBLANES, 128))
        hout = num / jnp.maximum(jnp.abs(den), stab)
        hs_ref[0, sl, :] = hout[:, 0:128]
        hs_ref[1, sl, :] = hout[:, 128:256]
        ke = (kb * ke_scale).astype(BF16)
        upd = lax.dot_general(ke, vaug, (((0,), (0,)), ((), ())), preferred_element_type=F32)
        cst_ref[b] = a_col * cst + jnp.where(blockmask, s_col * upd, 0.0)
        return 0

    lax.fori_loop(0, SUBLANES, per_batch, 0)
    h_ref[...] = jnp.concatenate([hs_ref[0], hs_ref[1]], -1)


def _mlstm(z, conv_w, conv_b, gate_b_pad, *, rev, n_lat_steps, n_steps):
    n = z.shape[0]
    T = ML_CHUNK
    rows = T * SUBLANES
    n_chunks = n_steps // T
    n_lat_chunks = n_lat_steps // T
    hb = rows // SUBLANES
    n_hblk = n // SUBLANES

    def blk(s):
        return (n_chunks - 1 - s) if rev else (s + n_lat_chunks) % n_chunks

    prev = lambda s: jnp.maximum(blk(s) * hb - 1, 0)
    nxt = lambda s: jnp.minimum((blk(s) + 1) * hb, n_hblk - 1)
    const = lambda s: (0, 0)
    kern = functools.partial(_mlstm_kernel, rev=rev, n_lat_chunks=n_lat_chunks, n_chunks=n_chunks)
    return pl.pallas_call(
        kern,
        grid=(n_chunks,),
        in_specs=[pl.BlockSpec((rows, ML_W), lambda s: (blk(s), Z_MLQ // ML_W)),
                  pl.BlockSpec((rows, ML_W), lambda s: (blk(s), Z_MLK // ML_W)),
                  pl.BlockSpec((rows, ML_W), lambda s: (blk(s), Z_MLV // ML_W)),
                  pl.BlockSpec((rows, 128), lambda s: (blk(s), Z_MISC // 128)),
                  pl.BlockSpec((SUBLANES, ML_W), lambda s: (prev(s), Z_MLQ // ML_W)),
                  pl.BlockSpec((SUBLANES, ML_W), lambda s: (nxt(s), Z_MLQ // ML_W)),
                  pl.BlockSpec((SUBLANES, ML_W), lambda s: (prev(s), Z_MLK // ML_W)),
                  pl.BlockSpec((SUBLANES, ML_W), lambda s: (nxt(s), Z_MLK // ML_W)),
                  pl.BlockSpec((3, 2 * ML_W), const),
                  pl.BlockSpec((1, 2 * ML_W), const),
                  pl.BlockSpec((1, 128), const)],
        out_specs=pl.BlockSpec((rows, ML_W), lambda s: (blk(s), 0)),
        out_shape=jax.ShapeDtypeStruct((n, ML_W), F32),
        scratch_shapes=[pltpu.VMEM((SUBLANES, ML_W, ML_AUG), F32),
                        pltpu.VMEM((SUBLANES, ML_H, SUBLANES, 128), F32),
                        pltpu.VMEM((rows + 2 * SUBLANES, 2 * ML_W), F32),
                        pltpu.VMEM((2, rows, 128), F32),
                        pltpu.VMEM((2, rows, 128), F32),
                        pltpu.VMEM((2, rows, 128), F32),
                        pltpu.VMEM((2, rows, 128), F32),
                        pltpu.VMEM((2, rows, 128), F32)],
        compiler_params=_cparams(("arbitrary",)),
        name="mlstm_rev" if rev else "mlstm_fwd",
    )(z, z, z, z, z, z, z, z, conv_w, conv_b, gate_b_pad)


def _post_kernel(x_ref, g1_ref, sc_ref, sh_ref, u_ref, y_ref, o_ref, attn_ref, hf_ref, hb_ref,
                 d_ref, gw_ref, gb_ref, hn_ref, pm_ref, wo_ref, l1g_ref, l1b_ref, rw_ref,
                 x1_ref, f_ref, lg_ref):
    y = y_ref[0] + y_ref[1] + u_ref[...] * d_ref[...]
    g = jax.nn.gelu(y)
    s5 = g * jax.nn.sigmoid(jnp.dot(g.astype(BF16), gw_ref[...], preferred_element_type=F32) + gb_ref[...])
    hh = jax.nn.sigmoid(o_ref[...]) * (hf_ref[...] + hb_ref[...])
    pm = pm_ref[...]
    mu = jnp.dot(hh, pm, preferred_element_type=F32, precision=lax.Precision.HIGHEST)
    hc = hh - mu
    var = jnp.dot(hc * hc, pm, preferred_element_type=F32, precision=lax.Precision.HIGHEST)
    ml = hc * lax.rsqrt(var + LN_EPS) * hn_ref[...]
    mix = jnp.concatenate([s5.astype(BF16), attn_ref[...], ml.astype(BF16)], -1)
    yo = jnp.dot(mix, wo_ref[...], preferred_element_type=F32)
    x1 = _ln(DEEPNORM_ALPHA * x_ref[...] + _per_batch(yo, g1_ref[0])) * l1g_ref[...] + l1b_ref[...]
    x1_ref[...] = x1
    f = _modulate(_ln(x1), sc_ref[0], sh_ref[0])
    f_ref[...] = f.astype(BF16)
    lg_ref[...] = jnp.dot(f, rw_ref[...], preferred_element_type=F32, precision=lax.Precision.HIGHEST)


def _post(x, mod, z, y2, attn, hf, hb, p, n_lat_tiles):
    n = x.shape[0]
    tm = ROW_TILE
    sel = lambda i: (i >= n_lat_tiles).astype(jnp.int32)
    const = lambda i: (0, 0)
    row = lambda w: pl.BlockSpec((tm, w), lambda i: (i, 0))
    return pl.pallas_call(
        _post_kernel,
        grid=(n // tm,),
        in_specs=[row(D_MODEL),
                  pl.BlockSpec((1, SUBLANES, D_MODEL), lambda i: (sel(i), 0, 2)),
                  pl.BlockSpec((1, SUBLANES, D_MODEL), lambda i: (sel(i), 0, 4)),
                  pl.BlockSpec((1, SUBLANES, D_MODEL), lambda i: (sel(i), 0, 3)),
                  pl.BlockSpec((tm, S5_W), lambda i: (i, Z_U // S5_W)),
                  pl.BlockSpec((2, tm, S5_W), lambda i: (0, i, 0)),
                  pl.BlockSpec((tm, ML_W), lambda i: (i, Z_MLO // ML_W)),
                  row(MLA_W), row(ML_W), row(ML_W),
                  pl.BlockSpec((1, S5_W), const),
                  pl.BlockSpec((S5_W, S5_W), const),
                  pl.BlockSpec((1, S5_W), const),
                  pl.BlockSpec((1, ML_W), const),
                  pl.BlockSpec((ML_W, ML_W), const),
                  pl.BlockSpec((D_MODEL, D_MODEL), const),
                  pl.BlockSpec((1, D_MODEL), const),
                  pl.BlockSpec((1, D_MODEL), const),
                  pl.BlockSpec((D_MODEL, N_EXPERTS), const)],
        out_specs=[row(D_MODEL), row(D_MODEL), row(N_EXPERTS)],
        out_shape=[jax.ShapeDtypeStruct((n, D_MODEL), F32),
                   jax.ShapeDtypeStruct((n, D_MODEL), BF16),
                   jax.ShapeDtypeStruct((n, N_EXPERTS), F32)],
        compiler_params=_cparams(("parallel",)),
        name="post_mix",
    )(x, mod, mod, mod, z, y2, z, attn, hf, hb,
      p["s5_d"], p["glu_w"], p["glu_b"], p["ml_norm_g"], p["head_mean"], p["w_out"], p["ln1_g"], p["ln1_b"],
      p["router_w"])


def _expert_kernel(be_ref, x_ref, g_ref, wg_ref, wu_ref, wd_ref, o_ref):
    x = x_ref[...]
    a = jnp.dot(x, wg_ref[0], preferred_element_type=F32)
    u = jnp.dot(x, wu_ref[0], preferred_element_type=F32)
    hmid = (a * jax.nn.sigmoid(a) * u).astype(BF16)
    o_ref[...] = (jnp.dot(hmid, wd_ref[0], preferred_element_type=F32) * g_ref[...]).astype(o_ref.dtype)


def _experts(blk_expert, xs, gates, wg, wu, wd):
    n_rows = xs.shape[0]
    tb = MOE_BLOCK
    grid_spec = pltpu.PrefetchScalarGridSpec(
        num_scalar_prefetch=1,
        grid=(n_rows // tb,),
        in_specs=[pl.BlockSpec((tb, D_MODEL), lambda i, be: (i, 0)),
                  pl.BlockSpec((tb, 1), lambda i, be: (i, 0)),
                  pl.BlockSpec((1, D_MODEL, EXPERT_F), lambda i, be: (be[i], 0, 0)),
                  pl.BlockSpec((1, D_MODEL, EXPERT_F), lambda i, be: (be[i], 0, 0)),
                  pl.BlockSpec((1, EXPERT_F, D_MODEL), lambda i, be: (be[i], 0, 0))],
        out_specs=pl.BlockSpec((tb, D_MODEL), lambda i, be: (i, 0)),
    )
    return pl.pallas_call(
        _expert_kernel,
        grid_spec=grid_spec,
        out_shape=jax.ShapeDtypeStruct((n_rows, D_MODEL), F32),
        compiler_params=_cparams(("arbitrary",)),
        name="moe_experts",
    )(blk_expert, xs, gates, wg, wu, wd)


def _final_kernel(x1_ref, f_ref, r_ref, g2_ref, wg_ref, wu_ref, wd_ref, lg_ref, lb_ref, o_ref):
    f = f_ref[...]
    a = jnp.dot(f, wg_ref[...], preferred_element_type=F32)
    u = jnp.dot(f, wu_ref[...], preferred_element_type=F32)
    shared = jnp.dot((a * jax.nn.sigmoid(a) * u).astype(BF16), wd_ref[...], preferred_element_type=F32)
    ffn = r_ref[...] + shared
    o_ref[...] = _ln(DEEPNORM_ALPHA * x1_ref[...] + _per_batch(ffn, g2_ref[0])) * lg_ref[...] + lb_ref[...]


def _final(x1, f, routed, mod, p, n_lat_tiles, n_out_tiles):
    tm = ROW_TILE
    sel = lambda i: (i >= n_lat_tiles).astype(jnp.int32)
    const = lambda i: (0, 0)
    row = lambda w: pl.BlockSpec((tm, w), lambda i: (i, 0))
    return pl.pallas_call(
        _final_kernel,
        grid=(n_out_tiles,),
        in_specs=[row(D_MODEL), row(D_MODEL), row(D_MODEL),
                  pl.BlockSpec((1, SUBLANES, D_MODEL), lambda i: (sel(i), 0, 5)),
                  pl.BlockSpec((D_MODEL, EXPERT_F), const),
                  pl.BlockSpec((D_MODEL, EXPERT_F), const),
                  pl.BlockSpec((EXPERT_F, D_MODEL), const),
                  pl.BlockSpec((1, D_MODEL), const),
                  pl.BlockSpec((1, D_MODEL), const)],
        out_specs=row(D_MODEL),
        out_shape=jax.ShapeDtypeStruct((n_out_tiles * tm, D_MODEL), F32),
        compiler_params=_cparams(("parallel",)),
        name="final_ffn",
    )(x1, f, routed, mod, p["sh_w_gate"], p["sh_w_up"], p["sh_w_down"], p["ln2_g"], p["ln2_b"])


def _route(logits, router_bias):
    n_tok = logits.shape[0]
    scores = jax.nn.sigmoid(logits)
    biased = scores + router_bias.astype(F32)
    group_score = jnp.sum(lax.top_k(biased.reshape(n_tok, N_EXPERT_GROUPS, EPG), 2)[0], -1)
    kth = lax.top_k(group_score, TOP_GROUPS)[0][:, -1:]
    allowed = jnp.repeat(group_score >= kth, EPG, axis=1)
    _, top_e = lax.top_k(jnp.where(allowed, biased, -jnp.inf), TOP_K)
    gate = jnp.take_along_axis(scores, top_e, axis=1)
    gate = gate / jnp.sum(gate, -1, keepdims=True) * ROUTED_SCALE
    return top_e, gate


def _dispatch(top_e, gate):
    n_tok = top_e.shape[0]
    n_assign = n_tok * TOP_K
    tb = MOE_BLOCK
    e_flat = top_e.reshape(n_assign)
    order = jnp.argsort(e_flat)
    e_sorted = e_flat[order]
    counts = jnp.bincount(e_flat, length=N_EXPERTS)
    starts = jnp.cumsum(counts) - counts
    padded = (counts + tb - 1) // tb * tb
    pad_end = jnp.cumsum(padded)
    pad_start = pad_end - padded
    dest_sorted = pad_start[e_sorted] + jnp.arange(n_assign) - starts[e_sorted]
    n_rows = -(-(n_assign + N_EXPERTS * (tb - 1)) // tb) * tb
    n_blocks = n_rows // tb
    tok_rows = jnp.zeros((n_rows,), jnp.int32).at[dest_sorted].set((order // TOP_K).astype(jnp.int32))
    gate_rows = jnp.zeros((n_rows,), F32).at[dest_sorted].set(gate.reshape(n_assign)[order])
    blk_expert = jnp.minimum(jnp.searchsorted(pad_end, jnp.arange(n_blocks) * tb, side="right"),
                             N_EXPERTS - 1).astype(jnp.int32)
    dest = jnp.zeros((n_assign,), jnp.int32).at[order].set(dest_sorted.astype(jnp.int32))
    return tok_rows, gate_rows, blk_expert, dest.reshape(n_tok, TOP_K)


def _pad_w_in(w_in):
    s5u, qc, kvc, kr, mq, mk, mv, mo, gates = jnp.split(
        w_in, [256, 512, 640, 672, 928, 1184, 1440, 1696], axis=1)
    misc = jnp.zeros((D_MODEL, 128), w_in.dtype)
    misc = misc.at[:, MISC_GATES:MISC_GATES + 16].set(gates).at[:, MISC_ROPE:MISC_ROPE + MLA_ROPE].set(kr)
    return jnp.concatenate([s5u, qc, mq, mk, mv, mo, kvc, misc], axis=1).astype(BF16)


def _pad_heads(w, width):
    k = w.shape[0]
    w3 = w.reshape(k, MLA_H, width)
    return jnp.pad(w3, ((0, 0), (0, 0), (0, HEAD_PAD - width))).reshape(k, MLA_H * HEAD_PAD)


def _rope_tables(n_lat, n_ctx):
    t = jnp.arange(n_lat)
    half = ROPE_AXIS // 2
    inv_freq = ROPE_BASE ** (-jnp.arange(half, dtype=F32) / half)
    ang_r = (t // GRID_W).astype(F32)[:, None] * inv_freq
    ang_c = (t % GRID_W).astype(F32)[:, None] * inv_freq
    ang = jnp.concatenate([ang_r, ang_r, ang_c, ang_c], -1)
    ang = jnp.concatenate([ang, jnp.zeros((n_ctx, MLA_ROPE), F32)], 0)
    cos, sin = jnp.cos(ang), jnp.sin(ang)
    first = (jnp.arange(MLA_ROPE) % ROPE_AXIS) < half

    def place(v, fill):
        out = jnp.full((v.shape[0], HEAD_PAD), fill, F32)
        return out.at[:, MLA_NOPE:MLA_NOPE + MLA_ROPE].set(v)

    c_q = place(cos, 1.0) * MLA_SCALE
    m_q = place(jnp.where(first, -sin, 0.0), 0.0) * MLA_SCALE
    p_q = place(jnp.where(first, 0.0, sin), 0.0) * MLA_SCALE
    c_k = place(cos, 0.0)
    m_k = place(jnp.where(first, -sin, 0.0), 0.0)
    p_k = place(jnp.where(first, 0.0, sin), 0.0)
    return jnp.stack([c_q, m_q, p_q, c_k, m_k, p_k], 0)


def _s5_params(lam_re, lam_im, log_step, b_re, b_im, c_re, c_im):
    lam = lax.complex(lam_re.astype(F32), lam_im.astype(F32))
    step = jnp.exp(log_step.astype(F32))[..., None]
    a_bar = jnp.exp(lam * step)
    b_mat = lax.complex(b_re.astype(F32), b_im.astype(F32))
    b_bar = ((a_bar - 1.0) / lam)[..., None] * b_mat
    eye = jnp.eye(S5_G, dtype=F32)
    bt = jnp.transpose(b_bar, (0, 1, 3, 2))
    b_blk = lambda v: jnp.einsum("dgcp,gh->dgchp", v, eye).reshape(2, S5_W, S5_STATE)
    bblk = jnp.concatenate([b_blk(bt.real), b_blk(bt.imag)], -1)
    ct = jnp.transpose(lax.complex(c_re.astype(F32), c_im.astype(F32)), (0, 1, 3, 2))
    c_blk = lambda v: jnp.einsum("dgpc,gh->dgphc", v, eye).reshape(2, S5_STATE, S5_W)
    cblk = jnp.concatenate([c_blk(ct.real), -c_blk(ct.imag)], 1)
    a2 = jnp.stack([a_bar.real.reshape(2, S5_STATE), a_bar.imag.reshape(2, S5_STATE)], 1)
    return bblk.astype(BF16), cblk.astype(BF16), a2


def kernel(x, c, ctx, c_ctx, ada_w, ada_b, w_in, s5_lambda_re, s5_lambda_im, s5_log_step, s5_b_re, s5_b_im, s5_c_re, s5_c_im, s5_d, s5_glu_w, s5_glu_b, mla_q_norm, mla_w_q_up, mla_kv_norm, mla_w_kv_up, ml_conv_w, ml_conv_b, ml_gate_b, ml_norm_g, w_out, ln1_g, ln1_b, ln2_g, ln2_b, router_w, router_bias, exp_w_gate, exp_w_up, exp_w_down, sh_w_gate, sh_w_up, sh_w_down):
    bsz, n_lat, d = x.shape
    n_ctx = ctx.shape[1]
    depth = ada_w.shape[0]
    assert bsz == SUBLANES and d == D_MODEL
    assert n_lat % ATT_TQ == 0 and n_lat % ML_CHUNK == 0 and n_ctx % ML_CHUNK == 0
    assert (n_lat * bsz) % ROW_TILE == 0 and (n_ctx * bsz) % ROW_TILE == 0
    n_steps = n_lat + n_ctx
    n_tok = n_steps * bsz
    n_lat_tiles = n_lat * bsz // ROW_TILE

    xs = jnp.concatenate([jnp.transpose(x, (1, 0, 2)), jnp.transpose(ctx, (1, 0, 2))], 0).reshape(n_tok, d)
    cc = jnp.concatenate([c, jnp.broadcast_to(c_ctx[None], (bsz, d))], 0)
    tabs = _rope_tables(n_lat, n_ctx)
    head_mean = jnp.kron(jnp.eye(ML_H, dtype=F32), jnp.full((ML_D, ML_D), 1.0 / ML_D, F32))

    for layer in range(depth):
        last = layer == depth - 1
        mod = _ada_mod(cc, ada_w[layer], ada_b[layer]).reshape(2, bsz, 6 * d)
        z = _inproj(xs, mod, _pad_w_in(w_in[layer]), n_lat_tiles)

        w_kv = mla_w_kv_up[layer].reshape(MLA_KVR, MLA_H, MLA_NOPE + MLA_V)
        wk_pad = _pad_heads(w_kv[:, :, :MLA_NOPE].reshape(MLA_KVR, MLA_H * MLA_NOPE), MLA_NOPE).astype(BF16)
        wv = w_kv[:, :, MLA_NOPE:].reshape(MLA_KVR, MLA_W).astype(BF16)
        wq_pad = _pad_heads(mla_w_q_up[layer], MLA_NOPE + MLA_ROPE).astype(BF16)
        q, k, v = _mla_prep(z, tabs, mla_q_norm[layer].reshape(1, -1), wq_pad,
                            mla_kv_norm[layer].reshape(1, -1), wk_pad, wv)
        q2 = q.reshape(n_steps, bsz * MLA_H * HEAD_PAD)
        k2 = k.reshape(n_steps, bsz * MLA_H * HEAD_PAD)
        v2 = v.reshape(n_steps, bsz * MLA_W)
        attn = _flash(q2, k2, v2, n_q_tiles=n_lat // ATT_TQ, tq=ATT_TQ, q_tile0=0, n_keys=n_steps, key_block=0)
        if last:
            attn_ctx = jnp.zeros((n_ctx, bsz * MLA_W), BF16)
        else:
            attn_ctx = _flash(q2, k2, v2, n_q_tiles=1, tq=n_ctx, q_tile0=n_lat // n_ctx, n_keys=n_ctx,
                              key_block=n_lat // n_ctx)
        attn = jnp.concatenate([attn, attn_ctx], 0).reshape(n_tok, MLA_W)

        bblk, cblk, a2 = _s5_params(s5_lambda_re[layer], s5_lambda_im[layer], s5_log_step[layer],
                                    s5_b_re[layer], s5_b_im[layer], s5_c_re[layer], s5_c_im[layer])
        y2 = _s5_scan(z, bblk, cblk, a2, n_lat, n_steps)

        gate_b_pad = jnp.zeros((1, 128), F32).at[0, :4 * ML_H].set(ml_gate_b[layer].reshape(4 * ML_H))
        cw, cb = ml_conv_w[layer], ml_conv_b[layer].reshape(1, -1)
        hf = _mlstm(z, cw, cb, gate_b_pad, rev=False, n_lat_steps=n_lat, n_steps=n_steps)
        hb = _mlstm(z, cw, cb, gate_b_pad, rev=True, n_lat_steps=n_lat, n_steps=n_steps)

        p = dict(s5_d=s5_d[layer].reshape(1, -1), glu_w=s5_glu_w[layer].astype(BF16),
                 glu_b=s5_glu_b[layer].reshape(1, -1), ml_norm_g=ml_norm_g[layer].reshape(1, -1),
                 head_mean=head_mean, w_out=w_out[layer].astype(BF16), ln1_g=ln1_g[layer].reshape(1, -1),
                 ln1_b=ln1_b[layer].reshape(1, -1), router_w=router_w[layer],
                 sh_w_gate=sh_w_gate[layer].astype(BF16), sh_w_up=sh_w_up[layer].astype(BF16),
                 sh_w_down=sh_w_down[layer].astype(BF16), ln2_g=ln2_g[layer].reshape(1, -1),
                 ln2_b=ln2_b[layer].reshape(1, -1))
        x1, f, logits = _post(xs, mod, z, y2, attn, hf, hb, p, n_lat_tiles)

        top_e, gate = _route(logits, router_bias[layer])
        tok_rows, gate_rows, blk_expert, dest = _dispatch(top_e, gate)
        rows_out = _experts(blk_expert, f[tok_rows], gate_rows[:, None], exp_w_gate[layer].astype(BF16),
                            exp_w_up[layer].astype(BF16), exp_w_down[layer].astype(BF16))
        routed = jnp.sum(rows_out[dest], axis=1)
        n_out_tiles = n_lat_tiles if last else n_tok // ROW_TILE
        xs = _final(x1, f, routed, mod, p, n_lat_tiles, n_out_tiles)

    return jnp.transpose(xs[:n_lat * bsz].reshape(n_lat, bsz, d), (1, 0, 2))
```

```python
import functools
import math

import jax
import jax.numpy as jnp
from jax import lax
from jax.experimental import pallas as pl
from jax.experimental.pallas import tpu as pltpu

F32 = jnp.float32
BF16 = jnp.bfloat16

D_MODEL = 1024
GRID_W = 64
S5_W = 256
S5_GC = 16
S5_G = 16
S5_P = 64
S5_STATE = S5_G * S5_P
MLA_H = 8
MLA_NOPE = 64
MLA_ROPE = 32
MLA_V = 64
MLA_QR = 256
MLA_KVR = 128
MLA_W = MLA_H * MLA_V
MLA_SCALE = (MLA_NOPE + MLA_ROPE) ** -0.5
ROPE_AXIS = MLA_ROPE // 2
ROPE_BASE = 10000.0
HEAD_PAD = 128
ML_H = 4
ML_D = 64
ML_W = 256
ML_AUG = ML_W + 128
N_EXPERTS = 64
TOP_K = 8
N_EXPERT_GROUPS = 8
TOP_GROUPS = 4
EPG = 8
EXPERT_F = 256
ROUTED_SCALE = 2.5
DEPTH = 2
DEEPNORM_ALPHA = (2 * DEPTH) ** 0.25
LN_EPS = 1e-5
SUBLANES = 8

Z_U, Z_QC, Z_MLQ, Z_MLK, Z_MLV, Z_MLO, Z_KVC, Z_MISC = 0, 256, 512, 768, 1024, 1280, 1536, 1664
Z_W = 1792
MISC_GATES = 0
MISC_ROPE = 64

VMEM_LIMIT = 48 * 1024 * 1024
ROW_TILE = 512
S5_STEPS = 64
ML_CHUNK = 256
ATT_TQ = 512
ATT_TK = 384
ATT_QSPLIT = 2
MOE_BLOCK = 256


def _cparams(sem):
    return pltpu.CompilerParams(dimension_semantics=sem, vmem_limit_bytes=VMEM_LIMIT)


def _ln(x):
    mu = jnp.mean(x, -1, keepdims=True)
    xc = x - mu
    var = jnp.mean(xc * xc, -1, keepdims=True)
    return xc * lax.rsqrt(var + LN_EPS)


def _modulate(h, sc, sh):
    tm, d = h.shape
    h3 = h.reshape(tm // SUBLANES, SUBLANES, d)
    return (h3 * (1.0 + sc)[None] + sh[None]).reshape(tm, d)


def _per_batch(v, g):
    tm, d = v.shape
    return (v.reshape(tm // SUBLANES, SUBLANES, d) * g[None]).reshape(tm, d)


def _ada_kernel(c_ref, w_ref, b_ref, o_ref):
    c = c_ref[...]
    s = c * jax.nn.sigmoid(c)
    o_ref[...] = jnp.dot(s.astype(BF16), w_ref[...].astype(BF16), preferred_element_type=F32) + b_ref[...]


def _ada_mod(cc, w, b):
    n = w.shape[1]
    tn = 1536
    return pl.pallas_call(
        _ada_kernel,
        grid=(n // tn,),
        in_specs=[pl.BlockSpec((16, D_MODEL), lambda j: (0, 0)),
                  pl.BlockSpec((D_MODEL, tn), lambda j: (0, j)),
                  pl.BlockSpec((1, tn), lambda j: (0, j))],
        out_specs=pl.BlockSpec((16, tn), lambda j: (0, j)),
        out_shape=jax.ShapeDtypeStruct((16, n), F32),
        compiler_params=_cparams(("arbitrary",)),
        name="ada_mod",
    )(cc, w, b.reshape(1, n))


def _inproj_kernel(x_ref, sc_ref, sh_ref, w_ref, z_ref):
    h = _modulate(_ln(x_ref[...]), sc_ref[0], sh_ref[0])
    z_ref[...] = jnp.dot(h.astype(BF16), w_ref[...], preferred_element_type=F32)


def _inproj(x, mod, w_pad, n_lat_tiles):
    n = x.shape[0]
    tm = ROW_TILE
    sel = lambda i: (i >= n_lat_tiles).astype(jnp.int32)
    return pl.pallas_call(
        _inproj_kernel,
        grid=(n // tm,),
        in_specs=[pl.BlockSpec((tm, D_MODEL), lambda i: (i, 0)),
                  pl.BlockSpec((1, SUBLANES, D_MODEL), lambda i: (sel(i), 0, 1)),
                  pl.BlockSpec((1, SUBLANES, D_MODEL), lambda i: (sel(i), 0, 0)),
                  pl.BlockSpec((D_MODEL, Z_W), lambda i: (0, 0))],
        out_specs=pl.BlockSpec((tm, Z_W), lambda i: (i, 0)),
        out_shape=jax.ShapeDtypeStruct((n, Z_W), F32),
        compiler_params=_cparams(("parallel",)),
        name="inproj",
    )(x, mod, mod, w_pad)


def _rope_block(x, tab_ref, base):
    tm = x.shape[0]

    def tab(i):
        t = tab_ref[base + i]
        return jnp.broadcast_to(t[:, None, :], (tm // SUBLANES, SUBLANES, HEAD_PAD)).reshape(tm, HEAD_PAD)

    return (x * tab(0) + pltpu.roll(x, HEAD_PAD - ROPE_AXIS // 2, 1) * tab(1)
            + pltpu.roll(x, ROPE_AXIS // 2, 1) * tab(2))


def _mla_prep_kernel(qc_ref, kvc_ref, misc_ref, tab_ref, qg_ref, wq_ref, kg_ref, wk_ref, wv_ref,
                     q_ref, k_ref, v_ref):
    qc = qc_ref[...]
    qn = qc * lax.rsqrt(jnp.mean(qc * qc, -1, keepdims=True) + 1e-6) * qg_ref[...]
    q = jnp.dot(qn.astype(BF16), wq_ref[...], preferred_element_type=F32)
    kvc = kvc_ref[...]
    kvn = (kvc * lax.rsqrt(jnp.mean(kvc * kvc, -1, keepdims=True) + 1e-6) * kg_ref[...]).astype(BF16)
    k = jnp.dot(kvn, wk_ref[...], preferred_element_type=F32)
    lane = lax.broadcasted_iota(jnp.int32, (1, MLA_H * HEAD_PAD), 1)
    ones_cols = jnp.where(lane % HEAD_PAD >= MLA_V, 1.0, 0.0)
    v_ref[...] = (jnp.dot(kvn, wv_ref[...], preferred_element_type=F32) + ones_cols).astype(BF16)
    kr = _rope_block(misc_ref[...], tab_ref, 3)
    for h in range(MLA_H):
        sl = slice(h * HEAD_PAD, (h + 1) * HEAD_PAD)
        q_ref[:, sl] = _rope_block(q[:, sl], tab_ref, 0).astype(BF16)
        k_ref[:, sl] = (k[:, sl] + kr).astype(BF16)


def _mla_prep(z, tabs, qg, wq_pad, kg, wk_pad, wv):
    n = z.shape[0]
    tm = ROW_TILE
    tt = tm // SUBLANES
    const = lambda i: (0, 0)
    return pl.pallas_call(
        _mla_prep_kernel,
        grid=(n // tm,),
        in_specs=[pl.BlockSpec((tm, MLA_QR), lambda i: (i, Z_QC // MLA_QR)),
                  pl.BlockSpec((tm, MLA_KVR), lambda i: (i, Z_KVC // MLA_KVR)),
                  pl.BlockSpec((tm, 128), lambda i: (i, Z_MISC // 128)),
                  pl.BlockSpec((6, tt, HEAD_PAD), lambda i: (0, i, 0)),
                  pl.BlockSpec((1, MLA_QR), const),
                  pl.BlockSpec((MLA_QR, MLA_H * HEAD_PAD), const),
                  pl.BlockSpec((1, MLA_KVR), const),
                  pl.BlockSpec((MLA_KVR, MLA_H * HEAD_PAD), const),
                  pl.BlockSpec((MLA_KVR, MLA_H * HEAD_PAD), const)],
        out_specs=[pl.BlockSpec((tm, MLA_H * HEAD_PAD), lambda i: (i, 0)),
                   pl.BlockSpec((tm, MLA_H * HEAD_PAD), lambda i: (i, 0)),
                   pl.BlockSpec((tm, MLA_H * HEAD_PAD), lambda i: (i, 0))],
        out_shape=[jax.ShapeDtypeStruct((n, MLA_H * HEAD_PAD), BF16),
                   jax.ShapeDtypeStruct((n, MLA_H * HEAD_PAD), BF16),
                   jax.ShapeDtypeStruct((n, MLA_H * HEAD_PAD), BF16)],
        compiler_params=_cparams(("parallel",)),
        name="mla_prep",
    )(z, z, z, tabs, qg, wq_pad, kg, wk_pad, wv)


def _flash_kernel(qt_ref, k_ref, vt_ref, o_ref, *, n_main, tk, tail, q_split):
    tq = qt_ref.shape[1]
    hq = tq // q_split
    chains = [(h, r) for h in range(2) for r in range(q_split)]
    nc = len(chains)

    def scores(start, size, c):
        h, r = chains[c]
        hs = slice(h * HEAD_PAD, (h + 1) * HEAD_PAD)
        return jnp.dot(k_ref[pl.ds(start, size), hs], qt_ref[hs, r * hq:(r + 1) * hq],
                       preferred_element_type=F32)

    def colmax(s):
        rows = s.shape[0]
        while rows % 16 == 0 and rows > 8:
            rows //= 2
            s = jnp.maximum(s[:rows], s[rows:])
        return jnp.max(s, 0, keepdims=True)

    def update(s, start, size, c, m, acc):
        h, _ = chains[c]
        hs = slice(h * HEAD_PAD, (h + 1) * HEAD_PAD)
        m_new = jnp.maximum(m, colmax(s))
        p = jnp.exp2(s - m_new).astype(BF16)
        acc = jnp.exp2(m - m_new) * acc + jnp.dot(vt_ref[hs, pl.ds(start, size)], p,
                                                  preferred_element_type=F32)
        return m_new, acc

    state = [(jnp.full((1, hq), -jnp.inf, F32), jnp.zeros((HEAD_PAD, hq), F32)) for _ in chains]
    if n_main:
        def body(j, carry):
            s_cur, st = carry[0], list(carry[1:])
            start = pl.multiple_of(j * tk, tk)
            nxt = pl.multiple_of(jnp.minimum(j + 1, n_main - 1) * tk, tk)
            for c in range(nc):
                s_next = scores(start, tk, c + 1) if c + 1 < nc else scores(nxt, tk, 0)
                st[c] = update(s_cur, start, tk, c, *st[c])
                s_cur = s_next
            return (s_cur, *st)

        carry = lax.fori_loop(0, n_main, body, (scores(0, tk, 0), *state))
        state = list(carry[1:])
    if tail:
        for c in range(nc):
            state[c] = update(scores(n_main * tk, tail, c), n_main * tk, tail, c, *state[c])
    for (h, r), (_, acc) in zip(chains, state):
        o_ref[h * MLA_V:(h + 1) * MLA_V, r * hq:(r + 1) * hq] = (
            acc[0:MLA_V] / acc[MLA_V:MLA_V + 1]).astype(BF16)


def _flash(qt, k2, vt, *, n_q_tiles, tq, q_tile0, n_keys, key_block):
    n_main, tail = divmod(n_keys, ATT_TK)
    pairs = MLA_H // 2
    kern = functools.partial(_flash_kernel, n_main=n_main, tk=ATT_TK, tail=tail, q_split=ATT_QSPLIT)
    return pl.pallas_call(
        kern,
        grid=(SUBLANES, pairs, n_q_tiles),
        in_specs=[pl.BlockSpec((2 * HEAD_PAD, tq), lambda b, p, i: (b * pairs + p, q_tile0 + i)),
                  pl.BlockSpec((n_keys, 2 * HEAD_PAD), lambda b, p, i: (key_block, b * pairs + p)),
                  pl.BlockSpec((2 * HEAD_PAD, n_keys), lambda b, p, i: (b * pairs + p, key_block))],
        out_specs=pl.BlockSpec((2 * MLA_V, tq), lambda b, p, i: (b * pairs + p, i)),
        out_shape=jax.ShapeDtypeStruct((SUBLANES * MLA_W, n_q_tiles * tq), BF16),
        compiler_params=_cparams(("parallel", "parallel", "arbitrary")),
        name="mla_flash",
    )(qt, k2, vt)


def _s5_kernel(u_ref, bb_ref, cc_ref, a_ref, y_ref, st_ref, bu_ref, *, steps):
    d = pl.program_id(0)

    @pl.when(pl.program_id(1) == 0)
    def _():
        st_ref[...] = jnp.zeros_like(st_ref)

    bu_ref[...] = jnp.dot(u_ref[...].astype(BF16), bb_ref[0], preferred_element_type=F32)
    ar = jnp.broadcast_to(a_ref[0, 0:1, :], (SUBLANES, S5_STATE))
    ai = jnp.broadcast_to(a_ref[0, 1:2, :], (SUBLANES, S5_STATE))

    def body(i, carry):
        re, im = carry
        t = jnp.where(d == 0, i, steps - 1 - i)
        row = pl.multiple_of(t * SUBLANES, SUBLANES)
        bur = bu_ref[pl.ds(row, SUBLANES), 0:S5_STATE]
        bui = bu_ref[pl.ds(row, SUBLANES), S5_STATE:2 * S5_STATE]
        nre = ar * re - ai * im + bur
        nim = ar * im + ai * re + bui
        bu_ref[pl.ds(row, SUBLANES), 0:S5_STATE] = nre
        bu_ref[pl.ds(row, SUBLANES), S5_STATE:2 * S5_STATE] = nim
        return nre, nim

    re, im = lax.fori_loop(0, steps, body, (st_ref[0], st_ref[1]))
    st_ref[0] = re
    st_ref[1] = im
    y_ref[0] = jnp.dot(bu_ref[...].astype(BF16), cc_ref[0], preferred_element_type=F32)


def _s5_scan(z, bblk, cblk, a2, n_lat_steps, n_steps):
    n = z.shape[0]
    rows = S5_STEPS * SUBLANES
    n_chunks = n_steps // S5_STEPS
    n_lat_chunks = n_lat_steps // S5_STEPS

    def blk(d, s):
        return jnp.where(d == 0, (s + n_lat_chunks) % n_chunks, n_chunks - 1 - s)

    return pl.pallas_call(
        functools.partial(_s5_kernel, steps=S5_STEPS),
        grid=(2, n_chunks),
        in_specs=[pl.BlockSpec((rows, S5_W), lambda d, s: (blk(d, s), Z_U // S5_W)),
                  pl.BlockSpec((1, S5_W, 2 * S5_STATE), lambda d, s: (d, 0, 0)),
                  pl.BlockSpec((1, 2 * S5_STATE, S5_W), lambda d, s: (d, 0, 0)),
                  pl.BlockSpec((1, 2, S5_STATE), lambda d, s: (d, 0, 0))],
        out_specs=pl.BlockSpec((1, rows, S5_W), lambda d, s: (d, blk(d, s), 0)),
        out_shape=jax.ShapeDtypeStruct((2, n, S5_W), F32),
        scratch_shapes=[pltpu.VMEM((2, SUBLANES, S5_STATE), F32),
                        pltpu.VMEM((rows, 2 * S5_STATE), F32)],
        compiler_params=_cparams(("arbitrary", "arbitrary")),
        name="s5_scan",
    )(z, bblk, cblk, a2)


def _shift_rows(x, sh, rev):
    z = jnp.zeros((sh, x.shape[1]), x.dtype)
    if rev:
        return jnp.concatenate([x[sh:], z], 0)
    return jnp.concatenate([z, x[:-sh]], 0)


def _mlstm_kernel(q_ref, k_ref, v_ref, g_ref, qp_ref, qn_ref, kp_ref, kn_ref, cw_ref, cb_ref, gb_ref,
                  h_ref, cst_ref, mst_ref, qk_ref, qs_ref, ks_ref, vs_ref, hs_ref, bc_ref, *, rev, n_lat_chunks, n_chunks):
    T = ML_CHUNK
    rows = T * SUBLANES
    s = pl.program_id(0)
    c = (n_chunks - 1 - s) if rev else (s + n_lat_chunks) % n_chunks

    @pl.when(s == 0)
    def _():
        cst_ref[...] = jnp.zeros_like(cst_ref)
        mst_ref[...] = jnp.zeros_like(mst_ref)

    first = jnp.logical_or(c == 0, c == n_lat_chunks)
    last = jnp.logical_or(c == n_lat_chunks - 1, c == n_chunks - 1)
    keep_p = jnp.where(first, 0.0, 1.0)
    keep_n = jnp.where(last, 0.0, 1.0)
    qk_ref[0:SUBLANES, 0:ML_W] = qp_ref[...] * keep_p
    qk_ref[0:SUBLANES, ML_W:2 * ML_W] = kp_ref[...] * keep_p
    qk_ref[SUBLANES:SUBLANES + rows, 0:ML_W] = q_ref[...]
    qk_ref[SUBLANES:SUBLANES + rows, ML_W:2 * ML_W] = k_ref[...]
    qk_ref[SUBLANES + rows:2 * SUBLANES + rows, 0:ML_W] = qn_ref[...] * keep_n
    qk_ref[SUBLANES + rows:2 * SUBLANES + rows, ML_W:2 * ML_W] = kn_ref[...] * keep_n
    cw = cw_ref[...]
    conv = (cb_ref[...] + qk_ref[0:rows, :] * cw[0:1] + qk_ref[SUBLANES:SUBLANES + rows, :] * cw[1:2]
            + qk_ref[2 * SUBLANES:2 * SUBLANES + rows, :] * cw[2:3])
    conv = conv * jax.nn.sigmoid(conv)
    for j in range(2):
        qs_ref[j] = conv[:, j * 128:(j + 1) * 128]
        ks_ref[j] = conv[:, ML_W + j * 128:ML_W + (j + 1) * 128] * (ML_D ** -0.5)
        vs_ref[j] = v_ref[:, j * 128:(j + 1) * 128]

    g = g_ref[...] + gb_ref[...]
    lane = lax.broadcasted_iota(jnp.int32, g.shape, 1)
    is_f = jnp.logical_and(lane >= 2 * ML_H, lane < 4 * ML_H)
    gm = jnp.where(is_f, jax.nn.log_sigmoid(g), g)
    cum = jnp.where(is_f, gm, 0.0)
    sh = SUBLANES
    while sh < rows:
        cum = cum + _shift_rows(cum, sh, rev)
        sh *= 2
    bc_ref[0] = gm
    bc_ref[1] = cum

    i_lane0 = ML_H if rev else 0
    f_lane0 = 3 * ML_H if rev else 2 * ML_H
    end_row = 0 if rev else T - 1
    ti = lax.broadcasted_iota(jnp.int32, (T, T), 0)
    si = lax.broadcasted_iota(jnp.int32, (T, T), 1)
    order = (si >= ti) if rev else (si <= ti)
    lane_w = lax.broadcasted_iota(jnp.int32, (1, ML_W), 1) // ML_D
    rowhead = lax.broadcasted_iota(jnp.int32, (ML_W, 1), 0) // ML_D
    colhead = lax.broadcasted_iota(jnp.int32, (1, ML_AUG), 1)
    colhead = jnp.where(colhead < ML_W, colhead // ML_D, colhead - ML_W)
    blockmask = rowhead == colhead
    ones_aug = jnp.ones((T, ML_AUG - ML_W), F32)

    def per_batch(b, _):
        sl = pl.ds(b, T, stride=SUBLANES)
        qb = jnp.concatenate([qs_ref[0, sl, :], qs_ref[1, sl, :]], -1)
        kb = jnp.concatenate([ks_ref[0, sl, :], ks_ref[1, sl, :]], -1)
        vb = jnp.concatenate([vs_ref[0, sl, :], vs_ref[1, sl, :]], -1)
        gcol = bc_ref[0, sl, :]
        ccol = bc_ref[1, sl, :]
        grow = gcol.T
        crow = ccol.T
        cst = cst_ref[b]
        qc = jnp.dot(qb.astype(BF16), cst.astype(BF16), preferred_element_type=F32)
        vaug = jnp.concatenate([vb, ones_aug], -1).astype(BF16)
        kb16 = kb.astype(BF16)
        num = jnp.zeros((T, ML_W), F32)
        den = jnp.zeros((T, ML_W), F32)
        stab = jnp.zeros((T, ML_W), F32)
        ke_scale = jnp.zeros((T, ML_W), F32)
        a_col = jnp.zeros((ML_W, 1), F32)
        s_col = jnp.zeros((ML_W, 1), F32)
        for h in range(ML_H):
            hm = lane_w == h
            il, fl = i_lane0 + h, f_lane0 + h
            b_col = ccol[:, fl:fl + 1]
            i_col = gcol[:, il:il + 1]
            b_row = crow[fl:fl + 1, :]
            i_row = grow[il:il + 1, :]
            b_tot = ccol[end_row:end_row + 1, fl:fl + 1]
            m_in = mst_ref[b, h][0:1, 0:1]
            ld = jnp.where(order, b_col - b_row + i_row, -jnp.inf)
            m_t = jnp.maximum(b_col + m_in, jnp.max(ld, -1, keepdims=True))
            dw = jnp.exp(ld - m_t)
            w_inter = jnp.exp(b_col + m_in - m_t)
            qh = jnp.where(hm, qb, 0.0).astype(BF16)
            sc = lax.dot_general(qh, kb16, (((1,), (1,)), ((), ())), preferred_element_type=F32) * dw
            pv = jnp.dot(sc.astype(BF16), vaug[:, 0:ML_W], preferred_element_type=F32)
            den_h = jnp.sum(sc, -1, keepdims=True) + w_inter * qc[:, ML_W + h:ML_W + h + 1]
            num = num + jnp.where(hm, pv + w_inter * qc[:, 0:ML_W], 0.0)
            den = den + jnp.where(hm, den_h, 0.0)
            stab = stab + jnp.where(hm, jnp.exp(-m_t), 0.0)
            w_end = b_tot - b_col + i_col
            m_loc = jnp.max(w_end, 0, keepdims=True)
            m_new = jnp.maximum(b_tot + m_in, m_loc)
            ke_scale = ke_scale + jnp.where(hm, jnp.exp(w_end - m_loc), 0.0)
            a_col = a_col + jnp.where(rowhead == h, jnp.exp(b_tot + m_in - m_new), 0.0)
            s_col = s_col + jnp.where(rowhead == h, jnp.exp(m_loc - m_new), 0.0)
            mst_ref[b, h] = jnp.broadcast_to(m_new, (SUBLANES, 128))
        hout = num / jnp.maximum(jnp.abs(den), stab)
        hs_ref[0, sl, :] = hout[:, 0:128]
        hs_ref[1, sl, :] = hout[:, 128:256]
        ke = (kb * ke_scale).astype(BF16)
        upd = lax.dot_general(ke, vaug, (((0,), (0,)), ((), ())), preferred_element_type=F32)
        cst_ref[b] = a_col * cst + jnp.where(blockmask, s_col * upd, 0.0)
        return 0

    lax.fori_loop(0, SUBLANES, per_batch, 0)
    h_ref[...] = jnp.concatenate([hs_ref[0], hs_ref[1]], -1)


def _mlstm(z, conv_w, conv_b, gate_b_pad, *, rev, n_lat_steps, n_steps):
    n = z.shape[0]
    T = ML_CHUNK
    rows = T * SUBLANES
    n_chunks = n_steps // T
    n_lat_chunks = n_lat_steps // T
    hb = rows // SUBLANES
    n_hblk = n // SUBLANES

    def blk(s):
        return (n_chunks - 1 - s) if rev else (s + n_lat_chunks) % n_chunks

    prev = lambda s: jnp.maximum(blk(s) * hb - 1, 0)
    nxt = lambda s: jnp.minimum((blk(s) + 1) * hb, n_hblk - 1)
    const = lambda s: (0, 0)
    kern = functools.partial(_mlstm_kernel, rev=rev, n_lat_chunks=n_lat_chunks, n_chunks=n_chunks)
    return pl.pallas_call(
        kern,
        grid=(n_chunks,),
        in_specs=[pl.BlockSpec((rows, ML_W), lambda s: (blk(s), Z_MLQ // ML_W)),
                  pl.BlockSpec((rows, ML_W), lambda s: (blk(s), Z_MLK // ML_W)),
                  pl.BlockSpec((rows, ML_W), lambda s: (blk(s), Z_MLV // ML_W)),
                  pl.BlockSpec((rows, 128), lambda s: (blk(s), Z_MISC // 128)),
                  pl.BlockSpec((SUBLANES, ML_W), lambda s: (prev(s), Z_MLQ // ML_W)),
                  pl.BlockSpec((SUBLANES, ML_W), lambda s: (nxt(s), Z_MLQ // ML_W)),
                  pl.BlockSpec((SUBLANES, ML_W), lambda s: (prev(s), Z_MLK // ML_W)),
                  pl.BlockSpec((SUBLANES, ML_W), lambda s: (nxt(s), Z_MLK // ML_W)),
                  pl.BlockSpec((3, 2 * ML_W), const),
                  pl.BlockSpec((1, 2 * ML_W), const),
                  pl.BlockSpec((1, 128), const)],
        out_specs=pl.BlockSpec((rows, ML_W), lambda s: (blk(s), 0)),
        out_shape=jax.ShapeDtypeStruct((n, ML_W), F32),
        scratch_shapes=[pltpu.VMEM((SUBLANES, ML_W, ML_AUG), F32),
                        pltpu.VMEM((SUBLANES, ML_H, SUBLANES, 128), F32),
                        pltpu.VMEM((rows + 2 * SUBLANES, 2 * ML_W), F32),
                        pltpu.VMEM((2, rows, 128), F32),
                        pltpu.VMEM((2, rows, 128), F32),
                        pltpu.VMEM((2, rows, 128), F32),
                        pltpu.VMEM((2, rows, 128), F32),
                        pltpu.VMEM((2, rows, 128), F32)],
        compiler_params=_cparams(("arbitrary",)),
        name="mlstm_rev" if rev else "mlstm_fwd",
    )(z, z, z, z, z, z, z, z, conv_w, conv_b, gate_b_pad)


def _post_kernel(x_ref, g1_ref, sc_ref, sh_ref, u_ref, y_ref, o_ref, attn_ref, hf_ref, hb_ref,
                 d_ref, gw_ref, gb_ref, hn_ref, pm_ref, wo_ref, l1g_ref, l1b_ref, rw_ref,
                 x1_ref, f_ref, lg_ref):
    y = y_ref[0] + y_ref[1] + u_ref[...] * d_ref[...]
    g = jax.nn.gelu(y)
    s5 = g * jax.nn.sigmoid(jnp.dot(g.astype(BF16), gw_ref[...], preferred_element_type=F32) + gb_ref[...])
    hh = jax.nn.sigmoid(o_ref[...]) * (hf_ref[...] + hb_ref[...])
    pm = pm_ref[...]
    mu = jnp.dot(hh, pm, preferred_element_type=F32, precision=lax.Precision.HIGHEST)
    hc = hh - mu
    var = jnp.dot(hc * hc, pm, preferred_element_type=F32, precision=lax.Precision.HIGHEST)
    ml = hc * lax.rsqrt(var + LN_EPS) * hn_ref[...]
    mix = jnp.concatenate([s5.astype(BF16), attn_ref[...], ml.astype(BF16)], -1)
    yo = jnp.dot(mix, wo_ref[...], preferred_element_type=F32)
    x1 = _ln(DEEPNORM_ALPHA * x_ref[...] + _per_batch(yo, g1_ref[0])) * l1g_ref[...] + l1b_ref[...]
    x1_ref[...] = x1
    f = _modulate(_ln(x1), sc_ref[0], sh_ref[0])
    f_ref[...] = f.astype(BF16)
    lg_ref[...] = lax.dot_general(rw_ref[...], f, (((1,), (1,)), ((), ())), preferred_element_type=F32,
                                  precision=lax.Precision.HIGHEST)


def _post(x, mod, z, y2, attn, hf, hb, p, n_lat_tiles):
    n = x.shape[0]
    tm = ROW_TILE
    sel = lambda i: (i >= n_lat_tiles).astype(jnp.int32)
    const = lambda i: (0, 0)
    row = lambda w: pl.BlockSpec((tm, w), lambda i: (i, 0))
    return pl.pallas_call(
        _post_kernel,
        grid=(n // tm,),
        in_specs=[row(D_MODEL),
                  pl.BlockSpec((1, SUBLANES, D_MODEL), lambda i: (sel(i), 0, 2)),
                  pl.BlockSpec((1, SUBLANES, D_MODEL), lambda i: (sel(i), 0, 4)),
                  pl.BlockSpec((1, SUBLANES, D_MODEL), lambda i: (sel(i), 0, 3)),
                  pl.BlockSpec((tm, S5_W), lambda i: (i, Z_U // S5_W)),
                  pl.BlockSpec((2, tm, S5_W), lambda i: (0, i, 0)),
                  pl.BlockSpec((tm, ML_W), lambda i: (i, Z_MLO // ML_W)),
                  row(MLA_W), row(ML_W), row(ML_W),
                  pl.BlockSpec((1, S5_W), const),
                  pl.BlockSpec((S5_W, S5_W), const),
                  pl.BlockSpec((1, S5_W), const),
                  pl.BlockSpec((1, ML_W), const),
                  pl.BlockSpec((ML_W, ML_W), const),
                  pl.BlockSpec((D_MODEL, D_MODEL), const),
                  pl.BlockSpec((1, D_MODEL), const),
                  pl.BlockSpec((1, D_MODEL), const),
                  pl.BlockSpec((N_EXPERTS, D_MODEL), const)],
        out_specs=[row(D_MODEL), row(D_MODEL), pl.BlockSpec((N_EXPERTS, tm), lambda i: (0, i))],
        out_shape=[jax.ShapeDtypeStruct((n, D_MODEL), F32),
                   jax.ShapeDtypeStruct((n, D_MODEL), BF16),
                   jax.ShapeDtypeStruct((N_EXPERTS, n), F32)],
        compiler_params=_cparams(("parallel",)),
        name="post_mix",
    )(x, mod, mod, mod, z, y2, z, attn, hf, hb,
      p["s5_d"], p["glu_w"], p["glu_b"], p["ml_norm_g"], p["head_mean"], p["w_out"], p["ln1_g"], p["ln1_b"],
      p["router_w"])


def _expert_kernel(be_ref, x_ref, wg_ref, wu_ref, wd_ref, o_ref):
    x = x_ref[...]
    a = jnp.dot(x, wg_ref[0], preferred_element_type=F32)
    u = jnp.dot(x, wu_ref[0], preferred_element_type=F32)
    hmid = (a * jax.nn.sigmoid(a) * u).astype(BF16)
    o_ref[...] = jnp.dot(hmid, wd_ref[0], preferred_element_type=F32).astype(o_ref.dtype)


def _experts(blk_expert, xs, wg, wu, wd):
    n_rows = xs.shape[0]
    tb = MOE_BLOCK
    grid_spec = pltpu.PrefetchScalarGridSpec(
        num_scalar_prefetch=1,
        grid=(n_rows // tb,),
        in_specs=[pl.BlockSpec((tb, D_MODEL), lambda i, be: (i, 0)),
                  pl.BlockSpec((1, D_MODEL, EXPERT_F), lambda i, be: (be[i], 0, 0)),
                  pl.BlockSpec((1, D_MODEL, EXPERT_F), lambda i, be: (be[i], 0, 0)),
                  pl.BlockSpec((1, EXPERT_F, D_MODEL), lambda i, be: (be[i], 0, 0))],
        out_specs=pl.BlockSpec((tb, D_MODEL), lambda i, be: (i, 0)),
    )
    return pl.pallas_call(
        _expert_kernel,
        grid_spec=grid_spec,
        out_shape=jax.ShapeDtypeStruct((n_rows, D_MODEL), BF16),
        compiler_params=_cparams(("arbitrary",)),
        name="moe_experts",
    )(blk_expert, xs, wg, wu, wd)


def _final_kernel(x1_ref, f_ref, r_ref, g2_ref, wg_ref, wu_ref, wd_ref, lg_ref, lb_ref, o_ref):
    f = f_ref[...]
    a = jnp.dot(f, wg_ref[...], preferred_element_type=F32)
    u = jnp.dot(f, wu_ref[...], preferred_element_type=F32)
    shared = jnp.dot((a * jax.nn.sigmoid(a) * u).astype(BF16), wd_ref[...], preferred_element_type=F32)
    ffn = r_ref[...] + shared
    o_ref[...] = _ln(DEEPNORM_ALPHA * x1_ref[...] + _per_batch(ffn, g2_ref[0])) * lg_ref[...] + lb_ref[...]


def _final(x1, f, routed, mod, p, n_lat_tiles, n_out_tiles):
    tm = ROW_TILE
    sel = lambda i: (i >= n_lat_tiles).astype(jnp.int32)
    const = lambda i: (0, 0)
    row = lambda w: pl.BlockSpec((tm, w), lambda i: (i, 0))
    return pl.pallas_call(
        _final_kernel,
        grid=(n_out_tiles,),
        in_specs=[row(D_MODEL), row(D_MODEL), row(D_MODEL),
                  pl.BlockSpec((1, SUBLANES, D_MODEL), lambda i: (sel(i), 0, 5)),
                  pl.BlockSpec((D_MODEL, EXPERT_F), const),
                  pl.BlockSpec((D_MODEL, EXPERT_F), const),
                  pl.BlockSpec((EXPERT_F, D_MODEL), const),
                  pl.BlockSpec((1, D_MODEL), const),
                  pl.BlockSpec((1, D_MODEL), const)],
        out_specs=row(D_MODEL),
        out_shape=jax.ShapeDtypeStruct((n_out_tiles * tm, D_MODEL), F32),
        compiler_params=_cparams(("parallel",)),
        name="final_ffn",
    )(x1, f, routed, mod, p["sh_w_gate"], p["sh_w_up"], p["sh_w_down"], p["ln2_g"], p["ln2_b"])


def _route_kernel(lg_ref, bias_ref, tri_ref, e_ref, g_ref, r_ref, cnt_ref, carry_ref):
    @pl.when(pl.program_id(0) == 0)
    def _():
        carry_ref[...] = jnp.zeros_like(carry_ref)

    tm = lg_ref.shape[1]
    neg = -jnp.inf
    s = jax.nn.sigmoid(lg_ref[...])
    g3 = (s + bias_ref[...]).reshape(N_EXPERT_GROUPS, EPG, tm)
    io3 = lax.broadcasted_iota(jnp.int32, (N_EXPERT_GROUPS, EPG, tm), 1)
    m1 = jnp.max(g3, 1, keepdims=True)
    f1 = jnp.min(jnp.where(g3 == m1, io3, EPG), 1, keepdims=True)
    m2 = jnp.max(jnp.where(io3 == f1, neg, g3), 1, keepdims=True)
    gs = m1 + m2
    iog = lax.broadcasted_iota(jnp.int32, (N_EXPERT_GROUPS, 1, tm), 0)
    cur = gs
    kth = gs
    for _ in range(TOP_GROUPS):
        kth = jnp.max(cur, 0, keepdims=True)
        fi = jnp.min(jnp.where(cur == kth, iog, N_EXPERT_GROUPS), 0, keepdims=True)
        cur = jnp.where(iog == fi, neg, cur)
    cand = jnp.where(gs >= kth, g3, neg).reshape(N_EXPERTS, tm)
    io = lax.broadcasted_iota(jnp.int32, (N_EXPERTS, tm), 0)
    memb = jnp.zeros((N_EXPERTS, tm), F32)
    es, gates, hots = [], [], []
    for _ in range(TOP_K):
        mk = jnp.max(cand, 0, keepdims=True)
        ik = jnp.min(jnp.where(cand == mk, io, N_EXPERTS), 0, keepdims=True)
        oh = io == ik
        gates.append(jnp.sum(jnp.where(oh, s, 0.0), 0, keepdims=True))
        es.append(ik)
        hots.append(oh)
        cand = jnp.where(oh, neg, cand)
        memb = memb + jnp.where(oh, 1.0, 0.0)
    gsum = gates[0]
    for gk in gates[1:]:
        gsum = gsum + gk
    g_ref[...] = jnp.concatenate(gates, 0) / gsum * ROUTED_SCALE
    e_ref[...] = jnp.concatenate(es, 0)
    pref = jnp.dot(memb.astype(BF16), tri_ref[...], preferred_element_type=F32) + carry_ref[:, 0:1]
    ranks = [jnp.sum(jnp.where(oh, pref, 0.0), 0, keepdims=True) for oh in hots]
    r_ref[...] = jnp.concatenate(ranks, 0).astype(jnp.int32)
    total = carry_ref[...] + jnp.sum(memb, 1, keepdims=True)
    carry_ref[...] = total
    cnt_ref[...] = total


def _route(logits_t, router_bias):
    n = logits_t.shape[1]
    tm = ROW_TILE
    tri = (jnp.arange(tm)[:, None] < jnp.arange(tm)[None, :]).astype(BF16)
    const = lambda i: (0, 0)
    col = pl.BlockSpec((TOP_K, tm), lambda i: (0, i))
    top_e, gate, rank, cnt = pl.pallas_call(
        _route_kernel,
        grid=(n // tm,),
        in_specs=[pl.BlockSpec((N_EXPERTS, tm), lambda i: (0, i)),
                  pl.BlockSpec((N_EXPERTS, 1), const),
                  pl.BlockSpec((tm, tm), const)],
        out_specs=[col, col, col, pl.BlockSpec((N_EXPERTS, 128), const)],
        out_shape=[jax.ShapeDtypeStruct((TOP_K, n), jnp.int32),
                   jax.ShapeDtypeStruct((TOP_K, n), F32),
                   jax.ShapeDtypeStruct((TOP_K, n), jnp.int32),
                   jax.ShapeDtypeStruct((N_EXPERTS, 128), F32)],
        scratch_shapes=[pltpu.VMEM((N_EXPERTS, 128), F32)],
        compiler_params=_cparams(("arbitrary",)),
        name="moe_route",
    )(logits_t, router_bias.astype(F32).reshape(N_EXPERTS, 1), tri)
    return top_e, gate, rank, cnt[:, 0].astype(jnp.int32)


def _dispatch(top_e, rank, counts):
    n_tok = top_e.shape[1]
    tb = MOE_BLOCK
    padded = (counts + tb - 1) // tb * tb
    pad_end = jnp.cumsum(padded)
    pad_start = pad_end - padded
    dest = pad_start[top_e] + rank
    n_rows = -(-(n_tok * TOP_K + N_EXPERTS * (tb - 1)) // tb) * tb
    n_blocks = n_rows // tb
    blk_expert = jnp.minimum(jnp.searchsorted(pad_end, jnp.arange(n_blocks) * tb, side="right"),
                             N_EXPERTS - 1).astype(jnp.int32)
    return dest.astype(jnp.int32), blk_expert, n_rows


def _pad_w_in(w_in):
    s5u, qc, kvc, kr, mq, mk, mv, mo, gates = jnp.split(
        w_in, [256, 512, 640, 672, 928, 1184, 1440, 1696], axis=1)
    misc = jnp.zeros((D_MODEL, 128), w_in.dtype)
    misc = misc.at[:, MISC_GATES:MISC_GATES + 16].set(gates).at[:, MISC_ROPE:MISC_ROPE + MLA_ROPE].set(kr)
    return jnp.concatenate([s5u, qc, mq, mk, mv, mo, kvc, misc], axis=1).astype(BF16)


def _pad_heads(w, width):
    k = w.shape[0]
    w3 = w.reshape(k, MLA_H, width)
    return jnp.pad(w3, ((0, 0), (0, 0), (0, HEAD_PAD - width))).reshape(k, MLA_H * HEAD_PAD)


def _rope_tables(n_lat, n_ctx):
    t = jnp.arange(n_lat)
    half = ROPE_AXIS // 2
    inv_freq = ROPE_BASE ** (-jnp.arange(half, dtype=F32) / half)
    ang_r = (t // GRID_W).astype(F32)[:, None] * inv_freq
    ang_c = (t % GRID_W).astype(F32)[:, None] * inv_freq
    ang = jnp.concatenate([ang_r, ang_r, ang_c, ang_c], -1)
    ang = jnp.concatenate([ang, jnp.zeros((n_ctx, MLA_ROPE), F32)], 0)
    cos, sin = jnp.cos(ang), jnp.sin(ang)
    first = (jnp.arange(MLA_ROPE) % ROPE_AXIS) < half

    def place(v, fill):
        out = jnp.full((v.shape[0], HEAD_PAD), fill, F32)
        return out.at[:, MLA_NOPE:MLA_NOPE + MLA_ROPE].set(v)

    q_scale = MLA_SCALE * math.log2(math.e)
    c_q = place(cos, 1.0) * q_scale
    m_q = place(jnp.where(first, -sin, 0.0), 0.0) * q_scale
    p_q = place(jnp.where(first, 0.0, sin), 0.0) * q_scale
    c_k = place(cos, 0.0)
    m_k = place(jnp.where(first, -sin, 0.0), 0.0)
    p_k = place(jnp.where(first, 0.0, sin), 0.0)
    return jnp.stack([c_q, m_q, p_q, c_k, m_k, p_k], 0)


def _s5_params(lam_re, lam_im, log_step, b_re, b_im, c_re, c_im):
    lr, li = lam_re.astype(F32), lam_im.astype(F32)
    step = jnp.exp(log_step.astype(F32))[..., None]
    mag = jnp.exp(lr * step)
    ar, ai = mag * jnp.cos(li * step), mag * jnp.sin(li * step)
    nr, ni = ar - 1.0, ai
    den = lr * lr + li * li
    cr, ci = ((nr * lr + ni * li) / den)[..., None], ((ni * lr - nr * li) / den)[..., None]
    br, bi = b_re.astype(F32), b_im.astype(F32)
    bbr, bbi = cr * br - ci * bi, cr * bi + ci * br
    eye = jnp.eye(S5_G, dtype=F32)
    b_blk = lambda v: jnp.einsum("dgcp,gh->dgchp", jnp.transpose(v, (0, 1, 3, 2)), eye).reshape(2, S5_W, S5_STATE)
    bblk = jnp.concatenate([b_blk(bbr), b_blk(bbi)], -1)
    c_blk = lambda v: jnp.einsum("dgpc,gh->dgphc", jnp.transpose(v.astype(F32), (0, 1, 3, 2)), eye).reshape(
        2, S5_STATE, S5_W)
    cblk = jnp.concatenate([c_blk(c_re), -c_blk(c_im)], 1)
    a2 = jnp.stack([ar.reshape(2, S5_STATE), ai.reshape(2, S5_STATE)], 1)
    return bblk.astype(BF16), cblk.astype(BF16), a2


def kernel(x, c, ctx, c_ctx, ada_w, ada_b, w_in, s5_lambda_re, s5_lambda_im, s5_log_step, s5_b_re, s5_b_im, s5_c_re, s5_c_im, s5_d, s5_glu_w, s5_glu_b, mla_q_norm, mla_w_q_up, mla_kv_norm, mla_w_kv_up, ml_conv_w, ml_conv_b, ml_gate_b, ml_norm_g, w_out, ln1_g, ln1_b, ln2_g, ln2_b, router_w, router_bias, exp_w_gate, exp_w_up, exp_w_down, sh_w_gate, sh_w_up, sh_w_down):
    bsz, n_lat, d = x.shape
    n_ctx = ctx.shape[1]
    depth = ada_w.shape[0]
    assert bsz == SUBLANES and d == D_MODEL
    assert n_lat % ATT_TQ == 0 and n_lat % ML_CHUNK == 0 and n_ctx % ML_CHUNK == 0
    assert (n_lat * bsz) % ROW_TILE == 0 and (n_ctx * bsz) % ROW_TILE == 0
    n_steps = n_lat + n_ctx
    n_tok = n_steps * bsz
    n_lat_tiles = n_lat * bsz // ROW_TILE

    xs = jnp.concatenate([jnp.transpose(x, (1, 0, 2)), jnp.transpose(ctx, (1, 0, 2))], 0).reshape(n_tok, d)
    cc = jnp.concatenate([c, jnp.broadcast_to(c_ctx[None], (bsz, d))], 0)
    tabs = _rope_tables(n_lat, n_ctx)
    head_mean = jnp.kron(jnp.eye(ML_H, dtype=F32), jnp.full((ML_D, ML_D), 1.0 / ML_D, F32))

    for layer in range(depth):
        last = layer == depth - 1
        mod = _ada_mod(cc, ada_w[layer], ada_b[layer]).reshape(2, bsz, 6 * d)
        z = _inproj(xs, mod, _pad_w_in(w_in[layer]), n_lat_tiles)

        w_kv = mla_w_kv_up[layer].reshape(MLA_KVR, MLA_H, MLA_NOPE + MLA_V)
        wk_pad = _pad_heads(w_kv[:, :, :MLA_NOPE].reshape(MLA_KVR, MLA_H * MLA_NOPE), MLA_NOPE).astype(BF16)
        wv = _pad_heads(w_kv[:, :, MLA_NOPE:].reshape(MLA_KVR, MLA_W), MLA_V).astype(BF16)
        wq_pad = _pad_heads(mla_w_q_up[layer], MLA_NOPE + MLA_ROPE).astype(BF16)
        q, k, v = _mla_prep(z, tabs, mla_q_norm[layer].reshape(1, -1), wq_pad,
                            mla_kv_norm[layer].reshape(1, -1), wk_pad, wv)
        qt = q.reshape(n_steps, bsz * MLA_H * HEAD_PAD).T
        k2 = k.reshape(n_steps, bsz * MLA_H * HEAD_PAD)
        vt = v.reshape(n_steps, bsz * MLA_H * HEAD_PAD).T
        attn = _flash(qt, k2, vt, n_q_tiles=n_lat // ATT_TQ, tq=ATT_TQ, q_tile0=0, n_keys=n_steps, key_block=0)
        if last:
            attn_ctx = jnp.zeros((bsz * MLA_W, n_ctx), BF16)
        else:
            attn_ctx = _flash(qt, k2, vt, n_q_tiles=1, tq=n_ctx, q_tile0=n_lat // n_ctx, n_keys=n_ctx,
                              key_block=n_lat // n_ctx)
        attn = jnp.concatenate([attn, attn_ctx], 1).T.reshape(n_tok, MLA_W)

        bblk, cblk, a2 = _s5_params(s5_lambda_re[layer], s5_lambda_im[layer], s5_log_step[layer],
                                    s5_b_re[layer], s5_b_im[layer], s5_c_re[layer], s5_c_im[layer])
        y2 = _s5_scan(z, bblk, cblk, a2, n_lat, n_steps)

        gate_b_pad = jnp.zeros((1, 128), F32).at[0, :4 * ML_H].set(ml_gate_b[layer].reshape(4 * ML_H))
        cw, cb = ml_conv_w[layer], ml_conv_b[layer].reshape(1, -1)
        hf = _mlstm(z, cw, cb, gate_b_pad, rev=False, n_lat_steps=n_lat, n_steps=n_steps)
        hb = _mlstm(z, cw, cb, gate_b_pad, rev=True, n_lat_steps=n_lat, n_steps=n_steps)

        p = dict(s5_d=s5_d[layer].reshape(1, -1), glu_w=s5_glu_w[layer].astype(BF16),
                 glu_b=s5_glu_b[layer].reshape(1, -1), ml_norm_g=ml_norm_g[layer].reshape(1, -1),
                 head_mean=head_mean, w_out=w_out[layer].astype(BF16), ln1_g=ln1_g[layer].reshape(1, -1),
                 ln1_b=ln1_b[layer].reshape(1, -1), router_w=router_w[layer].T,
                 sh_w_gate=sh_w_gate[layer].astype(BF16), sh_w_up=sh_w_up[layer].astype(BF16),
                 sh_w_down=sh_w_down[layer].astype(BF16), ln2_g=ln2_g[layer].reshape(1, -1),
                 ln2_b=ln2_b[layer].reshape(1, -1))
        x1, f, logits = _post(xs, mod, z, y2, attn, hf, hb, p, n_lat_tiles)

        top_e, gate, rank, counts = _route(logits, router_bias[layer])
        dest, blk_expert, n_rows = _dispatch(top_e, rank, counts)
        tok_ids = jnp.broadcast_to(jnp.arange(n_tok, dtype=jnp.int32)[None], (TOP_K, n_tok))
        tok_rows = jnp.zeros((n_rows,), jnp.int32).at[dest.reshape(-1)].set(tok_ids.reshape(-1))
        rows_out = _experts(blk_expert, f[tok_rows], exp_w_gate[layer].astype(BF16),
                            exp_w_up[layer].astype(BF16), exp_w_down[layer].astype(BF16))
        routed = jnp.einsum("kn,knd->nd", gate, rows_out[dest].astype(F32))
        n_out_tiles = n_lat_tiles if last else n_tok // ROW_TILE
        xs = _final(x1, f, routed, mod, p, n_lat_tiles, n_out_tiles)

    return jnp.transpose(xs[:n_lat * bsz].reshape(n_lat, bsz, d), (1, 0, 2))
```

```python
import functools
import math

import jax
import jax.numpy as jnp
from jax import lax
from jax.experimental import pallas as pl
from jax.experimental.pallas import tpu as pltpu

F32 = jnp.float32
BF16 = jnp.bfloat16

D_MODEL = 1024
GRID_W = 64
S5_W = 256
S5_GC = 16
S5_G = 16
S5_P = 64
S5_STATE = S5_G * S5_P
MLA_H = 8
MLA_NOPE = 64
MLA_ROPE = 32
MLA_V = 64
MLA_QR = 256
MLA_KVR = 128
MLA_W = MLA_H * MLA_V
MLA_SCALE = (MLA_NOPE + MLA_ROPE) ** -0.5
ROPE_AXIS = MLA_ROPE // 2
ROPE_BASE = 10000.0
HEAD_PAD = 128
ML_H = 4
ML_D = 64
ML_W = 256
ML_AUG = ML_W + 128
N_EXPERTS = 64
TOP_K = 8
N_EXPERT_GROUPS = 8
TOP_GROUPS = 4
EPG = 8
EXPERT_F = 256
ROUTED_SCALE = 2.5
DEPTH = 2
DEEPNORM_ALPHA = (2 * DEPTH) ** 0.25
LN_EPS = 1e-5
SUBLANES = 8

Z_U, Z_QC, Z_MLQ, Z_MLK, Z_MLV, Z_MLO, Z_KVC, Z_MISC = 0, 256, 512, 768, 1024, 1280, 1536, 1664
Z_W = 1792
MISC_GATES = 0
MISC_ROPE = 64

VMEM_LIMIT = 48 * 1024 * 1024
ROW_TILE = 512
S5_STEPS = 64
ML_CHUNK = 256
ATT_TQ = 1024
ATT_TK = 768
ATT_QSPLIT = 2
MOE_BLOCK = 256
PACK_W = D_MODEL // 2


def _cparams(sem):
    return pltpu.CompilerParams(dimension_semantics=sem, vmem_limit_bytes=VMEM_LIMIT)


def _ln(x):
    mu = jnp.mean(x, -1, keepdims=True)
    xc = x - mu
    var = jnp.mean(xc * xc, -1, keepdims=True)
    return xc * lax.rsqrt(var + LN_EPS)


def _pack_pairs(x):
    u = lax.bitcast_convert_type(x.astype(BF16).astype(F32), jnp.uint32)
    half = x.shape[1] // 2
    return (u[:, :half] >> 16) | (u[:, half:] & jnp.uint32(0xFFFF0000))


def _unpack_pairs(w):
    lo = lax.bitcast_convert_type(w << 16, F32)
    hi = lax.bitcast_convert_type(w & jnp.uint32(0xFFFF0000), F32)
    return jnp.concatenate([lo, hi], -1)


def _modulate(h, sc, sh):
    tm, d = h.shape
    h3 = h.reshape(tm // SUBLANES, SUBLANES, d)
    return (h3 * (1.0 + sc)[None] + sh[None]).reshape(tm, d)


def _per_batch(v, g):
    tm, d = v.shape
    return (v.reshape(tm // SUBLANES, SUBLANES, d) * g[None]).reshape(tm, d)


def _ada_kernel(c_ref, w_ref, b_ref, o_ref):
    c = c_ref[...]
    s = c * jax.nn.sigmoid(c)
    o_ref[...] = jnp.dot(s.astype(BF16), w_ref[...].astype(BF16), preferred_element_type=F32) + b_ref[...]


def _ada_mod(cc, w, b):
    n = w.shape[1]
    tn = 1536
    return pl.pallas_call(
        _ada_kernel,
        grid=(n // tn,),
        in_specs=[pl.BlockSpec((16, D_MODEL), lambda j: (0, 0)),
                  pl.BlockSpec((D_MODEL, tn), lambda j: (0, j)),
                  pl.BlockSpec((1, tn), lambda j: (0, j))],
        out_specs=pl.BlockSpec((16, tn), lambda j: (0, j)),
        out_shape=jax.ShapeDtypeStruct((16, n), F32),
        compiler_params=_cparams(("arbitrary",)),
        name="ada_mod",
    )(cc, w, b.reshape(1, n))


def _inproj_kernel(x_ref, sc_ref, sh_ref, w_ref, z_ref):
    h = _modulate(_ln(x_ref[...]), sc_ref[0], sh_ref[0])
    z_ref[...] = jnp.dot(h.astype(BF16), w_ref[...], preferred_element_type=F32)


def _inproj(x, mod, w_pad, n_lat_tiles):
    n = x.shape[0]
    tm = ROW_TILE
    sel = lambda i: (i >= n_lat_tiles).astype(jnp.int32)
    return pl.pallas_call(
        _inproj_kernel,
        grid=(n // tm,),
        in_specs=[pl.BlockSpec((tm, D_MODEL), lambda i: (i, 0)),
                  pl.BlockSpec((1, SUBLANES, D_MODEL), lambda i: (sel(i), 0, 1)),
                  pl.BlockSpec((1, SUBLANES, D_MODEL), lambda i: (sel(i), 0, 0)),
                  pl.BlockSpec((D_MODEL, Z_W), lambda i: (0, 0))],
        out_specs=pl.BlockSpec((tm, Z_W), lambda i: (i, 0)),
        out_shape=jax.ShapeDtypeStruct((n, Z_W), F32),
        compiler_params=_cparams(("parallel",)),
        name="inproj",
    )(x, mod, mod, w_pad)


def _rope_block(x, tab_ref, base):
    tm = x.shape[0]

    def tab(i):
        t = tab_ref[base + i]
        return jnp.broadcast_to(t[:, None, :], (tm // SUBLANES, SUBLANES, HEAD_PAD)).reshape(tm, HEAD_PAD)

    return (x * tab(0) + pltpu.roll(x, HEAD_PAD - ROPE_AXIS // 2, 1) * tab(1)
            + pltpu.roll(x, ROPE_AXIS // 2, 1) * tab(2))


def _mla_prep_kernel(qc_ref, kvc_ref, misc_ref, tab_ref, qg_ref, wq_ref, kg_ref, wk_ref, wv_ref,
                     q_ref, k_ref, v_ref):
    qc = qc_ref[...]
    qn = qc * lax.rsqrt(jnp.mean(qc * qc, -1, keepdims=True) + 1e-6) * qg_ref[...]
    q = jnp.dot(qn.astype(BF16), wq_ref[...], preferred_element_type=F32)
    kvc = kvc_ref[...]
    kvn = (kvc * lax.rsqrt(jnp.mean(kvc * kvc, -1, keepdims=True) + 1e-6) * kg_ref[...]).astype(BF16)
    k = jnp.dot(kvn, wk_ref[...], preferred_element_type=F32)
    lane = lax.broadcasted_iota(jnp.int32, (1, MLA_H * HEAD_PAD), 1)
    ones_cols = jnp.where(lane % HEAD_PAD >= MLA_V, 1.0, 0.0)
    v_ref[...] = (jnp.dot(kvn, wv_ref[...], preferred_element_type=F32) + ones_cols).astype(BF16)
    kr = _rope_block(misc_ref[...], tab_ref, 3)
    for h in range(MLA_H):
        sl = slice(h * HEAD_PAD, (h + 1) * HEAD_PAD)
        q_ref[:, sl] = _rope_block(q[:, sl], tab_ref, 0).astype(BF16)
        k_ref[:, sl] = (k[:, sl] + kr).astype(BF16)


def _mla_prep(z, tabs, qg, wq_pad, kg, wk_pad, wv):
    n = z.shape[0]
    tm = ROW_TILE
    tt = tm // SUBLANES
    const = lambda i: (0, 0)
    return pl.pallas_call(
        _mla_prep_kernel,
        grid=(n // tm,),
        in_specs=[pl.BlockSpec((tm, MLA_QR), lambda i: (i, Z_QC // MLA_QR)),
                  pl.BlockSpec((tm, MLA_KVR), lambda i: (i, Z_KVC // MLA_KVR)),
                  pl.BlockSpec((tm, 128), lambda i: (i, Z_MISC // 128)),
                  pl.BlockSpec((6, tt, HEAD_PAD), lambda i: (0, i, 0)),
                  pl.BlockSpec((1, MLA_QR), const),
                  pl.BlockSpec((MLA_QR, MLA_H * HEAD_PAD), const),
                  pl.BlockSpec((1, MLA_KVR), const),
                  pl.BlockSpec((MLA_KVR, MLA_H * HEAD_PAD), const),
                  pl.BlockSpec((MLA_KVR, MLA_H * HEAD_PAD), const)],
        out_specs=[pl.BlockSpec((tm, MLA_H * HEAD_PAD), lambda i: (i, 0)),
                   pl.BlockSpec((tm, MLA_H * HEAD_PAD), lambda i: (i, 0)),
                   pl.BlockSpec((tm, MLA_H * HEAD_PAD), lambda i: (i, 0))],
        out_shape=[jax.ShapeDtypeStruct((n, MLA_H * HEAD_PAD), BF16),
                   jax.ShapeDtypeStruct((n, MLA_H * HEAD_PAD), BF16),
                   jax.ShapeDtypeStruct((n, MLA_H * HEAD_PAD), BF16)],
        compiler_params=_cparams(("parallel",)),
        name="mla_prep",
    )(z, z, z, tabs, qg, wq_pad, kg, wk_pad, wv)


def _flash_kernel(qt_ref, k_ref, vt_ref, o_ref, *, n_main, tk, tail, q_split):
    tq = qt_ref.shape[1]
    hq = tq // q_split
    chains = [(h, r) for h in range(2) for r in range(q_split)]
    nc = len(chains)

    def scores(start, size, c):
        h, r = chains[c]
        hs = slice(h * HEAD_PAD, (h + 1) * HEAD_PAD)
        return jnp.dot(k_ref[pl.ds(start, size), hs], qt_ref[hs, r * hq:(r + 1) * hq],
                       preferred_element_type=F32)

    def colmax(s):
        rows = s.shape[0]
        while rows % 16 == 0 and rows > 8:
            rows //= 2
            s = jnp.maximum(s[:rows], s[rows:])
        return jnp.max(s, 0, keepdims=True)

    def update(s, start, size, c, m, acc):
        h, _ = chains[c]
        hs = slice(h * HEAD_PAD, (h + 1) * HEAD_PAD)
        m_new = jnp.maximum(m, colmax(s))
        p = jnp.exp2(s - m_new).astype(BF16)
        acc = jnp.exp2(m - m_new) * acc + jnp.dot(vt_ref[hs, pl.ds(start, size)], p,
                                                  preferred_element_type=F32)
        return m_new, acc

    state = [(jnp.full((1, hq), -jnp.inf, F32), jnp.zeros((HEAD_PAD, hq), F32)) for _ in chains]
    if n_main:
        def body(j, carry):
            s_cur, st = carry[0], list(carry[1:])
            start = pl.multiple_of(j * tk, tk)
            nxt = pl.multiple_of(jnp.minimum(j + 1, n_main - 1) * tk, tk)
            for c in range(nc):
                s_next = scores(start, tk, c + 1) if c + 1 < nc else scores(nxt, tk, 0)
                st[c] = update(s_cur, start, tk, c, *st[c])
                s_cur = s_next
            return (s_cur, *st)

        carry = lax.fori_loop(0, n_main, body, (scores(0, tk, 0), *state))
        state = list(carry[1:])
    if tail:
        for c in range(nc):
            state[c] = update(scores(n_main * tk, tail, c), n_main * tk, tail, c, *state[c])
    for (h, r), (_, acc) in zip(chains, state):
        o_ref[h * MLA_V:(h + 1) * MLA_V, r * hq:(r + 1) * hq] = (
            acc[0:MLA_V] / acc[MLA_V:MLA_V + 1]).astype(BF16)


def _flash(qt, k2, vt, *, n_q_tiles, tq, q_tile0, n_keys, key_block):
    n_main, tail = divmod(n_keys, ATT_TK)
    pairs = MLA_H // 2
    kern = functools.partial(_flash_kernel, n_main=n_main, tk=ATT_TK, tail=tail, q_split=ATT_QSPLIT)
    return pl.pallas_call(
        kern,
        grid=(SUBLANES, pairs, n_q_tiles),
        in_specs=[pl.BlockSpec((2 * HEAD_PAD, tq), lambda b, p, i: (b * pairs + p, q_tile0 + i)),
                  pl.BlockSpec((n_keys, 2 * HEAD_PAD), lambda b, p, i: (key_block, b * pairs + p)),
                  pl.BlockSpec((2 * HEAD_PAD, n_keys), lambda b, p, i: (b * pairs + p, key_block))],
        out_specs=pl.BlockSpec((2 * MLA_V, tq), lambda b, p, i: (b * pairs + p, i)),
        out_shape=jax.ShapeDtypeStruct((SUBLANES * MLA_W, n_q_tiles * tq), BF16),
        compiler_params=_cparams(("parallel", "parallel", "arbitrary")),
        name="mla_flash",
    )(qt, k2, vt)


def _s5_kernel(u_ref, bb_ref, cc_ref, a_ref, y_ref, st_ref, bu_ref, *, steps):
    d = pl.program_id(0)

    @pl.when(pl.program_id(1) == 0)
    def _():
        st_ref[...] = jnp.zeros_like(st_ref)

    bu_ref[...] = jnp.dot(u_ref[...].astype(BF16), bb_ref[0], preferred_element_type=F32)
    ar = jnp.broadcast_to(a_ref[0, 0:1, :], (SUBLANES, S5_STATE))
    ai = jnp.broadcast_to(a_ref[0, 1:2, :], (SUBLANES, S5_STATE))

    def body(i, carry):
        re, im = carry
        t = jnp.where(d == 0, i, steps - 1 - i)
        row = pl.multiple_of(t * SUBLANES, SUBLANES)
        bur = bu_ref[pl.ds(row, SUBLANES), 0:S5_STATE]
        bui = bu_ref[pl.ds(row, SUBLANES), S5_STATE:2 * S5_STATE]
        nre = ar * re - ai * im + bur
        nim = ar * im + ai * re + bui
        bu_ref[pl.ds(row, SUBLANES), 0:S5_STATE] = nre
        bu_ref[pl.ds(row, SUBLANES), S5_STATE:2 * S5_STATE] = nim
        return nre, nim

    re, im = lax.fori_loop(0, steps, body, (st_ref[0], st_ref[1]))
    st_ref[0] = re
    st_ref[1] = im
    y_ref[0] = jnp.dot(bu_ref[...].astype(BF16), cc_ref[0], preferred_element_type=F32)


def _s5_scan(z, bblk, cblk, a2, n_lat_steps, n_steps):
    n = z.shape[0]
    rows = S5_STEPS * SUBLANES
    n_chunks = n_steps // S5_STEPS
    n_lat_chunks = n_lat_steps // S5_STEPS

    def blk(d, s):
        return jnp.where(d == 0, (s + n_lat_chunks) % n_chunks, n_chunks - 1 - s)

    return pl.pallas_call(
        functools.partial(_s5_kernel, steps=S5_STEPS),
        grid=(2, n_chunks),
        in_specs=[pl.BlockSpec((rows, S5_W), lambda d, s: (blk(d, s), Z_U // S5_W)),
                  pl.BlockSpec((1, S5_W, 2 * S5_STATE), lambda d, s: (d, 0, 0)),
                  pl.BlockSpec((1, 2 * S5_STATE, S5_W), lambda d, s: (d, 0, 0)),
                  pl.BlockSpec((1, 2, S5_STATE), lambda d, s: (d, 0, 0))],
        out_specs=pl.BlockSpec((1, rows, S5_W), lambda d, s: (d, blk(d, s), 0)),
        out_shape=jax.ShapeDtypeStruct((2, n, S5_W), F32),
        scratch_shapes=[pltpu.VMEM((2, SUBLANES, S5_STATE), F32),
                        pltpu.VMEM((rows, 2 * S5_STATE), F32)],
        compiler_params=_cparams(("arbitrary", "arbitrary")),
        name="s5_scan",
    )(z, bblk, cblk, a2)


def _shift_rows(x, sh, rev):
    z = jnp.zeros((sh, x.shape[1]), x.dtype)
    if rev:
        return jnp.concatenate([x[sh:], z], 0)
    return jnp.concatenate([z, x[:-sh]], 0)


def _mlstm_kernel(q_ref, k_ref, v_ref, g_ref, qp_ref, qn_ref, kp_ref, kn_ref, cw_ref, cb_ref, gb_ref,
                  h_ref, cst_ref, mst_ref, qk_ref, qs_ref, ks_ref, vs_ref, hs_ref, bc_ref, *, rev, n_lat_chunks, n_chunks):
    T = ML_CHUNK
    rows = T * SUBLANES
    s = pl.program_id(0)
    c = (n_chunks - 1 - s) if rev else (s + n_lat_chunks) % n_chunks

    @pl.when(s == 0)
    def _():
        cst_ref[...] = jnp.zeros_like(cst_ref)
        mst_ref[...] = jnp.zeros_like(mst_ref)

    first = jnp.logical_or(c == 0, c == n_lat_chunks)
    last = jnp.logical_or(c == n_lat_chunks - 1, c == n_chunks - 1)
    keep_p = jnp.where(first, 0.0, 1.0)
    keep_n = jnp.where(last, 0.0, 1.0)
    qk_ref[0:SUBLANES, 0:ML_W] = qp_ref[...] * keep_p
    qk_ref[0:SUBLANES, ML_W:2 * ML_W] = kp_ref[...] * keep_p
    qk_ref[SUBLANES:SUBLANES + rows, 0:ML_W] = q_ref[...]
    qk_ref[SUBLANES:SUBLANES + rows, ML_W:2 * ML_W] = k_ref[...]
    qk_ref[SUBLANES + rows:2 * SUBLANES + rows, 0:ML_W] = qn_ref[...] * keep_n
    qk_ref[SUBLANES + rows:2 * SUBLANES + rows, ML_W:2 * ML_W] = kn_ref[...] * keep_n
    cw = cw_ref[...]
    conv = (cb_ref[...] + qk_ref[0:rows, :] * cw[0:1] + qk_ref[SUBLANES:SUBLANES + rows, :] * cw[1:2]
            + qk_ref[2 * SUBLANES:2 * SUBLANES + rows, :] * cw[2:3])
    conv = conv * jax.nn.sigmoid(conv)
    for j in range(2):
        qs_ref[j] = conv[:, j * 128:(j + 1) * 128]
        ks_ref[j] = conv[:, ML_W + j * 128:ML_W + (j + 1) * 128] * (ML_D ** -0.5)
        vs_ref[j] = v_ref[:, j * 128:(j + 1) * 128]

    g = g_ref[...] + gb_ref[...]
    lane = lax.broadcasted_iota(jnp.int32, g.shape, 1)
    is_f = jnp.logical_and(lane >= 2 * ML_H, lane < 4 * ML_H)
    gm = jnp.where(is_f, jax.nn.log_sigmoid(g), g)
    cum = jnp.where(is_f, gm, 0.0)
    sh = SUBLANES
    while sh < rows:
        cum = cum + _shift_rows(cum, sh, rev)
        sh *= 2
    bc_ref[0] = gm
    bc_ref[1] = cum

    i_lane0 = ML_H if rev else 0
    f_lane0 = 3 * ML_H if rev else 2 * ML_H
    end_row = 0 if rev else T - 1
    ti = lax.broadcasted_iota(jnp.int32, (T, T), 0)
    si = lax.broadcasted_iota(jnp.int32, (T, T), 1)
    order = (si >= ti) if rev else (si <= ti)
    lane_w = lax.broadcasted_iota(jnp.int32, (1, ML_W), 1) // ML_D
    rowhead = lax.broadcasted_iota(jnp.int32, (ML_W, 1), 0) // ML_D
    colhead = lax.broadcasted_iota(jnp.int32, (1, ML_AUG), 1)
    colhead = jnp.where(colhead < ML_W, colhead // ML_D, colhead - ML_W)
    blockmask = rowhead == colhead
    ones_aug = jnp.ones((T, ML_AUG - ML_W), F32)

    def per_batch(b, _):
        sl = pl.ds(b, T, stride=SUBLANES)
        qb = jnp.concatenate([qs_ref[0, sl, :], qs_ref[1, sl, :]], -1)
        kb = jnp.concatenate([ks_ref[0, sl, :], ks_ref[1, sl, :]], -1)
        vb = jnp.concatenate([vs_ref[0, sl, :], vs_ref[1, sl, :]], -1)
        gcol = bc_ref[0, sl, :]
        ccol = bc_ref[1, sl, :]
        grow = gcol.T
        crow = ccol.T
        cst = cst_ref[b]
        qc = jnp.dot(qb.astype(BF16), cst.astype(BF16), preferred_element_type=F32)
        vaug = jnp.concatenate([vb, ones_aug], -1).astype(BF16)
        kb16 = kb.astype(BF16)
        num = jnp.zeros((T, ML_W), F32)
        den = jnp.zeros((T, ML_W), F32)
        stab = jnp.zeros((T, ML_W), F32)
        ke_scale = jnp.zeros((T, ML_W), F32)
        a_col = jnp.zeros((ML_W, 1), F32)
        s_col = jnp.zeros((ML_W, 1), F32)
        for h in range(ML_H):
            hm = lane_w == h
            il, fl = i_lane0 + h, f_lane0 + h
            b_col = ccol[:, fl:fl + 1]
            i_col = gcol[:, il:il + 1]
            b_row = crow[fl:fl + 1, :]
            i_row = grow[il:il + 1, :]
            b_tot = ccol[end_row:end_row + 1, fl:fl + 1]
            m_in = mst_ref[b, h][0:1, 0:1]
            ld = jnp.where(order, b_col - b_row + i_row, -jnp.inf)
            m_t = jnp.maximum(b_col + m_in, jnp.max(ld, -1, keepdims=True))
            dw = jnp.exp(ld - m_t)
            w_inter = jnp.exp(b_col + m_in - m_t)
            qh = jnp.where(hm, qb, 0.0).astype(BF16)
            sc = lax.dot_general(qh, kb16, (((1,), (1,)), ((), ())), preferred_element_type=F32) * dw
            pv = jnp.dot(sc.astype(BF16), vaug[:, 0:ML_W], preferred_element_type=F32)
            den_h = jnp.sum(sc, -1, keepdims=True) + w_inter * qc[:, ML_W + h:ML_W + h + 1]
            num = num + jnp.where(hm, pv + w_inter * qc[:, 0:ML_W], 0.0)
            den = den + jnp.where(hm, den_h, 0.0)
            stab = stab + jnp.where(hm, jnp.exp(-m_t), 0.0)
            w_end = b_tot - b_col + i_col
            m_loc = jnp.max(w_end, 0, keepdims=True)
            m_new = jnp.maximum(b_tot + m_in, m_loc)
            ke_scale = ke_scale + jnp.where(hm, jnp.exp(w_end - m_loc), 0.0)
            a_col = a_col + jnp.where(rowhead == h, jnp.exp(b_tot + m_in - m_new), 0.0)
            s_col = s_col + jnp.where(rowhead == h, jnp.exp(m_loc - m_new), 0.0)
            mst_ref[b, h] = jnp.broadcast_to(m_new, (SUBLANES, 128))
        hout = num / jnp.maximum(jnp.abs(den), stab)
        hs_ref[0, sl, :] = hout[:, 0:128]
        hs_ref[1, sl, :] = hout[:, 128:256]
        ke = (kb * ke_scale).astype(BF16)
        upd = lax.dot_general(ke, vaug, (((0,), (0,)), ((), ())), preferred_element_type=F32)
        cst_ref[b] = a_col * cst + jnp.where(blockmask, s_col * upd, 0.0)
        return 0

    lax.fori_loop(0, SUBLANES, per_batch, 0)
    h_ref[...] = jnp.concatenate([hs_ref[0], hs_ref[1]], -1)


def _mlstm(z, conv_w, conv_b, gate_b_pad, *, rev, n_lat_steps, n_steps):
    n = z.shape[0]
    T = ML_CHUNK
    rows = T * SUBLANES
    n_chunks = n_steps // T
    n_lat_chunks = n_lat_steps // T
    hb = rows // SUBLANES
    n_hblk = n // SUBLANES

    def blk(s):
        return (n_chunks - 1 - s) if rev else (s + n_lat_chunks) % n_chunks

    prev = lambda s: jnp.maximum(blk(s) * hb - 1, 0)
    nxt = lambda s: jnp.minimum((blk(s) + 1) * hb, n_hblk - 1)
    const = lambda s: (0, 0)
    kern = functools.partial(_mlstm_kernel, rev=rev, n_lat_chunks=n_lat_chunks, n_chunks=n_chunks)
    return pl.pallas_call(
        kern,
        grid=(n_chunks,),
        in_specs=[pl.BlockSpec((rows, ML_W), lambda s: (blk(s), Z_MLQ // ML_W)),
                  pl.BlockSpec((rows, ML_W), lambda s: (blk(s), Z_MLK // ML_W)),
                  pl.BlockSpec((rows, ML_W), lambda s: (blk(s), Z_MLV // ML_W)),
                  pl.BlockSpec((rows, 128), lambda s: (blk(s), Z_MISC // 128)),
                  pl.BlockSpec((SUBLANES, ML_W), lambda s: (prev(s), Z_MLQ // ML_W)),
                  pl.BlockSpec((SUBLANES, ML_W), lambda s: (nxt(s), Z_MLQ // ML_W)),
                  pl.BlockSpec((SUBLANES, ML_W), lambda s: (prev(s), Z_MLK // ML_W)),
                  pl.BlockSpec((SUBLANES, ML_W), lambda s: (nxt(s), Z_MLK // ML_W)),
                  pl.BlockSpec((3, 2 * ML_W), const),
                  pl.BlockSpec((1, 2 * ML_W), const),
                  pl.BlockSpec((1, 128), const)],
        out_specs=pl.BlockSpec((rows, ML_W), lambda s: (blk(s), 0)),
        out_shape=jax.ShapeDtypeStruct((n, ML_W), F32),
        scratch_shapes=[pltpu.VMEM((SUBLANES, ML_W, ML_AUG), F32),
                        pltpu.VMEM((SUBLANES, ML_H, SUBLANES, 128), F32),
                        pltpu.VMEM((rows + 2 * SUBLANES, 2 * ML_W), F32),
                        pltpu.VMEM((2, rows, 128), F32),
                        pltpu.VMEM((2, rows, 128), F32),
                        pltpu.VMEM((2, rows, 128), F32),
                        pltpu.VMEM((2, rows, 128), F32),
                        pltpu.VMEM((2, rows, 128), F32)],
        compiler_params=_cparams(("arbitrary",)),
        name="mlstm_rev" if rev else "mlstm_fwd",
    )(z, z, z, z, z, z, z, z, conv_w, conv_b, gate_b_pad)


def _post_kernel(x_ref, g1_ref, sc_ref, sh_ref, u_ref, y_ref, o_ref, attn_ref, hf_ref, hb_ref,
                 d_ref, gw_ref, gb_ref, hn_ref, pm_ref, wo_ref, l1g_ref, l1b_ref, rw_ref,
                 x1_ref, f_ref, lg_ref):
    y = y_ref[0] + y_ref[1] + u_ref[...] * d_ref[...]
    g = jax.nn.gelu(y)
    s5 = g * jax.nn.sigmoid(jnp.dot(g.astype(BF16), gw_ref[...], preferred_element_type=F32) + gb_ref[...])
    hh = jax.nn.sigmoid(o_ref[...]) * (hf_ref[...] + hb_ref[...])
    pm = pm_ref[...]
    mu = jnp.dot(hh, pm, preferred_element_type=F32, precision=lax.Precision.HIGHEST)
    hc = hh - mu
    var = jnp.dot(hc * hc, pm, preferred_element_type=F32, precision=lax.Precision.HIGHEST)
    ml = hc * lax.rsqrt(var + LN_EPS) * hn_ref[...]
    mix = jnp.concatenate([s5.astype(BF16), attn_ref[...], ml.astype(BF16)], -1)
    yo = jnp.dot(mix, wo_ref[...], preferred_element_type=F32)
    x1 = _ln(DEEPNORM_ALPHA * x_ref[...] + _per_batch(yo, g1_ref[0])) * l1g_ref[...] + l1b_ref[...]
    x1_ref[...] = x1
    f = _modulate(_ln(x1), sc_ref[0], sh_ref[0])
    f_ref[...] = _pack_pairs(f)
    lg_ref[...] = lax.dot_general(rw_ref[...], f, (((1,), (1,)), ((), ())), preferred_element_type=F32,
                                  precision=lax.Precision.HIGHEST)


def _post(x, mod, z, y2, attn, hf, hb, p, n_lat_tiles):
    n = x.shape[0]
    tm = ROW_TILE
    sel = lambda i: (i >= n_lat_tiles).astype(jnp.int32)
    const = lambda i: (0, 0)
    row = lambda w: pl.BlockSpec((tm, w), lambda i: (i, 0))
    return pl.pallas_call(
        _post_kernel,
        grid=(n // tm,),
        in_specs=[row(D_MODEL),
                  pl.BlockSpec((1, SUBLANES, D_MODEL), lambda i: (sel(i), 0, 2)),
                  pl.BlockSpec((1, SUBLANES, D_MODEL), lambda i: (sel(i), 0, 4)),
                  pl.BlockSpec((1, SUBLANES, D_MODEL), lambda i: (sel(i), 0, 3)),
                  pl.BlockSpec((tm, S5_W), lambda i: (i, Z_U // S5_W)),
                  pl.BlockSpec((2, tm, S5_W), lambda i: (0, i, 0)),
                  pl.BlockSpec((tm, ML_W), lambda i: (i, Z_MLO // ML_W)),
                  row(MLA_W), row(ML_W), row(ML_W),
                  pl.BlockSpec((1, S5_W), const),
                  pl.BlockSpec((S5_W, S5_W), const),
                  pl.BlockSpec((1, S5_W), const),
                  pl.BlockSpec((1, ML_W), const),
                  pl.BlockSpec((ML_W, ML_W), const),
                  pl.BlockSpec((D_MODEL, D_MODEL), const),
                  pl.BlockSpec((1, D_MODEL), const),
                  pl.BlockSpec((1, D_MODEL), const),
                  pl.BlockSpec((N_EXPERTS, D_MODEL), const)],
        out_specs=[row(D_MODEL), row(PACK_W), pl.BlockSpec((N_EXPERTS, tm), lambda i: (0, i))],
        out_shape=[jax.ShapeDtypeStruct((n, D_MODEL), F32),
                   jax.ShapeDtypeStruct((n, PACK_W), jnp.uint32),
                   jax.ShapeDtypeStruct((N_EXPERTS, n), F32)],
        compiler_params=_cparams(("parallel",)),
        name="post_mix",
    )(x, mod, mod, mod, z, y2, z, attn, hf, hb,
      p["s5_d"], p["glu_w"], p["glu_b"], p["ml_norm_g"], p["head_mean"], p["w_out"], p["ln1_g"], p["ln1_b"],
      p["router_w"])


def _expert_kernel(be_ref, x_ref, wg_ref, wu_ref, wd_ref, o_ref):
    x = _unpack_pairs(x_ref[...]).astype(BF16)
    a = jnp.dot(x, wg_ref[0], preferred_element_type=F32)
    u = jnp.dot(x, wu_ref[0], preferred_element_type=F32)
    hmid = (a * jax.nn.sigmoid(a) * u).astype(BF16)
    o_ref[...] = _pack_pairs(jnp.dot(hmid, wd_ref[0], preferred_element_type=F32))


def _experts(blk_expert, xs, wg, wu, wd):
    n_rows = xs.shape[0]
    tb = MOE_BLOCK
    grid_spec = pltpu.PrefetchScalarGridSpec(
        num_scalar_prefetch=1,
        grid=(n_rows // tb,),
        in_specs=[pl.BlockSpec((tb, PACK_W), lambda i, be: (i, 0)),
                  pl.BlockSpec((1, D_MODEL, EXPERT_F), lambda i, be: (be[i], 0, 0)),
                  pl.BlockSpec((1, D_MODEL, EXPERT_F), lambda i, be: (be[i], 0, 0)),
                  pl.BlockSpec((1, EXPERT_F, D_MODEL), lambda i, be: (be[i], 0, 0))],
        out_specs=pl.BlockSpec((tb, PACK_W), lambda i, be: (i, 0)),
    )
    return pl.pallas_call(
        _expert_kernel,
        grid_spec=grid_spec,
        out_shape=jax.ShapeDtypeStruct((n_rows, PACK_W), jnp.uint32),
        compiler_params=_cparams(("arbitrary",)),
        name="moe_experts",
    )(blk_expert, xs, wg, wu, wd)


def _idx_copy(idx_hbm, idx_smem, sem, tile, slot):
    return pltpu.make_async_copy(idx_hbm.at[tile], idx_smem.at[slot], sem.at[slot])


def _dispatch_kernel(pe_ref, pc_ref, f_ref, idx_hbm, xs_hbm, idx_smem, zero_ref, isem, zsem, rsem, *, tm):
    i = pl.program_id(0)
    n = pl.num_programs(0)
    slot = i % 2
    tb = MOE_BLOCK

    def row_copy(t, r):
        return pltpu.make_async_copy(f_ref.at[pl.ds(t, 1)], xs_hbm.at[pl.ds(r, 1)], rsem)

    def pad_copy(e):
        start = pl.multiple_of(pe_ref[e] - tb, tb)
        return pltpu.make_async_copy(zero_ref, xs_hbm.at[pl.ds(start, tb)], zsem)

    @pl.when(i == 0)
    def _():
        _idx_copy(idx_hbm, idx_smem, isem, 0, 0).start()
        zero_ref[...] = jnp.zeros_like(zero_ref)

        def fill(e, c):
            @pl.when(pc_ref[e] > 0)
            def _():
                pad_copy(e).start()
            return c

        def drain(e, c):
            @pl.when(pc_ref[e] > 0)
            def _():
                pad_copy(e).wait()
            return c

        lax.fori_loop(0, N_EXPERTS, fill, 0)
        lax.fori_loop(0, N_EXPERTS, drain, 0)

    _idx_copy(idx_hbm, idx_smem, isem, i, slot).wait()

    @pl.when(i + 1 < n)
    def _():
        _idx_copy(idx_hbm, idx_smem, isem, i + 1, 1 - slot).start()

    def issue(t, c):
        for k in range(TOP_K):
            row_copy(t, idx_smem[slot, k * tm + t]).start()
        return c

    def drain_rows(t, c):
        for k in range(TOP_K):
            row_copy(t, idx_smem[slot, k * tm + t]).wait()
        return c

    lax.fori_loop(0, tm, issue, 0)
    lax.fori_loop(0, tm, drain_rows, 0)


def _dispatch_rows(fpk, idx_tiles, pad_end, padded, n_rows):
    n = fpk.shape[0]
    tm = ROW_TILE
    grid_spec = pltpu.PrefetchScalarGridSpec(
        num_scalar_prefetch=2,
        grid=(n // tm,),
        in_specs=[pl.BlockSpec((tm, PACK_W), lambda i, pe, pc: (i, 0)),
                  pl.BlockSpec(memory_space=pl.ANY)],
        out_specs=pl.BlockSpec(memory_space=pl.ANY),
        scratch_shapes=[pltpu.SMEM((2, TOP_K * tm), jnp.int32),
                        pltpu.VMEM((MOE_BLOCK, PACK_W), jnp.uint32),
                        pltpu.SemaphoreType.DMA((2,)),
                        pltpu.SemaphoreType.DMA,
                        pltpu.SemaphoreType.DMA],
    )
    return pl.pallas_call(
        functools.partial(_dispatch_kernel, tm=tm),
        grid_spec=grid_spec,
        out_shape=jax.ShapeDtypeStruct((n_rows, PACK_W), jnp.uint32),
        compiler_params=_cparams(("arbitrary",)),
        name="moe_dispatch",
    )(pad_end, padded, fpk, idx_tiles)


def _final_kernel(x1_ref, f_ref, gate_ref, g2_ref, wg_ref, wu_ref, wd_ref, lg_ref, lb_ref, idx_hbm, ys_hbm,
                  o_ref, idx_smem, buf_ref, isem, rsem, *, tm):
    i = pl.program_id(0)
    n = pl.num_programs(0)
    slot = i % 2

    def row_copy(k, t, r):
        return pltpu.make_async_copy(ys_hbm.at[pl.ds(r, 1)], buf_ref.at[k, pl.ds(t, 1)], rsem)

    @pl.when(i == 0)
    def _():
        _idx_copy(idx_hbm, idx_smem, isem, 0, 0).start()

    _idx_copy(idx_hbm, idx_smem, isem, i, slot).wait()

    @pl.when(i + 1 < n)
    def _():
        _idx_copy(idx_hbm, idx_smem, isem, i + 1, 1 - slot).start()

    def issue(t, c):
        for k in range(TOP_K):
            row_copy(k, t, idx_smem[slot, k * tm + t]).start()
        return c

    def drain_rows(t, c):
        for k in range(TOP_K):
            row_copy(k, t, idx_smem[slot, k * tm + t]).wait()
        return c

    lax.fori_loop(0, tm, issue, 0)
    f = _unpack_pairs(f_ref[...]).astype(BF16)
    a = jnp.dot(f, wg_ref[...], preferred_element_type=F32)
    u = jnp.dot(f, wu_ref[...], preferred_element_type=F32)
    ffn = jnp.dot((a * jax.nn.sigmoid(a) * u).astype(BF16), wd_ref[...], preferred_element_type=F32)
    lax.fori_loop(0, tm, drain_rows, 0)
    gate = gate_ref[...]
    for k in range(TOP_K):
        ffn = ffn + gate[:, k:k + 1] * _unpack_pairs(buf_ref[k])
    o_ref[...] = _ln(DEEPNORM_ALPHA * x1_ref[...] + _per_batch(ffn, g2_ref[0])) * lg_ref[...] + lb_ref[...]


def _final(x1, fpk, gate_t, idx_tiles, ys, mod, p, n_lat_tiles, n_out_tiles):
    tm = ROW_TILE
    sel = lambda i: (i >= n_lat_tiles).astype(jnp.int32)
    const = lambda i: (0, 0)
    row = lambda w: pl.BlockSpec((tm, w), lambda i: (i, 0))
    return pl.pallas_call(
        functools.partial(_final_kernel, tm=tm),
        grid=(n_out_tiles,),
        in_specs=[row(D_MODEL), row(PACK_W), row(TOP_K),
                  pl.BlockSpec((1, SUBLANES, D_MODEL), lambda i: (sel(i), 0, 5)),
                  pl.BlockSpec((D_MODEL, EXPERT_F), const),
                  pl.BlockSpec((D_MODEL, EXPERT_F), const),
                  pl.BlockSpec((EXPERT_F, D_MODEL), const),
                  pl.BlockSpec((1, D_MODEL), const),
                  pl.BlockSpec((1, D_MODEL), const),
                  pl.BlockSpec(memory_space=pl.ANY),
                  pl.BlockSpec(memory_space=pl.ANY)],
        out_specs=row(D_MODEL),
        out_shape=jax.ShapeDtypeStruct((n_out_tiles * tm, D_MODEL), F32),
        scratch_shapes=[pltpu.SMEM((2, TOP_K * tm), jnp.int32),
                        pltpu.VMEM((TOP_K, tm, PACK_W), jnp.uint32),
                        pltpu.SemaphoreType.DMA((2,)),
                        pltpu.SemaphoreType.DMA],
        compiler_params=_cparams(("arbitrary",)),
        name="final_ffn",
    )(x1, fpk, gate_t, mod, p["sh_w_gate"], p["sh_w_up"], p["sh_w_down"], p["ln2_g"], p["ln2_b"], idx_tiles, ys)


def _route_kernel(lg_ref, bias_ref, tri_ref, e_ref, g_ref, r_ref, cnt_ref, carry_ref):
    @pl.when(pl.program_id(0) == 0)
    def _():
        carry_ref[...] = jnp.zeros_like(carry_ref)

    tm = lg_ref.shape[1]
    neg = -jnp.inf
    s = jax.nn.sigmoid(lg_ref[...])
    g3 = (s + bias_ref[...]).reshape(N_EXPERT_GROUPS, EPG, tm)
    io3 = lax.broadcasted_iota(jnp.int32, (N_EXPERT_GROUPS, EPG, tm), 1)
    m1 = jnp.max(g3, 1, keepdims=True)
    f1 = jnp.min(jnp.where(g3 == m1, io3, EPG), 1, keepdims=True)
    m2 = jnp.max(jnp.where(io3 == f1, neg, g3), 1, keepdims=True)
    gs = m1 + m2
    iog = lax.broadcasted_iota(jnp.int32, (N_EXPERT_GROUPS, 1, tm), 0)
    cur = gs
    kth = gs
    for _ in range(TOP_GROUPS):
        kth = jnp.max(cur, 0, keepdims=True)
        fi = jnp.min(jnp.where(cur == kth, iog, N_EXPERT_GROUPS), 0, keepdims=True)
        cur = jnp.where(iog == fi, neg, cur)
    cand = jnp.where(gs >= kth, g3, neg).reshape(N_EXPERTS, tm)
    io = lax.broadcasted_iota(jnp.int32, (N_EXPERTS, tm), 0)
    memb = jnp.zeros((N_EXPERTS, tm), F32)
    es, gates, hots = [], [], []
    for _ in range(TOP_K):
        mk = jnp.max(cand, 0, keepdims=True)
        ik = jnp.min(jnp.where(cand == mk, io, N_EXPERTS), 0, keepdims=True)
        oh = io == ik
        gates.append(jnp.sum(jnp.where(oh, s, 0.0), 0, keepdims=True))
        es.append(ik)
        hots.append(oh)
        cand = jnp.where(oh, neg, cand)
        memb = memb + jnp.where(oh, 1.0, 0.0)
    gsum = gates[0]
    for gk in gates[1:]:
        gsum = gsum + gk
    g_ref[...] = jnp.concatenate(gates, 0) / gsum * ROUTED_SCALE
    e_ref[...] = jnp.concatenate(es, 0)
    pref = jnp.dot(memb.astype(BF16), tri_ref[...], preferred_element_type=F32) + carry_ref[:, 0:1]
    ranks = [jnp.sum(jnp.where(oh, pref, 0.0), 0, keepdims=True) for oh in hots]
    r_ref[...] = jnp.concatenate(ranks, 0).astype(jnp.int32)
    total = carry_ref[...] + jnp.sum(memb, 1, keepdims=True)
    carry_ref[...] = total
    cnt_ref[...] = total


def _route(logits_t, router_bias):
    n = logits_t.shape[1]
    tm = ROW_TILE
    tri = (jnp.arange(tm)[:, None] < jnp.arange(tm)[None, :]).astype(BF16)
    const = lambda i: (0, 0)
    col = pl.BlockSpec((TOP_K, tm), lambda i: (0, i))
    top_e, gate, rank, cnt = pl.pallas_call(
        _route_kernel,
        grid=(n // tm,),
        in_specs=[pl.BlockSpec((N_EXPERTS, tm), lambda i: (0, i)),
                  pl.BlockSpec((N_EXPERTS, 1), const),
                  pl.BlockSpec((tm, tm), const)],
        out_specs=[col, col, col, pl.BlockSpec((N_EXPERTS, 128), const)],
        out_shape=[jax.ShapeDtypeStruct((TOP_K, n), jnp.int32),
                   jax.ShapeDtypeStruct((TOP_K, n), F32),
                   jax.ShapeDtypeStruct((TOP_K, n), jnp.int32),
                   jax.ShapeDtypeStruct((N_EXPERTS, 128), F32)],
        scratch_shapes=[pltpu.VMEM((N_EXPERTS, 128), F32)],
        compiler_params=_cparams(("arbitrary",)),
        name="moe_route",
    )(logits_t, router_bias.astype(F32).reshape(N_EXPERTS, 1), tri)
    return top_e, gate, rank, cnt[:, 0].astype(jnp.int32)


def _dispatch(top_e, rank, counts):
    n_tok = top_e.shape[1]
    tb = MOE_BLOCK
    padded = (counts + tb - 1) // tb * tb
    pad_end = jnp.cumsum(padded)
    pad_start = pad_end - padded
    dest = (pad_start[top_e] + rank).astype(jnp.int32)
    n_rows = -(-(n_tok * TOP_K + N_EXPERTS * (tb - 1)) // tb) * tb
    n_blocks = n_rows // tb
    blk_start = jnp.arange(n_blocks, dtype=jnp.int32) * tb
    blk_expert = jnp.minimum(jnp.sum((pad_end[None, :] <= blk_start[:, None]).astype(jnp.int32), 1),
                             N_EXPERTS - 1)
    tm = ROW_TILE
    idx_tiles = jnp.transpose(dest.reshape(TOP_K, n_tok // tm, tm), (1, 0, 2)).reshape(n_tok // tm, TOP_K * tm)
    return idx_tiles, blk_expert, pad_end.astype(jnp.int32), padded.astype(jnp.int32), n_rows


def _pad_w_in(w_in):
    s5u, qc, kvc, kr, mq, mk, mv, mo, gates = jnp.split(
        w_in, [256, 512, 640, 672, 928, 1184, 1440, 1696], axis=1)
    misc = jnp.zeros((D_MODEL, 128), w_in.dtype)
    misc = misc.at[:, MISC_GATES:MISC_GATES + 16].set(gates).at[:, MISC_ROPE:MISC_ROPE + MLA_ROPE].set(kr)
    return jnp.concatenate([s5u, qc, mq, mk, mv, mo, kvc, misc], axis=1).astype(BF16)


def _pad_heads(w, width):
    k = w.shape[0]
    w3 = w.reshape(k, MLA_H, width)
    return jnp.pad(w3, ((0, 0), (0, 0), (0, HEAD_PAD - width))).reshape(k, MLA_H * HEAD_PAD)


def _rope_tables(n_lat, n_ctx):
    t = jnp.arange(n_lat)
    half = ROPE_AXIS // 2
    inv_freq = ROPE_BASE ** (-jnp.arange(half, dtype=F32) / half)
    ang_r = (t // GRID_W).astype(F32)[:, None] * inv_freq
    ang_c = (t % GRID_W).astype(F32)[:, None] * inv_freq
    ang = jnp.concatenate([ang_r, ang_r, ang_c, ang_c], -1)
    ang = jnp.concatenate([ang, jnp.zeros((n_ctx, MLA_ROPE), F32)], 0)
    cos, sin = jnp.cos(ang), jnp.sin(ang)
    first = (jnp.arange(MLA_ROPE) % ROPE_AXIS) < half

    def place(v, fill):
        out = jnp.full((v.shape[0], HEAD_PAD), fill, F32)
        return out.at[:, MLA_NOPE:MLA_NOPE + MLA_ROPE].set(v)

    q_scale = MLA_SCALE * math.log2(math.e)
    c_q = place(cos, 1.0) * q_scale
    m_q = place(jnp.where(first, -sin, 0.0), 0.0) * q_scale
    p_q = place(jnp.where(first, 0.0, sin), 0.0) * q_scale
    c_k = place(cos, 0.0)
    m_k = place(jnp.where(first, -sin, 0.0), 0.0)
    p_k = place(jnp.where(first, 0.0, sin), 0.0)
    return jnp.stack([c_q, m_q, p_q, c_k, m_k, p_k], 0)


def _s5_params(lam_re, lam_im, log_step, b_re, b_im, c_re, c_im):
    lr, li = lam_re.astype(F32), lam_im.astype(F32)
    step = jnp.exp(log_step.astype(F32))[..., None]
    mag = jnp.exp(lr * step)
    ar, ai = mag * jnp.cos(li * step), mag * jnp.sin(li * step)
    nr, ni = ar - 1.0, ai
    den = lr * lr + li * li
    cr, ci = ((nr * lr + ni * li) / den)[..., None], ((ni * lr - nr * li) / den)[..., None]
    br, bi = b_re.astype(F32), b_im.astype(F32)
    bbr, bbi = cr * br - ci * bi, cr * bi + ci * br
    eye = jnp.eye(S5_G, dtype=F32)
    b_blk = lambda v: jnp.einsum("dgcp,gh->dgchp", jnp.transpose(v, (0, 1, 3, 2)), eye).reshape(2, S5_W, S5_STATE)
    bblk = jnp.concatenate([b_blk(bbr), b_blk(bbi)], -1)
    c_blk = lambda v: jnp.einsum("dgpc,gh->dgphc", jnp.transpose(v.astype(F32), (0, 1, 3, 2)), eye).reshape(
        2, S5_STATE, S5_W)
    cblk = jnp.concatenate([c_blk(c_re), -c_blk(c_im)], 1)
    a2 = jnp.stack([ar.reshape(2, S5_STATE), ai.reshape(2, S5_STATE)], 1)
    return bblk.astype(BF16), cblk.astype(BF16), a2


def kernel(x, c, ctx, c_ctx, ada_w, ada_b, w_in, s5_lambda_re, s5_lambda_im, s5_log_step, s5_b_re, s5_b_im, s5_c_re, s5_c_im, s5_d, s5_glu_w, s5_glu_b, mla_q_norm, mla_w_q_up, mla_kv_norm, mla_w_kv_up, ml_conv_w, ml_conv_b, ml_gate_b, ml_norm_g, w_out, ln1_g, ln1_b, ln2_g, ln2_b, router_w, router_bias, exp_w_gate, exp_w_up, exp_w_down, sh_w_gate, sh_w_up, sh_w_down):
    bsz, n_lat, d = x.shape
    n_ctx = ctx.shape[1]
    depth = ada_w.shape[0]
    assert bsz == SUBLANES and d == D_MODEL
    assert n_lat % ATT_TQ == 0 and n_lat % ML_CHUNK == 0 and n_ctx % ML_CHUNK == 0
    assert (n_lat * bsz) % ROW_TILE == 0 and (n_ctx * bsz) % ROW_TILE == 0
    n_steps = n_lat + n_ctx
    n_tok = n_steps * bsz
    n_lat_tiles = n_lat * bsz // ROW_TILE

    xs = jnp.concatenate([jnp.transpose(x, (1, 0, 2)), jnp.transpose(ctx, (1, 0, 2))], 0).reshape(n_tok, d)
    cc = jnp.concatenate([c, jnp.broadcast_to(c_ctx[None], (bsz, d))], 0)
    tabs = _rope_tables(n_lat, n_ctx)
    head_mean = jnp.kron(jnp.eye(ML_H, dtype=F32), jnp.full((ML_D, ML_D), 1.0 / ML_D, F32))

    for layer in range(depth):
        last = layer == depth - 1
        mod = _ada_mod(cc, ada_w[layer], ada_b[layer]).reshape(2, bsz, 6 * d)
        z = _inproj(xs, mod, _pad_w_in(w_in[layer]), n_lat_tiles)

        w_kv = mla_w_kv_up[layer].reshape(MLA_KVR, MLA_H, MLA_NOPE + MLA_V)
        wk_pad = _pad_heads(w_kv[:, :, :MLA_NOPE].reshape(MLA_KVR, MLA_H * MLA_NOPE), MLA_NOPE).astype(BF16)
        wv = _pad_heads(w_kv[:, :, MLA_NOPE:].reshape(MLA_KVR, MLA_W), MLA_V).astype(BF16)
        wq_pad = _pad_heads(mla_w_q_up[layer], MLA_NOPE + MLA_ROPE).astype(BF16)
        q, k, v = _mla_prep(z, tabs, mla_q_norm[layer].reshape(1, -1), wq_pad,
                            mla_kv_norm[layer].reshape(1, -1), wk_pad, wv)
        qt = q.reshape(n_steps, bsz * MLA_H * HEAD_PAD).T
        k2 = k.reshape(n_steps, bsz * MLA_H * HEAD_PAD)
        vt = v.reshape(n_steps, bsz * MLA_H * HEAD_PAD).T
        attn = _flash(qt, k2, vt, n_q_tiles=n_lat // ATT_TQ, tq=ATT_TQ, q_tile0=0, n_keys=n_steps, key_block=0)
        if last:
            attn_ctx = jnp.zeros((bsz * MLA_W, n_ctx), BF16)
        else:
            attn_ctx = _flash(qt, k2, vt, n_q_tiles=1, tq=n_ctx, q_tile0=n_lat // n_ctx, n_keys=n_ctx,
                              key_block=n_lat // n_ctx)
        attn = jnp.concatenate([attn, attn_ctx], 1).T.reshape(n_tok, MLA_W)

        bblk, cblk, a2 = _s5_params(s5_lambda_re[layer], s5_lambda_im[layer], s5_log_step[layer],
                                    s5_b_re[layer], s5_b_im[layer], s5_c_re[layer], s5_c_im[layer])
        y2 = _s5_scan(z, bblk, cblk, a2, n_lat, n_steps)

        gate_b_pad = jnp.zeros((1, 128), F32).at[0, :4 * ML_H].set(ml_gate_b[layer].reshape(4 * ML_H))
        cw, cb = ml_conv_w[layer], ml_conv_b[layer].reshape(1, -1)
        hf = _mlstm(z, cw, cb, gate_b_pad, rev=False, n_lat_steps=n_lat, n_steps=n_steps)
        hb = _mlstm(z, cw, cb, gate_b_pad, rev=True, n_lat_steps=n_lat, n_steps=n_steps)

        p = dict(s5_d=s5_d[layer].reshape(1, -1), glu_w=s5_glu_w[layer].astype(BF16),
                 glu_b=s5_glu_b[layer].reshape(1, -1), ml_norm_g=ml_norm_g[layer].reshape(1, -1),
                 head_mean=head_mean, w_out=w_out[layer].astype(BF16), ln1_g=ln1_g[layer].reshape(1, -1),
                 ln1_b=ln1_b[layer].reshape(1, -1), router_w=router_w[layer].T,
                 sh_w_gate=sh_w_gate[layer].astype(BF16), sh_w_up=sh_w_up[layer].astype(BF16),
                 sh_w_down=sh_w_down[layer].astype(BF16), ln2_g=ln2_g[layer].reshape(1, -1),
                 ln2_b=ln2_b[layer].reshape(1, -1))
        x1, fpk, logits = _post(xs, mod, z, y2, attn, hf, hb, p, n_lat_tiles)

        top_e, gate, rank, counts = _route(logits, router_bias[layer])
        idx_tiles, blk_expert, pad_end, padded, n_rows = _dispatch(top_e, rank, counts)
        rows_in = _dispatch_rows(fpk, idx_tiles, pad_end, padded, n_rows)
        rows_out = _experts(blk_expert, rows_in, exp_w_gate[layer].astype(BF16),
                            exp_w_up[layer].astype(BF16), exp_w_down[layer].astype(BF16))
        n_out_tiles = n_lat_tiles if last else n_tok // ROW_TILE
        xs = _final(x1, fpk, gate.T, idx_tiles, rows_out, mod, p, n_lat_tiles, n_out_tiles)

    return jnp.transpose(xs[:n_lat * bsz].reshape(n_lat, bsz, d), (1, 0, 2))
```

```python
import functools
import math

import jax
import jax.numpy as jnp
from jax import lax
from jax.experimental import pallas as pl
from jax.experimental.pallas import tpu as pltpu

F32 = jnp.float32
BF16 = jnp.bfloat16

D_MODEL = 1024
GRID_W = 64
S5_W = 256
S5_GC = 16
S5_G = 16
S5_P = 64
S5_STATE = S5_G * S5_P
MLA_H = 8
MLA_NOPE = 64
MLA_ROPE = 32
MLA_V = 64
MLA_QR = 256
MLA_KVR = 128
MLA_W = MLA_H * MLA_V
MLA_SCALE = (MLA_NOPE + MLA_ROPE) ** -0.5
ROPE_AXIS = MLA_ROPE // 2
ROPE_BASE = 10000.0
HEAD_PAD = 128
ML_H = 4
ML_D = 64
ML_W = 256
ML_AUG = ML_W + 128
N_EXPERTS = 64
TOP_K = 8
N_EXPERT_GROUPS = 8
TOP_GROUPS = 4
EPG = 8
EXPERT_F = 256
ROUTED_SCALE = 2.5
DEPTH = 2
DEEPNORM_ALPHA = (2 * DEPTH) ** 0.25
LN_EPS = 1e-5
SUBLANES = 8

Z_U, Z_QC, Z_MLQ, Z_MLK, Z_MLV, Z_MLO, Z_KVC, Z_MISC = 0, 256, 512, 768, 1024, 1280, 1536, 1664
Z_W = 1792
MISC_GATES = 0
MISC_ROPE = 64

VMEM_LIMIT = 48 * 1024 * 1024
ROW_TILE = 512
S5_STEPS = 64
ML_CHUNK = 256
ATT_TQ = 1024
ATT_TK = 768
ATT_QSPLIT = 2
MOE_BLOCK = 256
ROW_TILE_SHAPE = (SUBLANES, D_MODEL // SUBLANES)


def _cparams(sem):
    return pltpu.CompilerParams(dimension_semantics=sem, vmem_limit_bytes=VMEM_LIMIT)


def _ln(x):
    mu = jnp.mean(x, -1, keepdims=True)
    xc = x - mu
    var = jnp.mean(xc * xc, -1, keepdims=True)
    return xc * lax.rsqrt(var + LN_EPS)


def _store_tile_rows(ref, x):
    for s in range(SUBLANES):
        ref[:, s, :] = x[:, s * 128:(s + 1) * 128]


def _load_tile_rows(ref):
    return jnp.concatenate([ref[:, s, :] for s in range(SUBLANES)], -1)


def _modulate(h, sc, sh):
    tm, d = h.shape
    h3 = h.reshape(tm // SUBLANES, SUBLANES, d)
    return (h3 * (1.0 + sc)[None] + sh[None]).reshape(tm, d)


def _per_batch(v, g):
    tm, d = v.shape
    return (v.reshape(tm // SUBLANES, SUBLANES, d) * g[None]).reshape(tm, d)


def _ada_kernel(c_ref, w_ref, b_ref, o_ref):
    c = c_ref[...]
    s = c * jax.nn.sigmoid(c)
    o_ref[...] = jnp.dot(s.astype(BF16), w_ref[...].astype(BF16), preferred_element_type=F32) + b_ref[...]


def _ada_mod(cc, w, b):
    n = w.shape[1]
    tn = 1536
    return pl.pallas_call(
        _ada_kernel,
        grid=(n // tn,),
        in_specs=[pl.BlockSpec((16, D_MODEL), lambda j: (0, 0)),
                  pl.BlockSpec((D_MODEL, tn), lambda j: (0, j)),
                  pl.BlockSpec((1, tn), lambda j: (0, j))],
        out_specs=pl.BlockSpec((16, tn), lambda j: (0, j)),
        out_shape=jax.ShapeDtypeStruct((16, n), F32),
        compiler_params=_cparams(("arbitrary",)),
        name="ada_mod",
    )(cc, w, b.reshape(1, n))


def _inproj_kernel(x_ref, sc_ref, sh_ref, w_ref, z_ref):
    h = _modulate(_ln(x_ref[...]), sc_ref[0], sh_ref[0])
    z_ref[...] = jnp.dot(h.astype(BF16), w_ref[...], preferred_element_type=F32)


def _inproj(x, mod, w_pad, n_lat_tiles):
    n = x.shape[0]
    tm = ROW_TILE
    sel = lambda i: (i >= n_lat_tiles).astype(jnp.int32)
    return pl.pallas_call(
        _inproj_kernel,
        grid=(n // tm,),
        in_specs=[pl.BlockSpec((tm, D_MODEL), lambda i: (i, 0)),
                  pl.BlockSpec((1, SUBLANES, D_MODEL), lambda i: (sel(i), 0, 1)),
                  pl.BlockSpec((1, SUBLANES, D_MODEL), lambda i: (sel(i), 0, 0)),
                  pl.BlockSpec((D_MODEL, Z_W), lambda i: (0, 0))],
        out_specs=pl.BlockSpec((tm, Z_W), lambda i: (i, 0)),
        out_shape=jax.ShapeDtypeStruct((n, Z_W), F32),
        compiler_params=_cparams(("parallel",)),
        name="inproj",
    )(x, mod, mod, w_pad)


def _rope_block(x, tab_ref, base):
    tm = x.shape[0]

    def tab(i):
        t = tab_ref[base + i]
        return jnp.broadcast_to(t[:, None, :], (tm // SUBLANES, SUBLANES, HEAD_PAD)).reshape(tm, HEAD_PAD)

    return (x * tab(0) + pltpu.roll(x, HEAD_PAD - ROPE_AXIS // 2, 1) * tab(1)
            + pltpu.roll(x, ROPE_AXIS // 2, 1) * tab(2))


def _mla_prep_kernel(qc_ref, kvc_ref, misc_ref, tab_ref, qg_ref, wq_ref, kg_ref, wk_ref, wv_ref,
                     q_ref, k_ref, v_ref):
    qc = qc_ref[...]
    qn = qc * lax.rsqrt(jnp.mean(qc * qc, -1, keepdims=True) + 1e-6) * qg_ref[...]
    q = jnp.dot(qn.astype(BF16), wq_ref[...], preferred_element_type=F32)
    kvc = kvc_ref[...]
    kvn = (kvc * lax.rsqrt(jnp.mean(kvc * kvc, -1, keepdims=True) + 1e-6) * kg_ref[...]).astype(BF16)
    k = jnp.dot(kvn, wk_ref[...], preferred_element_type=F32)
    lane = lax.broadcasted_iota(jnp.int32, (1, MLA_H * HEAD_PAD), 1)
    ones_cols = jnp.where(lane % HEAD_PAD >= MLA_V, 1.0, 0.0)
    v_ref[...] = (jnp.dot(kvn, wv_ref[...], preferred_element_type=F32) + ones_cols).astype(BF16)
    kr = _rope_block(misc_ref[...], tab_ref, 3)
    for h in range(MLA_H):
        sl = slice(h * HEAD_PAD, (h + 1) * HEAD_PAD)
        q_ref[:, sl] = _rope_block(q[:, sl], tab_ref, 0).astype(BF16)
        k_ref[:, sl] = (k[:, sl] + kr).astype(BF16)


def _mla_prep(z, tabs, qg, wq_pad, kg, wk_pad, wv):
    n = z.shape[0]
    tm = ROW_TILE
    tt = tm // SUBLANES
    const = lambda i: (0, 0)
    return pl.pallas_call(
        _mla_prep_kernel,
        grid=(n // tm,),
        in_specs=[pl.BlockSpec((tm, MLA_QR), lambda i: (i, Z_QC // MLA_QR)),
                  pl.BlockSpec((tm, MLA_KVR), lambda i: (i, Z_KVC // MLA_KVR)),
                  pl.BlockSpec((tm, 128), lambda i: (i, Z_MISC // 128)),
                  pl.BlockSpec((6, tt, HEAD_PAD), lambda i: (0, i, 0)),
                  pl.BlockSpec((1, MLA_QR), const),
                  pl.BlockSpec((MLA_QR, MLA_H * HEAD_PAD), const),
                  pl.BlockSpec((1, MLA_KVR), const),
                  pl.BlockSpec((MLA_KVR, MLA_H * HEAD_PAD), const),
                  pl.BlockSpec((MLA_KVR, MLA_H * HEAD_PAD), const)],
        out_specs=[pl.BlockSpec((tm, MLA_H * HEAD_PAD), lambda i: (i, 0)),
                   pl.BlockSpec((tm, MLA_H * HEAD_PAD), lambda i: (i, 0)),
                   pl.BlockSpec((tm, MLA_H * HEAD_PAD), lambda i: (i, 0))],
        out_shape=[jax.ShapeDtypeStruct((n, MLA_H * HEAD_PAD), BF16),
                   jax.ShapeDtypeStruct((n, MLA_H * HEAD_PAD), BF16),
                   jax.ShapeDtypeStruct((n, MLA_H * HEAD_PAD), BF16)],
        compiler_params=_cparams(("parallel",)),
        name="mla_prep",
    )(z, z, z, tabs, qg, wq_pad, kg, wk_pad, wv)


def _flash_kernel(qt_ref, k_ref, vt_ref, o_ref, *, n_main, tk, tail, q_split):
    tq = qt_ref.shape[1]
    hq = tq // q_split
    chains = [(h, r) for h in range(2) for r in range(q_split)]
    nc = len(chains)

    def scores(start, size, c):
        h, r = chains[c]
        hs = slice(h * HEAD_PAD, (h + 1) * HEAD_PAD)
        return jnp.dot(k_ref[pl.ds(start, size), hs], qt_ref[hs, r * hq:(r + 1) * hq],
                       preferred_element_type=F32)

    def colmax(s):
        rows = s.shape[0]
        while rows % 16 == 0 and rows > 8:
            rows //= 2
            s = jnp.maximum(s[:rows], s[rows:])
        return jnp.max(s, 0, keepdims=True)

    def update(s, start, size, c, m, acc):
        h, _ = chains[c]
        hs = slice(h * HEAD_PAD, (h + 1) * HEAD_PAD)
        m_new = jnp.maximum(m, colmax(s))
        p = jnp.exp2(s - m_new).astype(BF16)
        acc = jnp.exp2(m - m_new) * acc + jnp.dot(vt_ref[hs, pl.ds(start, size)], p,
                                                  preferred_element_type=F32)
        return m_new, acc

    state = [(jnp.full((1, hq), -jnp.inf, F32), jnp.zeros((HEAD_PAD, hq), F32)) for _ in chains]
    if n_main:
        def body(j, carry):
            s_cur, st = carry[0], list(carry[1:])
            start = pl.multiple_of(j * tk, tk)
            nxt = pl.multiple_of(jnp.minimum(j + 1, n_main - 1) * tk, tk)
            for c in range(nc):
                s_next = scores(start, tk, c + 1) if c + 1 < nc else scores(nxt, tk, 0)
                st[c] = update(s_cur, start, tk, c, *st[c])
                s_cur = s_next
            return (s_cur, *st)

        carry = lax.fori_loop(0, n_main, body, (scores(0, tk, 0), *state))
        state = list(carry[1:])
    if tail:
        for c in range(nc):
            state[c] = update(scores(n_main * tk, tail, c), n_main * tk, tail, c, *state[c])
    for (h, r), (_, acc) in zip(chains, state):
        o_ref[h * MLA_V:(h + 1) * MLA_V, r * hq:(r + 1) * hq] = (
            acc[0:MLA_V] / acc[MLA_V:MLA_V + 1]).astype(BF16)


def _flash(qt, k2, vt, *, n_q_tiles, tq, q_tile0, n_keys, key_block):
    n_main, tail = divmod(n_keys, ATT_TK)
    pairs = MLA_H // 2
    kern = functools.partial(_flash_kernel, n_main=n_main, tk=ATT_TK, tail=tail, q_split=ATT_QSPLIT)
    return pl.pallas_call(
        kern,
        grid=(SUBLANES, pairs, n_q_tiles),
        in_specs=[pl.BlockSpec((2 * HEAD_PAD, tq), lambda b, p, i: (b * pairs + p, q_tile0 + i)),
                  pl.BlockSpec((n_keys, 2 * HEAD_PAD), lambda b, p, i: (key_block, b * pairs + p)),
                  pl.BlockSpec((2 * HEAD_PAD, n_keys), lambda b, p, i: (b * pairs + p, key_block))],
        out_specs=pl.BlockSpec((2 * MLA_V, tq), lambda b, p, i: (b * pairs + p, i)),
        out_shape=jax.ShapeDtypeStruct((SUBLANES * MLA_W, n_q_tiles * tq), BF16),
        compiler_params=_cparams(("parallel", "parallel", "arbitrary")),
        name="mla_flash",
    )(qt, k2, vt)


def _s5_kernel(u_ref, bb_ref, cc_ref, a_ref, y_ref, st_ref, bu_ref, *, steps):
    d = pl.program_id(0)

    @pl.when(pl.program_id(1) == 0)
    def _():
        st_ref[...] = jnp.zeros_like(st_ref)

    bu_ref[...] = jnp.dot(u_ref[...].astype(BF16), bb_ref[0], preferred_element_type=F32)
    ar = jnp.broadcast_to(a_ref[0, 0:1, :], (SUBLANES, S5_STATE))
    ai = jnp.broadcast_to(a_ref[0, 1:2, :], (SUBLANES, S5_STATE))

    def body(i, carry):
        re, im = carry
        t = jnp.where(d == 0, i, steps - 1 - i)
        row = pl.multiple_of(t * SUBLANES, SUBLANES)
        bur = bu_ref[pl.ds(row, SUBLANES), 0:S5_STATE]
        bui = bu_ref[pl.ds(row, SUBLANES), S5_STATE:2 * S5_STATE]
        nre = ar * re - ai * im + bur
        nim = ar * im + ai * re + bui
        bu_ref[pl.ds(row, SUBLANES), 0:S5_STATE] = nre
        bu_ref[pl.ds(row, SUBLANES), S5_STATE:2 * S5_STATE] = nim
        return nre, nim

    re, im = lax.fori_loop(0, steps, body, (st_ref[0], st_ref[1]))
    st_ref[0] = re
    st_ref[1] = im
    y_ref[0] = jnp.dot(bu_ref[...].astype(BF16), cc_ref[0], preferred_element_type=F32)


def _s5_scan(z, bblk, cblk, a2, n_lat_steps, n_steps):
    n = z.shape[0]
    rows = S5_STEPS * SUBLANES
    n_chunks = n_steps // S5_STEPS
    n_lat_chunks = n_lat_steps // S5_STEPS

    def blk(d, s):
        return jnp.where(d == 0, (s + n_lat_chunks) % n_chunks, n_chunks - 1 - s)

    return pl.pallas_call(
        functools.partial(_s5_kernel, steps=S5_STEPS),
        grid=(2, n_chunks),
        in_specs=[pl.BlockSpec((rows, S5_W), lambda d, s: (blk(d, s), Z_U // S5_W)),
                  pl.BlockSpec((1, S5_W, 2 * S5_STATE), lambda d, s: (d, 0, 0)),
                  pl.BlockSpec((1, 2 * S5_STATE, S5_W), lambda d, s: (d, 0, 0)),
                  pl.BlockSpec((1, 2, S5_STATE), lambda d, s: (d, 0, 0))],
        out_specs=pl.BlockSpec((1, rows, S5_W), lambda d, s: (d, blk(d, s), 0)),
        out_shape=jax.ShapeDtypeStruct((2, n, S5_W), F32),
        scratch_shapes=[pltpu.VMEM((2, SUBLANES, S5_STATE), F32),
                        pltpu.VMEM((rows, 2 * S5_STATE), F32)],
        compiler_params=_cparams(("arbitrary", "arbitrary")),
        name="s5_scan",
    )(z, bblk, cblk, a2)


def _shift_rows(x, sh, rev):
    z = jnp.zeros((sh, x.shape[1]), x.dtype)
    if rev:
        return jnp.concatenate([x[sh:], z], 0)
    return jnp.concatenate([z, x[:-sh]], 0)


def _mlstm_kernel(q_ref, k_ref, v_ref, g_ref, qp_ref, qn_ref, kp_ref, kn_ref, cw_ref, cb_ref, gb_ref,
                  h_ref, cst_ref, mst_ref, qk_ref, qs_ref, ks_ref, vs_ref, hs_ref, bc_ref, *, rev, n_lat_chunks, n_chunks):
    T = ML_CHUNK
    rows = T * SUBLANES
    s = pl.program_id(0)
    c = (n_chunks - 1 - s) if rev else (s + n_lat_chunks) % n_chunks

    @pl.when(s == 0)
    def _():
        cst_ref[...] = jnp.zeros_like(cst_ref)
        mst_ref[...] = jnp.zeros_like(mst_ref)

    first = jnp.logical_or(c == 0, c == n_lat_chunks)
    last = jnp.logical_or(c == n_lat_chunks - 1, c == n_chunks - 1)
    keep_p = jnp.where(first, 0.0, 1.0)
    keep_n = jnp.where(last, 0.0, 1.0)
    qk_ref[0:SUBLANES, 0:ML_W] = qp_ref[...] * keep_p
    qk_ref[0:SUBLANES, ML_W:2 * ML_W] = kp_ref[...] * keep_p
    qk_ref[SUBLANES:SUBLANES + rows, 0:ML_W] = q_ref[...]
    qk_ref[SUBLANES:SUBLANES + rows, ML_W:2 * ML_W] = k_ref[...]
    qk_ref[SUBLANES + rows:2 * SUBLANES + rows, 0:ML_W] = qn_ref[...] * keep_n
    qk_ref[SUBLANES + rows:2 * SUBLANES + rows, ML_W:2 * ML_W] = kn_ref[...] * keep_n
    cw = cw_ref[...]
    conv = (cb_ref[...] + qk_ref[0:rows, :] * cw[0:1] + qk_ref[SUBLANES:SUBLANES + rows, :] * cw[1:2]
            + qk_ref[2 * SUBLANES:2 * SUBLANES + rows, :] * cw[2:3])
    conv = conv * jax.nn.sigmoid(conv)
    for j in range(2):
        qs_ref[j] = conv[:, j * 128:(j + 1) * 128]
        ks_ref[j] = conv[:, ML_W + j * 128:ML_W + (j + 1) * 128] * (ML_D ** -0.5)
        vs_ref[j] = v_ref[:, j * 128:(j + 1) * 128]

    g = g_ref[...] + gb_ref[...]
    lane = lax.broadcasted_iota(jnp.int32, g.shape, 1)
    is_f = jnp.logical_and(lane >= 2 * ML_H, lane < 4 * ML_H)
    gm = jnp.where(is_f, jax.nn.log_sigmoid(g), g)
    cum = jnp.where(is_f, gm, 0.0)
    sh = SUBLANES
    while sh < rows:
        cum = cum + _shift_rows(cum, sh, rev)
        sh *= 2
    bc_ref[0] = gm
    bc_ref[1] = cum

    i_lane0 = ML_H if rev else 0
    f_lane0 = 3 * ML_H if rev else 2 * ML_H
    end_row = 0 if rev else T - 1
    ti = lax.broadcasted_iota(jnp.int32, (T, T), 0)
    si = lax.broadcasted_iota(jnp.int32, (T, T), 1)
    order = (si >= ti) if rev else (si <= ti)
    lane_w = lax.broadcasted_iota(jnp.int32, (1, ML_W), 1) // ML_D
    rowhead = lax.broadcasted_iota(jnp.int32, (ML_W, 1), 0) // ML_D
    colhead = lax.broadcasted_iota(jnp.int32, (1, ML_AUG), 1)
    colhead = jnp.where(colhead < ML_W, colhead // ML_D, colhead - ML_W)
    blockmask = rowhead == colhead
    ones_aug = jnp.ones((T, ML_AUG - ML_W), F32)

    def per_batch(b, _):
        sl = pl.ds(b, T, stride=SUBLANES)
        qb = jnp.concatenate([qs_ref[0, sl, :], qs_ref[1, sl, :]], -1)
        kb = jnp.concatenate([ks_ref[0, sl, :], ks_ref[1, sl, :]], -1)
        vb = jnp.concatenate([vs_ref[0, sl, :], vs_ref[1, sl, :]], -1)
        gcol = bc_ref[0, sl, :]
        ccol = bc_ref[1, sl, :]
        grow = gcol.T
        crow = ccol.T
        cst = cst_ref[b]
        qc = jnp.dot(qb.astype(BF16), cst.astype(BF16), preferred_element_type=F32)
        vaug = jnp.concatenate([vb, ones_aug], -1).astype(BF16)
        kb16 = kb.astype(BF16)
        num = jnp.zeros((T, ML_W), F32)
        den = jnp.zeros((T, ML_W), F32)
        stab = jnp.zeros((T, ML_W), F32)
        ke_scale = jnp.zeros((T, ML_W), F32)
        a_col = jnp.zeros((ML_W, 1), F32)
        s_col = jnp.zeros((ML_W, 1), F32)
        for h in range(ML_H):
            hm = lane_w == h
            il, fl = i_lane0 + h, f_lane0 + h
            b_col = ccol[:, fl:fl + 1]
            i_col = gcol[:, il:il + 1]
            b_row = crow[fl:fl + 1, :]
            i_row = grow[il:il + 1, :]
            b_tot = ccol[end_row:end_row + 1, fl:fl + 1]
            m_in = mst_ref[b, h][0:1, 0:1]
            ld = jnp.where(order, b_col - b_row + i_row, -jnp.inf)
            m_t = jnp.maximum(b_col + m_in, jnp.max(ld, -1, keepdims=True))
            dw = jnp.exp(ld - m_t)
            w_inter = jnp.exp(b_col + m_in - m_t)
            qh = jnp.where(hm, qb, 0.0).astype(BF16)
            sc = lax.dot_general(qh, kb16, (((1,), (1,)), ((), ())), preferred_element_type=F32) * dw
            pv = jnp.dot(sc.astype(BF16), vaug[:, 0:ML_W], preferred_element_type=F32)
            den_h = jnp.sum(sc, -1, keepdims=True) + w_inter * qc[:, ML_W + h:ML_W + h + 1]
            num = num + jnp.where(hm, pv + w_inter * qc[:, 0:ML_W], 0.0)
            den = den + jnp.where(hm, den_h, 0.0)
            stab = stab + jnp.where(hm, jnp.exp(-m_t), 0.0)
            w_end = b_tot - b_col + i_col
            m_loc = jnp.max(w_end, 0, keepdims=True)
            m_new = jnp.maximum(b_tot + m_in, m_loc)
            ke_scale = ke_scale + jnp.where(hm, jnp.exp(w_end - m_loc), 0.0)
            a_col = a_col + jnp.where(rowhead == h, jnp.exp(b_tot + m_in - m_new), 0.0)
            s_col = s_col + jnp.where(rowhead == h, jnp.exp(m_loc - m_new), 0.0)
            mst_ref[b, h] = jnp.broadcast_to(m_new, (SUBLANES, 128))
        hout = num / jnp.maximum(jnp.abs(den), stab)
        hs_ref[0, sl, :] = hout[:, 0:128]
        hs_ref[1, sl, :] = hout[:, 128:256]
        ke = (kb * ke_scale).astype(BF16)
        upd = lax.dot_general(ke, vaug, (((0,), (0,)), ((), ())), preferred_element_type=F32)
        cst_ref[b] = a_col * cst + jnp.where(blockmask, s_col * upd, 0.0)
        return 0

    lax.fori_loop(0, SUBLANES, per_batch, 0)
    h_ref[...] = jnp.concatenate([hs_ref[0], hs_ref[1]], -1)


def _mlstm(z, conv_w, conv_b, gate_b_pad, *, rev, n_lat_steps, n_steps):
    n = z.shape[0]
    T = ML_CHUNK
    rows = T * SUBLANES
    n_chunks = n_steps // T
    n_lat_chunks = n_lat_steps // T
    hb = rows // SUBLANES
    n_hblk = n // SUBLANES

    def blk(s):
        return (n_chunks - 1 - s) if rev else (s + n_lat_chunks) % n_chunks

    prev = lambda s: jnp.maximum(blk(s) * hb - 1, 0)
    nxt = lambda s: jnp.minimum((blk(s) + 1) * hb, n_hblk - 1)
    const = lambda s: (0, 0)
    kern = functools.partial(_mlstm_kernel, rev=rev, n_lat_chunks=n_lat_chunks, n_chunks=n_chunks)
    return pl.pallas_call(
        kern,
        grid=(n_chunks,),
        in_specs=[pl.BlockSpec((rows, ML_W), lambda s: (blk(s), Z_MLQ // ML_W)),
                  pl.BlockSpec((rows, ML_W), lambda s: (blk(s), Z_MLK // ML_W)),
                  pl.BlockSpec((rows, ML_W), lambda s: (blk(s), Z_MLV // ML_W)),
                  pl.BlockSpec((rows, 128), lambda s: (blk(s), Z_MISC // 128)),
                  pl.BlockSpec((SUBLANES, ML_W), lambda s: (prev(s), Z_MLQ // ML_W)),
                  pl.BlockSpec((SUBLANES, ML_W), lambda s: (nxt(s), Z_MLQ // ML_W)),
                  pl.BlockSpec((SUBLANES, ML_W), lambda s: (prev(s), Z_MLK // ML_W)),
                  pl.BlockSpec((SUBLANES, ML_W), lambda s: (nxt(s), Z_MLK // ML_W)),
                  pl.BlockSpec((3, 2 * ML_W), const),
                  pl.BlockSpec((1, 2 * ML_W), const),
                  pl.BlockSpec((1, 128), const)],
        out_specs=pl.BlockSpec((rows, ML_W), lambda s: (blk(s), 0)),
        out_shape=jax.ShapeDtypeStruct((n, ML_W), F32),
        scratch_shapes=[pltpu.VMEM((SUBLANES, ML_W, ML_AUG), F32),
                        pltpu.VMEM((SUBLANES, ML_H, SUBLANES, 128), F32),
                        pltpu.VMEM((rows + 2 * SUBLANES, 2 * ML_W), F32),
                        pltpu.VMEM((2, rows, 128), F32),
                        pltpu.VMEM((2, rows, 128), F32),
                        pltpu.VMEM((2, rows, 128), F32),
                        pltpu.VMEM((2, rows, 128), F32),
                        pltpu.VMEM((2, rows, 128), F32)],
        compiler_params=_cparams(("arbitrary",)),
        name="mlstm_rev" if rev else "mlstm_fwd",
    )(z, z, z, z, z, z, z, z, conv_w, conv_b, gate_b_pad)


def _post_kernel(x_ref, g1_ref, sc_ref, sh_ref, u_ref, y_ref, o_ref, attn_ref, hf_ref, hb_ref,
                 d_ref, gw_ref, gb_ref, hn_ref, pm_ref, wo_ref, l1g_ref, l1b_ref, rw_ref,
                 x1_ref, f_ref, lg_ref):
    y = y_ref[0] + y_ref[1] + u_ref[...] * d_ref[...]
    g = jax.nn.gelu(y)
    s5 = g * jax.nn.sigmoid(jnp.dot(g.astype(BF16), gw_ref[...], preferred_element_type=F32) + gb_ref[...])
    hh = jax.nn.sigmoid(o_ref[...]) * (hf_ref[...] + hb_ref[...])
    pm = pm_ref[...]
    mu = jnp.dot(hh, pm, preferred_element_type=F32, precision=lax.Precision.HIGHEST)
    hc = hh - mu
    var = jnp.dot(hc * hc, pm, preferred_element_type=F32, precision=lax.Precision.HIGHEST)
    ml = hc * lax.rsqrt(var + LN_EPS) * hn_ref[...]
    mix = jnp.concatenate([s5.astype(BF16), attn_ref[...], ml.astype(BF16)], -1)
    yo = jnp.dot(mix, wo_ref[...], preferred_element_type=F32)
    x1 = _ln(DEEPNORM_ALPHA * x_ref[...] + _per_batch(yo, g1_ref[0])) * l1g_ref[...] + l1b_ref[...]
    x1_ref[...] = x1
    f = _modulate(_ln(x1), sc_ref[0], sh_ref[0])
    _store_tile_rows(f_ref, f)
    lg_ref[...] = lax.dot_general(rw_ref[...], f, (((1,), (1,)), ((), ())), preferred_element_type=F32,
                                  precision=lax.Precision.HIGHEST)


def _post(x, mod, z, y2, attn, hf, hb, p, n_lat_tiles):
    n = x.shape[0]
    tm = ROW_TILE
    sel = lambda i: (i >= n_lat_tiles).astype(jnp.int32)
    const = lambda i: (0, 0)
    row = lambda w: pl.BlockSpec((tm, w), lambda i: (i, 0))
    return pl.pallas_call(
        _post_kernel,
        grid=(n // tm,),
        in_specs=[row(D_MODEL),
                  pl.BlockSpec((1, SUBLANES, D_MODEL), lambda i: (sel(i), 0, 2)),
                  pl.BlockSpec((1, SUBLANES, D_MODEL), lambda i: (sel(i), 0, 4)),
                  pl.BlockSpec((1, SUBLANES, D_MODEL), lambda i: (sel(i), 0, 3)),
                  pl.BlockSpec((tm, S5_W), lambda i: (i, Z_U // S5_W)),
                  pl.BlockSpec((2, tm, S5_W), lambda i: (0, i, 0)),
                  pl.BlockSpec((tm, ML_W), lambda i: (i, Z_MLO // ML_W)),
                  row(MLA_W), row(ML_W), row(ML_W),
                  pl.BlockSpec((1, S5_W), const),
                  pl.BlockSpec((S5_W, S5_W), const),
                  pl.BlockSpec((1, S5_W), const),
                  pl.BlockSpec((1, ML_W), const),
                  pl.BlockSpec((ML_W, ML_W), const),
                  pl.BlockSpec((D_MODEL, D_MODEL), const),
                  pl.BlockSpec((1, D_MODEL), const),
                  pl.BlockSpec((1, D_MODEL), const),
                  pl.BlockSpec((N_EXPERTS, D_MODEL), const)],
        out_specs=[row(D_MODEL), pl.BlockSpec((tm,) + ROW_TILE_SHAPE, lambda i: (i, 0, 0)),
                   pl.BlockSpec((N_EXPERTS, tm), lambda i: (0, i))],
        out_shape=[jax.ShapeDtypeStruct((n, D_MODEL), F32),
                   jax.ShapeDtypeStruct((n,) + ROW_TILE_SHAPE, F32),
                   jax.ShapeDtypeStruct((N_EXPERTS, n), F32)],
        compiler_params=_cparams(("parallel",)),
        name="post_mix",
    )(x, mod, mod, mod, z, y2, z, attn, hf, hb,
      p["s5_d"], p["glu_w"], p["glu_b"], p["ml_norm_g"], p["head_mean"], p["w_out"], p["ln1_g"], p["ln1_b"],
      p["router_w"])


def _expert_kernel(be_ref, x_ref, wg_ref, wu_ref, wd_ref, o_ref):
    x = _load_tile_rows(x_ref).astype(BF16)
    a = jnp.dot(x, wg_ref[0], preferred_element_type=F32)
    u = jnp.dot(x, wu_ref[0], preferred_element_type=F32)
    hmid = (a * jax.nn.sigmoid(a) * u).astype(BF16)
    _store_tile_rows(o_ref, jnp.dot(hmid, wd_ref[0], preferred_element_type=F32))


def _experts(blk_expert, xs, wg, wu, wd):
    n_rows = xs.shape[0]
    tb = MOE_BLOCK
    rows = pl.BlockSpec((tb,) + ROW_TILE_SHAPE, lambda i, be: (i, 0, 0))
    grid_spec = pltpu.PrefetchScalarGridSpec(
        num_scalar_prefetch=1,
        grid=(n_rows // tb,),
        in_specs=[rows,
                  pl.BlockSpec((1, D_MODEL, EXPERT_F), lambda i, be: (be[i], 0, 0)),
                  pl.BlockSpec((1, D_MODEL, EXPERT_F), lambda i, be: (be[i], 0, 0)),
                  pl.BlockSpec((1, EXPERT_F, D_MODEL), lambda i, be: (be[i], 0, 0))],
        out_specs=rows,
    )
    return pl.pallas_call(
        _expert_kernel,
        grid_spec=grid_spec,
        out_shape=jax.ShapeDtypeStruct((n_rows,) + ROW_TILE_SHAPE, F32),
        compiler_params=_cparams(("arbitrary",)),
        name="moe_experts",
    )(blk_expert, xs, wg, wu, wd)


def _idx_copy(idx_hbm, idx_smem, sem, tile, slot, width):
    return pltpu.make_async_copy(idx_hbm.at[tile], idx_smem.at[pl.ds(pl.multiple_of(slot * width, width), width)],
                                 sem.at[slot])


def _row_groups(tm, base, fn):
    def trip(g, c):
        t0 = pl.multiple_of(g * SUBLANES, SUBLANES)
        for j in range(SUBLANES):
            for k in range(TOP_K):
                fn(k, t0 + j, base + (k * tm + j) + t0)
        return c

    lax.fori_loop(0, tm // SUBLANES, trip, 0)


def _dispatch_kernel(pe_ref, pc_ref, f_ref, idx_hbm, xs_hbm, idx_smem, zero_ref, isem, zsem, rsem, *, tm):
    i = pl.program_id(0)
    n = pl.num_programs(0)
    slot = i % 2
    tb = MOE_BLOCK
    width = TOP_K * tm

    def row_copy(t, r):
        return pltpu.make_async_copy(f_ref.at[t], xs_hbm.at[r], rsem)

    def pad_copy(e):
        start = pl.multiple_of(pe_ref[e] - tb, tb)
        return pltpu.make_async_copy(zero_ref, xs_hbm.at[pl.ds(start, tb)], zsem)

    @pl.when(i == 0)
    def _():
        _idx_copy(idx_hbm, idx_smem, isem, 0, 0, width).start()
        zero_ref[...] = jnp.zeros_like(zero_ref)

        def fill(e, c):
            @pl.when(pc_ref[e] > 0)
            def _():
                pad_copy(e).start()
            return c

        def drain(e, c):
            @pl.when(pc_ref[e] > 0)
            def _():
                pad_copy(e).wait()
            return c

        lax.fori_loop(0, N_EXPERTS, fill, 0)
        lax.fori_loop(0, N_EXPERTS, drain, 0)

    _idx_copy(idx_hbm, idx_smem, isem, i, slot, width).wait()

    @pl.when(i + 1 < n)
    def _():
        _idx_copy(idx_hbm, idx_smem, isem, i + 1, 1 - slot, width).start()

    base = slot * width
    _row_groups(tm, base, lambda k, t, a: row_copy(t, idx_smem[a]).start())
    _row_groups(tm, base, lambda k, t, a: row_copy(t, idx_smem[a]).wait())


def _dispatch_rows(f3, idx_tiles, pad_end, padded, n_rows):
    n = f3.shape[0]
    tm = ROW_TILE
    grid_spec = pltpu.PrefetchScalarGridSpec(
        num_scalar_prefetch=2,
        grid=(n // tm,),
        in_specs=[pl.BlockSpec((tm,) + ROW_TILE_SHAPE, lambda i, pe, pc: (i, 0, 0)),
                  pl.BlockSpec(memory_space=pl.ANY)],
        out_specs=pl.BlockSpec(memory_space=pl.ANY),
        scratch_shapes=[pltpu.SMEM((2 * TOP_K * tm,), jnp.int32),
                        pltpu.VMEM((MOE_BLOCK,) + ROW_TILE_SHAPE, F32),
                        pltpu.SemaphoreType.DMA((2,)),
                        pltpu.SemaphoreType.DMA,
                        pltpu.SemaphoreType.DMA],
    )
    return pl.pallas_call(
        functools.partial(_dispatch_kernel, tm=tm),
        grid_spec=grid_spec,
        out_shape=jax.ShapeDtypeStruct((n_rows,) + ROW_TILE_SHAPE, F32),
        compiler_params=_cparams(("arbitrary",)),
        name="moe_dispatch",
    )(pad_end, padded, f3, idx_tiles)


def _final_kernel(x1_ref, f_ref, gate_ref, g2_ref, wg_ref, wu_ref, wd_ref, lg_ref, lb_ref, idx_hbm, ys_hbm,
                  o_ref, idx_smem, buf_ref, isem, rsem, *, tm):
    i = pl.program_id(0)
    n = pl.num_programs(0)
    slot = i % 2
    width = TOP_K * tm

    def row_copy(k, t, r):
        return pltpu.make_async_copy(ys_hbm.at[r], buf_ref.at[k, t], rsem)

    @pl.when(i == 0)
    def _():
        _idx_copy(idx_hbm, idx_smem, isem, 0, 0, width).start()

    _idx_copy(idx_hbm, idx_smem, isem, i, slot, width).wait()

    @pl.when(i + 1 < n)
    def _():
        _idx_copy(idx_hbm, idx_smem, isem, i + 1, 1 - slot, width).start()

    base = slot * width
    _row_groups(tm, base, lambda k, t, a: row_copy(k, t, idx_smem[a]).start())
    f = _load_tile_rows(f_ref).astype(BF16)
    a = jnp.dot(f, wg_ref[...], preferred_element_type=F32)
    u = jnp.dot(f, wu_ref[...], preferred_element_type=F32)
    ffn = jnp.dot((a * jax.nn.sigmoid(a) * u).astype(BF16), wd_ref[...], preferred_element_type=F32)
    _row_groups(tm, base, lambda k, t, a: row_copy(k, t, idx_smem[a]).wait())
    gate = gate_ref[...]
    for k in range(TOP_K):
        ffn = ffn + gate[:, k:k + 1] * _load_tile_rows(buf_ref.at[k])
    o_ref[...] = _ln(DEEPNORM_ALPHA * x1_ref[...] + _per_batch(ffn, g2_ref[0])) * lg_ref[...] + lb_ref[...]


def _final(x1, f3, gate_t, idx_tiles, ys, mod, p, n_lat_tiles, n_out_tiles):
    tm = ROW_TILE
    sel = lambda i: (i >= n_lat_tiles).astype(jnp.int32)
    const = lambda i: (0, 0)
    row = lambda w: pl.BlockSpec((tm, w), lambda i: (i, 0))
    return pl.pallas_call(
        functools.partial(_final_kernel, tm=tm),
        grid=(n_out_tiles,),
        in_specs=[row(D_MODEL), pl.BlockSpec((tm,) + ROW_TILE_SHAPE, lambda i: (i, 0, 0)), row(TOP_K),
                  pl.BlockSpec((1, SUBLANES, D_MODEL), lambda i: (sel(i), 0, 5)),
                  pl.BlockSpec((D_MODEL, EXPERT_F), const),
                  pl.BlockSpec((D_MODEL, EXPERT_F), const),
                  pl.BlockSpec((EXPERT_F, D_MODEL), const),
                  pl.BlockSpec((1, D_MODEL), const),
                  pl.BlockSpec((1, D_MODEL), const),
                  pl.BlockSpec(memory_space=pl.ANY),
                  pl.BlockSpec(memory_space=pl.ANY)],
        out_specs=row(D_MODEL),
        out_shape=jax.ShapeDtypeStruct((n_out_tiles * tm, D_MODEL), F32),
        scratch_shapes=[pltpu.SMEM((2 * TOP_K * tm,), jnp.int32),
                        pltpu.VMEM((TOP_K, tm) + ROW_TILE_SHAPE, F32),
                        pltpu.SemaphoreType.DMA((2,)),
                        pltpu.SemaphoreType.DMA],
        compiler_params=_cparams(("arbitrary",)),
        name="final_ffn",
    )(x1, f3, gate_t, mod, p["sh_w_gate"], p["sh_w_up"], p["sh_w_down"], p["ln2_g"], p["ln2_b"], idx_tiles, ys)


def _route_kernel(lg_ref, bias_ref, tri_ref, e_ref, g_ref, r_ref, cnt_ref, carry_ref):
    @pl.when(pl.program_id(0) == 0)
    def _():
        carry_ref[...] = jnp.zeros_like(carry_ref)

    tm = lg_ref.shape[1]
    neg = -jnp.inf
    s = jax.nn.sigmoid(lg_ref[...])
    g3 = (s + bias_ref[...]).reshape(N_EXPERT_GROUPS, EPG, tm)
    io3 = lax.broadcasted_iota(jnp.int32, (N_EXPERT_GROUPS, EPG, tm), 1)
    m1 = jnp.max(g3, 1, keepdims=True)
    f1 = jnp.min(jnp.where(g3 == m1, io3, EPG), 1, keepdims=True)
    m2 = jnp.max(jnp.where(io3 == f1, neg, g3), 1, keepdims=True)
    gs = m1 + m2
    iog = lax.broadcasted_iota(jnp.int32, (N_EXPERT_GROUPS, 1, tm), 0)
    cur = gs
    kth = gs
    for _ in range(TOP_GROUPS):
        kth = jnp.max(cur, 0, keepdims=True)
        fi = jnp.min(jnp.where(cur == kth, iog, N_EXPERT_GROUPS), 0, keepdims=True)
        cur = jnp.where(iog == fi, neg, cur)
    cand = jnp.where(gs >= kth, g3, neg).reshape(N_EXPERTS, tm)
    io = lax.broadcasted_iota(jnp.int32, (N_EXPERTS, tm), 0)
    memb = jnp.zeros((N_EXPERTS, tm), F32)
    es, gates, hots = [], [], []
    for _ in range(TOP_K):
        mk = jnp.max(cand, 0, keepdims=True)
        ik = jnp.min(jnp.where(cand == mk, io, N_EXPERTS), 0, keepdims=True)
        oh = io == ik
        gates.append(jnp.sum(jnp.where(oh, s, 0.0), 0, keepdims=True))
        es.append(ik)
        hots.append(oh)
        cand = jnp.where(oh, neg, cand)
        memb = memb + jnp.where(oh, 1.0, 0.0)
    gsum = gates[0]
    for gk in gates[1:]:
        gsum = gsum + gk
    g_ref[...] = jnp.concatenate(gates, 0) / gsum * ROUTED_SCALE
    e_ref[...] = jnp.concatenate(es, 0)
    pref = jnp.dot(memb.astype(BF16), tri_ref[...], preferred_element_type=F32) + carry_ref[:, 0:1]
    ranks = [jnp.sum(jnp.where(oh, pref, 0.0), 0, keepdims=True) for oh in hots]
    r_ref[...] = jnp.concatenate(ranks, 0).astype(jnp.int32)
    total = carry_ref[...] + jnp.sum(memb, 1, keepdims=True)
    carry_ref[...] = total
    cnt_ref[...] = total


def _route(logits_t, router_bias):
    n = logits_t.shape[1]
    tm = ROW_TILE
    tri = (jnp.arange(tm)[:, None] < jnp.arange(tm)[None, :]).astype(BF16)
    const = lambda i: (0, 0)
    col = pl.BlockSpec((TOP_K, tm), lambda i: (0, i))
    top_e, gate, rank, cnt = pl.pallas_call(
        _route_kernel,
        grid=(n // tm,),
        in_specs=[pl.BlockSpec((N_EXPERTS, tm), lambda i: (0, i)),
                  pl.BlockSpec((N_EXPERTS, 1), const),
                  pl.BlockSpec((tm, tm), const)],
        out_specs=[col, col, col, pl.BlockSpec((N_EXPERTS, 128), const)],
        out_shape=[jax.ShapeDtypeStruct((TOP_K, n), jnp.int32),
                   jax.ShapeDtypeStruct((TOP_K, n), F32),
                   jax.ShapeDtypeStruct((TOP_K, n), jnp.int32),
                   jax.ShapeDtypeStruct((N_EXPERTS, 128), F32)],
        scratch_shapes=[pltpu.VMEM((N_EXPERTS, 128), F32)],
        compiler_params=_cparams(("arbitrary",)),
        name="moe_route",
    )(logits_t, router_bias.astype(F32).reshape(N_EXPERTS, 1), tri)
    return top_e, gate, rank, cnt[:, 0].astype(jnp.int32)


def _dispatch(top_e, rank, counts):
    n_tok = top_e.shape[1]
    tb = MOE_BLOCK
    padded = (counts + tb - 1) // tb * tb
    pad_end = jnp.cumsum(padded)
    pad_start = pad_end - padded
    hot = top_e[:, :, None] == jnp.arange(N_EXPERTS, dtype=top_e.dtype)
    dest = (jnp.sum(jnp.where(hot, pad_start.astype(jnp.int32), 0), -1) + rank).astype(jnp.int32)
    n_rows = -(-(n_tok * TOP_K + N_EXPERTS * (tb - 1)) // tb) * tb
    n_blocks = n_rows // tb
    blk_start = jnp.arange(n_blocks, dtype=jnp.int32) * tb
    blk_expert = jnp.minimum(jnp.sum((pad_end[None, :] <= blk_start[:, None]).astype(jnp.int32), 1),
                             N_EXPERTS - 1)
    tm = ROW_TILE
    idx_tiles = jnp.transpose(dest.reshape(TOP_K, n_tok // tm, tm), (1, 0, 2)).reshape(n_tok // tm, TOP_K * tm)
    return idx_tiles, blk_expert, pad_end.astype(jnp.int32), padded.astype(jnp.int32), n_rows


def _pad_w_in(w_in):
    s5u, qc, kvc, kr, mq, mk, mv, mo, gates = jnp.split(
        w_in, [256, 512, 640, 672, 928, 1184, 1440, 1696], axis=1)
    misc = jnp.zeros((D_MODEL, 128), w_in.dtype)
    misc = misc.at[:, MISC_GATES:MISC_GATES + 16].set(gates).at[:, MISC_ROPE:MISC_ROPE + MLA_ROPE].set(kr)
    return jnp.concatenate([s5u, qc, mq, mk, mv, mo, kvc, misc], axis=1).astype(BF16)


def _pad_heads(w, width):
    k = w.shape[0]
    w3 = w.reshape(k, MLA_H, width)
    return jnp.pad(w3, ((0, 0), (0, 0), (0, HEAD_PAD - width))).reshape(k, MLA_H * HEAD_PAD)


def _rope_tables(n_lat, n_ctx):
    t = jnp.arange(n_lat)
    half = ROPE_AXIS // 2
    inv_freq = ROPE_BASE ** (-jnp.arange(half, dtype=F32) / half)
    ang_r = (t // GRID_W).astype(F32)[:, None] * inv_freq
    ang_c = (t % GRID_W).astype(F32)[:, None] * inv_freq
    ang = jnp.concatenate([ang_r, ang_r, ang_c, ang_c], -1)
    ang = jnp.concatenate([ang, jnp.zeros((n_ctx, MLA_ROPE), F32)], 0)
    cos, sin = jnp.cos(ang), jnp.sin(ang)
    first = (jnp.arange(MLA_ROPE) % ROPE_AXIS) < half

    def place(v, fill):
        out = jnp.full((v.shape[0], HEAD_PAD), fill, F32)
        return out.at[:, MLA_NOPE:MLA_NOPE + MLA_ROPE].set(v)

    q_scale = MLA_SCALE * math.log2(math.e)
    c_q = place(cos, 1.0) * q_scale
    m_q = place(jnp.where(first, -sin, 0.0), 0.0) * q_scale
    p_q = place(jnp.where(first, 0.0, sin), 0.0) * q_scale
    c_k = place(cos, 0.0)
    m_k = place(jnp.where(first, -sin, 0.0), 0.0)
    p_k = place(jnp.where(first, 0.0, sin), 0.0)
    return jnp.stack([c_q, m_q, p_q, c_k, m_k, p_k], 0)


def _s5_params(lam_re, lam_im, log_step, b_re, b_im, c_re, c_im):
    lr, li = lam_re.astype(F32), lam_im.astype(F32)
    step = jnp.exp(log_step.astype(F32))[..., None]
    mag = jnp.exp(lr * step)
    ar, ai = mag * jnp.cos(li * step), mag * jnp.sin(li * step)
    nr, ni = ar - 1.0, ai
    den = lr * lr + li * li
    cr, ci = ((nr * lr + ni * li) / den)[..., None], ((ni * lr - nr * li) / den)[..., None]
    br, bi = b_re.astype(F32), b_im.astype(F32)
    bbr, bbi = cr * br - ci * bi, cr * bi + ci * br
    eye = jnp.eye(S5_G, dtype=F32)
    b_blk = lambda v: jnp.einsum("dgcp,gh->dgchp", jnp.transpose(v, (0, 1, 3, 2)), eye).reshape(2, S5_W, S5_STATE)
    bblk = jnp.concatenate([b_blk(bbr), b_blk(bbi)], -1)
    c_blk = lambda v: jnp.einsum("dgpc,gh->dgphc", jnp.transpose(v.astype(F32), (0, 1, 3, 2)), eye).reshape(
        2, S5_STATE, S5_W)
    cblk = jnp.concatenate([c_blk(c_re), -c_blk(c_im)], 1)
    a2 = jnp.stack([ar.reshape(2, S5_STATE), ai.reshape(2, S5_STATE)], 1)
    return bblk.astype(BF16), cblk.astype(BF16), a2


def kernel(x, c, ctx, c_ctx, ada_w, ada_b, w_in, s5_lambda_re, s5_lambda_im, s5_log_step, s5_b_re, s5_b_im, s5_c_re, s5_c_im, s5_d, s5_glu_w, s5_glu_b, mla_q_norm, mla_w_q_up, mla_kv_norm, mla_w_kv_up, ml_conv_w, ml_conv_b, ml_gate_b, ml_norm_g, w_out, ln1_g, ln1_b, ln2_g, ln2_b, router_w, router_bias, exp_w_gate, exp_w_up, exp_w_down, sh_w_gate, sh_w_up, sh_w_down):
    bsz, n_lat, d = x.shape
    n_ctx = ctx.shape[1]
    depth = ada_w.shape[0]
    assert bsz == SUBLANES and d == D_MODEL
    assert n_lat % ATT_TQ == 0 and n_lat % ML_CHUNK == 0 and n_ctx % ML_CHUNK == 0
    assert (n_lat * bsz) % ROW_TILE == 0 and (n_ctx * bsz) % ROW_TILE == 0
    n_steps = n_lat + n_ctx
    n_tok = n_steps * bsz
    n_lat_tiles = n_lat * bsz // ROW_TILE

    xs = jnp.concatenate([jnp.transpose(x, (1, 0, 2)), jnp.transpose(ctx, (1, 0, 2))], 0).reshape(n_tok, d)
    cc = jnp.concatenate([c, jnp.broadcast_to(c_ctx[None], (bsz, d))], 0)
    tabs = _rope_tables(n_lat, n_ctx)
    head_mean = jnp.kron(jnp.eye(ML_H, dtype=F32), jnp.full((ML_D, ML_D), 1.0 / ML_D, F32))

    for layer in range(depth):
        last = layer == depth - 1
        mod = _ada_mod(cc, ada_w[layer], ada_b[layer]).reshape(2, bsz, 6 * d)
        z = _inproj(xs, mod, _pad_w_in(w_in[layer]), n_lat_tiles)

        w_kv = mla_w_kv_up[layer].reshape(MLA_KVR, MLA_H, MLA_NOPE + MLA_V)
        wk_pad = _pad_heads(w_kv[:, :, :MLA_NOPE].reshape(MLA_KVR, MLA_H * MLA_NOPE), MLA_NOPE).astype(BF16)
        wv = _pad_heads(w_kv[:, :, MLA_NOPE:].reshape(MLA_KVR, MLA_W), MLA_V).astype(BF16)
        wq_pad = _pad_heads(mla_w_q_up[layer], MLA_NOPE + MLA_ROPE).astype(BF16)
        q, k, v = _mla_prep(z, tabs, mla_q_norm[layer].reshape(1, -1), wq_pad,
                            mla_kv_norm[layer].reshape(1, -1), wk_pad, wv)
        qt = q.reshape(n_steps, bsz * MLA_H * HEAD_PAD).T
        k2 = k.reshape(n_steps, bsz * MLA_H * HEAD_PAD)
        vt = v.reshape(n_steps, bsz * MLA_H * HEAD_PAD).T
        attn = _flash(qt, k2, vt, n_q_tiles=n_lat // ATT_TQ, tq=ATT_TQ, q_tile0=0, n_keys=n_steps, key_block=0)
        if last:
            attn_ctx = jnp.zeros((bsz * MLA_W, n_ctx), BF16)
        else:
            attn_ctx = _flash(qt, k2, vt, n_q_tiles=1, tq=n_ctx, q_tile0=n_lat // n_ctx, n_keys=n_ctx,
                              key_block=n_lat // n_ctx)
        attn = jnp.concatenate([attn, attn_ctx], 1).T.reshape(n_tok, MLA_W)

        bblk, cblk, a2 = _s5_params(s5_lambda_re[layer], s5_lambda_im[layer], s5_log_step[layer],
                                    s5_b_re[layer], s5_b_im[layer], s5_c_re[layer], s5_c_im[layer])
        y2 = _s5_scan(z, bblk, cblk, a2, n_lat, n_steps)

        gate_b_pad = jnp.zeros((1, 128), F32).at[0, :4 * ML_H].set(ml_gate_b[layer].reshape(4 * ML_H))
        cw, cb = ml_conv_w[layer], ml_conv_b[layer].reshape(1, -1)
        hf = _mlstm(z, cw, cb, gate_b_pad, rev=False, n_lat_steps=n_lat, n_steps=n_steps)
        hb = _mlstm(z, cw, cb, gate_b_pad, rev=True, n_lat_steps=n_lat, n_steps=n_steps)

        p = dict(s5_d=s5_d[layer].reshape(1, -1), glu_w=s5_glu_w[layer].astype(BF16),
                 glu_b=s5_glu_b[layer].reshape(1, -1), ml_norm_g=ml_norm_g[layer].reshape(1, -1),
                 head_mean=head_mean, w_out=w_out[layer].astype(BF16), ln1_g=ln1_g[layer].reshape(1, -1),
                 ln1_b=ln1_b[layer].reshape(1, -1), router_w=router_w[layer].T,
                 sh_w_gate=sh_w_gate[layer].astype(BF16), sh_w_up=sh_w_up[layer].astype(BF16),
                 sh_w_down=sh_w_down[layer].astype(BF16), ln2_g=ln2_g[layer].reshape(1, -1),
                 ln2_b=ln2_b[layer].reshape(1, -1))
        x1, f3, logits = _post(xs, mod, z, y2, attn, hf, hb, p, n_lat_tiles)

        top_e, gate, rank, counts = _route(logits, router_bias[layer])
        idx_tiles, blk_expert, pad_end, padded, n_rows = _dispatch(top_e, rank, counts)
        rows_in = _dispatch_rows(f3, idx_tiles, pad_end, padded, n_rows)
        rows_out = _experts(blk_expert, rows_in, exp_w_gate[layer].astype(BF16),
                            exp_w_up[layer].astype(BF16), exp_w_down[layer].astype(BF16))
        n_out_tiles = n_lat_tiles if last else n_tok // ROW_TILE
        xs = _final(x1, f3, gate.T, idx_tiles, rows_out, mod, p, n_lat_tiles, n_out_tiles)

    return jnp.transpose(xs[:n_lat * bsz].reshape(n_lat, bsz, d), (1, 0, 2))
```

```python
import functools
import math

import jax
import jax.numpy as jnp
from jax import lax
from jax.experimental import pallas as pl
from jax.experimental.pallas import tpu as pltpu

F32 = jnp.float32
BF16 = jnp.bfloat16

D_MODEL = 1024
GRID_W = 64
S5_W = 256
S5_GC = 16
S5_G = 16
S5_P = 64
S5_STATE = S5_G * S5_P
MLA_H = 8
MLA_NOPE = 64
MLA_ROPE = 32
MLA_V = 64
MLA_QR = 256
MLA_KVR = 128
MLA_W = MLA_H * MLA_V
MLA_SCALE = (MLA_NOPE + MLA_ROPE) ** -0.5
ROPE_AXIS = MLA_ROPE // 2
ROPE_BASE = 10000.0
HEAD_PAD = 128
ML_H = 4
ML_D = 64
ML_W = 256
ML_AUG = ML_W + 128
N_EXPERTS = 64
TOP_K = 8
N_EXPERT_GROUPS = 8
TOP_GROUPS = 4
EPG = 8
EXPERT_F = 256
ROUTED_SCALE = 2.5
DEPTH = 2
DEEPNORM_ALPHA = (2 * DEPTH) ** 0.25
LN_EPS = 1e-5
SUBLANES = 8

Z_U, Z_QC, Z_MLQ, Z_MLK, Z_MLV, Z_MLO, Z_KVC, Z_MISC = 0, 256, 512, 768, 1024, 1280, 1536, 1664
Z_W = 1792
MISC_GATES = 0
MISC_ROPE = 64

VMEM_LIMIT = 48 * 1024 * 1024
ROW_TILE = 512
S5_STEPS = 64
ML_CHUNK = 256
ATT_TQ = 1024
ATT_TK = 768
ATT_QSPLIT = 2
MOE_BLOCK = 512
MOE_TILE = 256
ROW_TILE_SHAPE = (SUBLANES, D_MODEL // SUBLANES)


def _cparams(sem):
    return pltpu.CompilerParams(dimension_semantics=sem, vmem_limit_bytes=VMEM_LIMIT)


def _ln(x):
    mu = jnp.mean(x, -1, keepdims=True)
    xc = x - mu
    var = jnp.mean(xc * xc, -1, keepdims=True)
    return xc * lax.rsqrt(var + LN_EPS)


def _store_tile_rows(ref, x):
    for s in range(SUBLANES):
        ref[:, s, :] = x[:, s * 128:(s + 1) * 128]


def _load_tile_rows(ref):
    return jnp.concatenate([ref[:, s, :] for s in range(SUBLANES)], -1)


def _modulate(h, sc, sh):
    tm, d = h.shape
    h3 = h.reshape(tm // SUBLANES, SUBLANES, d)
    return (h3 * (1.0 + sc)[None] + sh[None]).reshape(tm, d)


def _per_batch(v, g):
    tm, d = v.shape
    return (v.reshape(tm // SUBLANES, SUBLANES, d) * g[None]).reshape(tm, d)


def _ada_kernel(c_ref, w_ref, b_ref, o_ref):
    c = c_ref[...]
    s = c * jax.nn.sigmoid(c)
    o_ref[...] = jnp.dot(s.astype(BF16), w_ref[...].astype(BF16), preferred_element_type=F32) + b_ref[...]


def _ada_mod(cc, w, b):
    n = w.shape[1]
    tn = 1536
    return pl.pallas_call(
        _ada_kernel,
        grid=(n // tn,),
        in_specs=[pl.BlockSpec((16, D_MODEL), lambda j: (0, 0)),
                  pl.BlockSpec((D_MODEL, tn), lambda j: (0, j)),
                  pl.BlockSpec((1, tn), lambda j: (0, j))],
        out_specs=pl.BlockSpec((16, tn), lambda j: (0, j)),
        out_shape=jax.ShapeDtypeStruct((16, n), F32),
        compiler_params=_cparams(("arbitrary",)),
        name="ada_mod",
    )(cc, w, b.reshape(1, n))


def _inproj_kernel(x_ref, sc_ref, sh_ref, w_ref, z_ref):
    h = _modulate(_ln(x_ref[...]), sc_ref[0], sh_ref[0])
    z_ref[...] = jnp.dot(h.astype(BF16), w_ref[...], preferred_element_type=F32)


def _inproj(x, mod, w_pad, n_lat_tiles):
    n = x.shape[0]
    tm = ROW_TILE
    sel = lambda i: (i >= n_lat_tiles).astype(jnp.int32)
    return pl.pallas_call(
        _inproj_kernel,
        grid=(n // tm,),
        in_specs=[pl.BlockSpec((tm, D_MODEL), lambda i: (i, 0)),
                  pl.BlockSpec((1, SUBLANES, D_MODEL), lambda i: (sel(i), 0, 1)),
                  pl.BlockSpec((1, SUBLANES, D_MODEL), lambda i: (sel(i), 0, 0)),
                  pl.BlockSpec((D_MODEL, Z_W), lambda i: (0, 0))],
        out_specs=pl.BlockSpec((tm, Z_W), lambda i: (i, 0)),
        out_shape=jax.ShapeDtypeStruct((n, Z_W), F32),
        compiler_params=_cparams(("parallel",)),
        name="inproj",
    )(x, mod, mod, w_pad)


def _rope_block(x, tab_ref, base):
    tm = x.shape[0]

    def tab(i):
        t = tab_ref[base + i]
        return jnp.broadcast_to(t[:, None, :], (tm // SUBLANES, SUBLANES, HEAD_PAD)).reshape(tm, HEAD_PAD)

    return (x * tab(0) + pltpu.roll(x, HEAD_PAD - ROPE_AXIS // 2, 1) * tab(1)
            + pltpu.roll(x, ROPE_AXIS // 2, 1) * tab(2))


def _mla_prep_kernel(qc_ref, kvc_ref, misc_ref, tab_ref, qg_ref, wq_ref, kg_ref, wk_ref, wv_ref,
                     q_ref, k_ref, v_ref):
    qc = qc_ref[...]
    qn = qc * lax.rsqrt(jnp.mean(qc * qc, -1, keepdims=True) + 1e-6) * qg_ref[...]
    q = jnp.dot(qn.astype(BF16), wq_ref[...], preferred_element_type=F32)
    kvc = kvc_ref[...]
    kvn = (kvc * lax.rsqrt(jnp.mean(kvc * kvc, -1, keepdims=True) + 1e-6) * kg_ref[...]).astype(BF16)
    k = jnp.dot(kvn, wk_ref[...], preferred_element_type=F32)
    lane = lax.broadcasted_iota(jnp.int32, (1, MLA_H * HEAD_PAD), 1)
    ones_cols = jnp.where(lane % HEAD_PAD >= MLA_V, 1.0, 0.0)
    v_ref[...] = (jnp.dot(kvn, wv_ref[...], preferred_element_type=F32) + ones_cols).astype(BF16)
    kr = _rope_block(misc_ref[...], tab_ref, 3)
    for h in range(MLA_H):
        sl = slice(h * HEAD_PAD, (h + 1) * HEAD_PAD)
        q_ref[:, sl] = _rope_block(q[:, sl], tab_ref, 0).astype(BF16)
        k_ref[:, sl] = (k[:, sl] + kr).astype(BF16)


def _mla_prep(z, tabs, qg, wq_pad, kg, wk_pad, wv):
    n = z.shape[0]
    tm = ROW_TILE
    tt = tm // SUBLANES
    const = lambda i: (0, 0)
    return pl.pallas_call(
        _mla_prep_kernel,
        grid=(n // tm,),
        in_specs=[pl.BlockSpec((tm, MLA_QR), lambda i: (i, Z_QC // MLA_QR)),
                  pl.BlockSpec((tm, MLA_KVR), lambda i: (i, Z_KVC // MLA_KVR)),
                  pl.BlockSpec((tm, 128), lambda i: (i, Z_MISC // 128)),
                  pl.BlockSpec((6, tt, HEAD_PAD), lambda i: (0, i, 0)),
                  pl.BlockSpec((1, MLA_QR), const),
                  pl.BlockSpec((MLA_QR, MLA_H * HEAD_PAD), const),
                  pl.BlockSpec((1, MLA_KVR), const),
                  pl.BlockSpec((MLA_KVR, MLA_H * HEAD_PAD), const),
                  pl.BlockSpec((MLA_KVR, MLA_H * HEAD_PAD), const)],
        out_specs=[pl.BlockSpec((tm, MLA_H * HEAD_PAD), lambda i: (i, 0)),
                   pl.BlockSpec((tm, MLA_H * HEAD_PAD), lambda i: (i, 0)),
                   pl.BlockSpec((tm, MLA_H * HEAD_PAD), lambda i: (i, 0))],
        out_shape=[jax.ShapeDtypeStruct((n, MLA_H * HEAD_PAD), BF16),
                   jax.ShapeDtypeStruct((n, MLA_H * HEAD_PAD), BF16),
                   jax.ShapeDtypeStruct((n, MLA_H * HEAD_PAD), BF16)],
        compiler_params=_cparams(("parallel",)),
        name="mla_prep",
    )(z, z, z, tabs, qg, wq_pad, kg, wk_pad, wv)


def _flash_kernel(qt_ref, k_ref, vt_ref, o_ref, *, n_main, tk, tail, q_split):
    tq = qt_ref.shape[1]
    hq = tq // q_split
    chains = [(h, r) for h in range(2) for r in range(q_split)]
    nc = len(chains)

    def scores(start, size, c):
        h, r = chains[c]
        hs = slice(h * HEAD_PAD, (h + 1) * HEAD_PAD)
        return jnp.dot(k_ref[pl.ds(start, size), hs], qt_ref[hs, r * hq:(r + 1) * hq],
                       preferred_element_type=F32)

    def colmax(s):
        rows = s.shape[0]
        while rows % 16 == 0 and rows > 8:
            rows //= 2
            s = jnp.maximum(s[:rows], s[rows:])
        return jnp.max(s, 0, keepdims=True)

    def update(s, start, size, c, m, acc):
        h, _ = chains[c]
        hs = slice(h * HEAD_PAD, (h + 1) * HEAD_PAD)
        m_new = jnp.maximum(m, colmax(s))
        p = jnp.exp2(s - m_new).astype(BF16)
        acc = jnp.exp2(m - m_new) * acc + jnp.dot(vt_ref[hs, pl.ds(start, size)], p,
                                                  preferred_element_type=F32)
        return m_new, acc

    state = [(jnp.full((1, hq), -jnp.inf, F32), jnp.zeros((HEAD_PAD, hq), F32)) for _ in chains]
    if n_main:
        def body(j, carry):
            s_cur, st = carry[0], list(carry[1:])
            start = pl.multiple_of(j * tk, tk)
            nxt = pl.multiple_of(jnp.minimum(j + 1, n_main - 1) * tk, tk)
            for c in range(nc):
                s_next = scores(start, tk, c + 1) if c + 1 < nc else scores(nxt, tk, 0)
                st[c] = update(s_cur, start, tk, c, *st[c])
                s_cur = s_next
            return (s_cur, *st)

        carry = lax.fori_loop(0, n_main, body, (scores(0, tk, 0), *state))
        state = list(carry[1:])
    if tail:
        for c in range(nc):
            state[c] = update(scores(n_main * tk, tail, c), n_main * tk, tail, c, *state[c])
    for (h, r), (_, acc) in zip(chains, state):
        o_ref[h * MLA_V:(h + 1) * MLA_V, r * hq:(r + 1) * hq] = (
            acc[0:MLA_V] / acc[MLA_V:MLA_V + 1]).astype(BF16)


def _flash(qt, k2, vt, *, n_q_tiles, tq, q_tile0, n_keys, key_block):
    n_main, tail = divmod(n_keys, ATT_TK)
    pairs = MLA_H // 2
    kern = functools.partial(_flash_kernel, n_main=n_main, tk=ATT_TK, tail=tail, q_split=ATT_QSPLIT)
    return pl.pallas_call(
        kern,
        grid=(SUBLANES, pairs, n_q_tiles),
        in_specs=[pl.BlockSpec((2 * HEAD_PAD, tq), lambda b, p, i: (b * pairs + p, q_tile0 + i)),
                  pl.BlockSpec((n_keys, 2 * HEAD_PAD), lambda b, p, i: (key_block, b * pairs + p)),
                  pl.BlockSpec((2 * HEAD_PAD, n_keys), lambda b, p, i: (b * pairs + p, key_block))],
        out_specs=pl.BlockSpec((2 * MLA_V, tq), lambda b, p, i: (b * pairs + p, i)),
        out_shape=jax.ShapeDtypeStruct((SUBLANES * MLA_W, n_q_tiles * tq), BF16),
        compiler_params=_cparams(("parallel", "parallel", "arbitrary")),
        name="mla_flash",
    )(qt, k2, vt)


def _s5_kernel(u_ref, bb_ref, cc_ref, a_ref, y_ref, st_ref, bu_ref, *, steps):
    d = pl.program_id(0)

    @pl.when(pl.program_id(1) == 0)
    def _():
        st_ref[...] = jnp.zeros_like(st_ref)

    bu_ref[...] = jnp.dot(u_ref[...].astype(BF16), bb_ref[0], preferred_element_type=F32)
    ar = jnp.broadcast_to(a_ref[0, 0:1, :], (SUBLANES, S5_STATE))
    ai = jnp.broadcast_to(a_ref[0, 1:2, :], (SUBLANES, S5_STATE))

    def body(i, carry):
        re, im = carry
        t = jnp.where(d == 0, i, steps - 1 - i)
        row = pl.multiple_of(t * SUBLANES, SUBLANES)
        bur = bu_ref[pl.ds(row, SUBLANES), 0:S5_STATE]
        bui = bu_ref[pl.ds(row, SUBLANES), S5_STATE:2 * S5_STATE]
        nre = ar * re - ai * im + bur
        nim = ar * im + ai * re + bui
        bu_ref[pl.ds(row, SUBLANES), 0:S5_STATE] = nre
        bu_ref[pl.ds(row, SUBLANES), S5_STATE:2 * S5_STATE] = nim
        return nre, nim

    re, im = lax.fori_loop(0, steps, body, (st_ref[0], st_ref[1]), unroll=4)
    st_ref[0] = re
    st_ref[1] = im
    y_ref[0] = jnp.dot(bu_ref[...].astype(BF16), cc_ref[0], preferred_element_type=F32)


def _s5_scan(z, bblk, cblk, a2, n_lat_steps, n_steps):
    n = z.shape[0]
    rows = S5_STEPS * SUBLANES
    n_chunks = n_steps // S5_STEPS
    n_lat_chunks = n_lat_steps // S5_STEPS

    def blk(d, s):
        return jnp.where(d == 0, (s + n_lat_chunks) % n_chunks, n_chunks - 1 - s)

    return pl.pallas_call(
        functools.partial(_s5_kernel, steps=S5_STEPS),
        grid=(2, n_chunks),
        in_specs=[pl.BlockSpec((rows, S5_W), lambda d, s: (blk(d, s), Z_U // S5_W)),
                  pl.BlockSpec((1, S5_W, 2 * S5_STATE), lambda d, s: (d, 0, 0)),
                  pl.BlockSpec((1, 2 * S5_STATE, S5_W), lambda d, s: (d, 0, 0)),
                  pl.BlockSpec((1, 2, S5_STATE), lambda d, s: (d, 0, 0))],
        out_specs=pl.BlockSpec((1, rows, S5_W), lambda d, s: (d, blk(d, s), 0)),
        out_shape=jax.ShapeDtypeStruct((2, n, S5_W), F32),
        scratch_shapes=[pltpu.VMEM((2, SUBLANES, S5_STATE), F32),
                        pltpu.VMEM((rows, 2 * S5_STATE), F32)],
        compiler_params=_cparams(("arbitrary", "arbitrary")),
        name="s5_scan",
    )(z, bblk, cblk, a2)


def _shift_rows(x, sh, rev):
    z = jnp.zeros((sh, x.shape[1]), x.dtype)
    if rev:
        return jnp.concatenate([x[sh:], z], 0)
    return jnp.concatenate([z, x[:-sh]], 0)


def _mlstm_kernel(q_ref, k_ref, v_ref, g_ref, qp_ref, qn_ref, kp_ref, kn_ref, cw_ref, cb_ref, gb_ref,
                  h_ref, cst_ref, mst_ref, qk_ref, qs_ref, ks_ref, vs_ref, hs_ref, bc_ref, *, rev, n_lat_chunks, n_chunks):
    T = ML_CHUNK
    rows = T * SUBLANES
    s = pl.program_id(0)
    c = (n_chunks - 1 - s) if rev else (s + n_lat_chunks) % n_chunks

    @pl.when(s == 0)
    def _():
        cst_ref[...] = jnp.zeros_like(cst_ref)
        mst_ref[...] = jnp.zeros_like(mst_ref)

    first = jnp.logical_or(c == 0, c == n_lat_chunks)
    last = jnp.logical_or(c == n_lat_chunks - 1, c == n_chunks - 1)
    keep_p = jnp.where(first, 0.0, 1.0)
    keep_n = jnp.where(last, 0.0, 1.0)
    qk_ref[0:SUBLANES, 0:ML_W] = qp_ref[...] * keep_p
    qk_ref[0:SUBLANES, ML_W:2 * ML_W] = kp_ref[...] * keep_p
    qk_ref[SUBLANES:SUBLANES + rows, 0:ML_W] = q_ref[...]
    qk_ref[SUBLANES:SUBLANES + rows, ML_W:2 * ML_W] = k_ref[...]
    qk_ref[SUBLANES + rows:2 * SUBLANES + rows, 0:ML_W] = qn_ref[...] * keep_n
    qk_ref[SUBLANES + rows:2 * SUBLANES + rows, ML_W:2 * ML_W] = kn_ref[...] * keep_n
    cw = cw_ref[...]
    conv = (cb_ref[...] + qk_ref[0:rows, :] * cw[0:1] + qk_ref[SUBLANES:SUBLANES + rows, :] * cw[1:2]
            + qk_ref[2 * SUBLANES:2 * SUBLANES + rows, :] * cw[2:3])
    conv = conv * jax.nn.sigmoid(conv)
    for j in range(2):
        qs_ref[j] = conv[:, j * 128:(j + 1) * 128]
        ks_ref[j] = conv[:, ML_W + j * 128:ML_W + (j + 1) * 128] * (ML_D ** -0.5)
        vs_ref[j] = v_ref[:, j * 128:(j + 1) * 128]

    g = g_ref[...] + gb_ref[...]
    lane = lax.broadcasted_iota(jnp.int32, g.shape, 1)
    is_f = jnp.logical_and(lane >= 2 * ML_H, lane < 4 * ML_H)
    gm = jnp.where(is_f, jax.nn.log_sigmoid(g), g)
    cum = jnp.where(is_f, gm, 0.0)
    sh = SUBLANES
    while sh < rows:
        cum = cum + _shift_rows(cum, sh, rev)
        sh *= 2
    bc_ref[0] = gm
    bc_ref[1] = cum

    i_lane0 = ML_H if rev else 0
    f_lane0 = 3 * ML_H if rev else 2 * ML_H
    end_row = 0 if rev else T - 1
    ti = lax.broadcasted_iota(jnp.int32, (T, T), 0)
    si = lax.broadcasted_iota(jnp.int32, (T, T), 1)
    order = (si >= ti) if rev else (si <= ti)
    lane_w = lax.broadcasted_iota(jnp.int32, (1, ML_W), 1) // ML_D
    rowhead = lax.broadcasted_iota(jnp.int32, (ML_W, 1), 0) // ML_D
    colhead = lax.broadcasted_iota(jnp.int32, (1, ML_AUG), 1)
    colhead = jnp.where(colhead < ML_W, colhead // ML_D, colhead - ML_W)
    blockmask = rowhead == colhead
    ones_aug = jnp.ones((T, ML_AUG - ML_W), F32)

    def per_batch(b, _):
        sl = pl.ds(b, T, stride=SUBLANES)
        qb = jnp.concatenate([qs_ref[0, sl, :], qs_ref[1, sl, :]], -1)
        kb = jnp.concatenate([ks_ref[0, sl, :], ks_ref[1, sl, :]], -1)
        vb = jnp.concatenate([vs_ref[0, sl, :], vs_ref[1, sl, :]], -1)
        gcol = bc_ref[0, sl, :]
        ccol = bc_ref[1, sl, :]
        grow = gcol.T
        crow = ccol.T
        cst = cst_ref[b]
        qc = jnp.dot(qb.astype(BF16), cst.astype(BF16), preferred_element_type=F32)
        vaug = jnp.concatenate([vb, ones_aug], -1).astype(BF16)
        kb16 = kb.astype(BF16)
        num = jnp.zeros((T, ML_W), F32)
        den = jnp.zeros((T, ML_W), F32)
        stab = jnp.zeros((T, ML_W), F32)
        ke_scale = jnp.zeros((T, ML_W), F32)
        a_col = jnp.zeros((ML_W, 1), F32)
        s_col = jnp.zeros((ML_W, 1), F32)
        for h in range(ML_H):
            hm = lane_w == h
            il, fl = i_lane0 + h, f_lane0 + h
            b_col = ccol[:, fl:fl + 1]
            i_col = gcol[:, il:il + 1]
            b_row = crow[fl:fl + 1, :]
            i_row = grow[il:il + 1, :]
            b_tot = ccol[end_row:end_row + 1, fl:fl + 1]
            m_in = mst_ref[b, h][0:1, 0:1]
            ld = jnp.where(order, b_col - b_row + i_row, -jnp.inf)
            m_t = jnp.maximum(b_col + m_in, jnp.max(ld, -1, keepdims=True))
            dw = jnp.exp(ld - m_t)
            w_inter = jnp.exp(b_col + m_in - m_t)
            qh = jnp.where(hm, qb, 0.0).astype(BF16)
            sc = lax.dot_general(qh, kb16, (((1,), (1,)), ((), ())), preferred_element_type=F32) * dw
            pv = jnp.dot(sc.astype(BF16), vaug[:, 0:ML_W], preferred_element_type=F32)
            den_h = jnp.sum(sc, -1, keepdims=True) + w_inter * qc[:, ML_W + h:ML_W + h + 1]
            num = num + jnp.where(hm, pv + w_inter * qc[:, 0:ML_W], 0.0)
            den = den + jnp.where(hm, den_h, 0.0)
            stab = stab + jnp.where(hm, jnp.exp(-m_t), 0.0)
            w_end = b_tot - b_col + i_col
            m_loc = jnp.max(w_end, 0, keepdims=True)
            m_new = jnp.maximum(b_tot + m_in, m_loc)
            ke_scale = ke_scale + jnp.where(hm, jnp.exp(w_end - m_loc), 0.0)
            a_col = a_col + jnp.where(rowhead == h, jnp.exp(b_tot + m_in - m_new), 0.0)
            s_col = s_col + jnp.where(rowhead == h, jnp.exp(m_loc - m_new), 0.0)
            mst_ref[b, h] = jnp.broadcast_to(m_new, (SUBLANES, 128))
        hout = num / jnp.maximum(jnp.abs(den), stab)
        hs_ref[0, sl, :] = hout[:, 0:128]
        hs_ref[1, sl, :] = hout[:, 128:256]
        ke = (kb * ke_scale).astype(BF16)
        upd = lax.dot_general(ke, vaug, (((0,), (0,)), ((), ())), preferred_element_type=F32)
        cst_ref[b] = a_col * cst + jnp.where(blockmask, s_col * upd, 0.0)
        return 0

    lax.fori_loop(0, SUBLANES, per_batch, 0)
    h_ref[...] = jnp.concatenate([hs_ref[0], hs_ref[1]], -1)


def _mlstm(z, conv_w, conv_b, gate_b_pad, *, rev, n_lat_steps, n_steps):
    n = z.shape[0]
    T = ML_CHUNK
    rows = T * SUBLANES
    n_chunks = n_steps // T
    n_lat_chunks = n_lat_steps // T
    hb = rows // SUBLANES
    n_hblk = n // SUBLANES

    def blk(s):
        return (n_chunks - 1 - s) if rev else (s + n_lat_chunks) % n_chunks

    prev = lambda s: jnp.maximum(blk(s) * hb - 1, 0)
    nxt = lambda s: jnp.minimum((blk(s) + 1) * hb, n_hblk - 1)
    const = lambda s: (0, 0)
    kern = functools.partial(_mlstm_kernel, rev=rev, n_lat_chunks=n_lat_chunks, n_chunks=n_chunks)
    return pl.pallas_call(
        kern,
        grid=(n_chunks,),
        in_specs=[pl.BlockSpec((rows, ML_W), lambda s: (blk(s), Z_MLQ // ML_W)),
                  pl.BlockSpec((rows, ML_W), lambda s: (blk(s), Z_MLK // ML_W)),
                  pl.BlockSpec((rows, ML_W), lambda s: (blk(s), Z_MLV // ML_W)),
                  pl.BlockSpec((rows, 128), lambda s: (blk(s), Z_MISC // 128)),
                  pl.BlockSpec((SUBLANES, ML_W), lambda s: (prev(s), Z_MLQ // ML_W)),
                  pl.BlockSpec((SUBLANES, ML_W), lambda s: (nxt(s), Z_MLQ // ML_W)),
                  pl.BlockSpec((SUBLANES, ML_W), lambda s: (prev(s), Z_MLK // ML_W)),
                  pl.BlockSpec((SUBLANES, ML_W), lambda s: (nxt(s), Z_MLK // ML_W)),
                  pl.BlockSpec((3, 2 * ML_W), const),
                  pl.BlockSpec((1, 2 * ML_W), const),
                  pl.BlockSpec((1, 128), const)],
        out_specs=pl.BlockSpec((rows, ML_W), lambda s: (blk(s), 0)),
        out_shape=jax.ShapeDtypeStruct((n, ML_W), F32),
        scratch_shapes=[pltpu.VMEM((SUBLANES, ML_W, ML_AUG), F32),
                        pltpu.VMEM((SUBLANES, ML_H, SUBLANES, 128), F32),
                        pltpu.VMEM((rows + 2 * SUBLANES, 2 * ML_W), F32),
                        pltpu.VMEM((2, rows, 128), F32),
                        pltpu.VMEM((2, rows, 128), F32),
                        pltpu.VMEM((2, rows, 128), F32),
                        pltpu.VMEM((2, rows, 128), F32),
                        pltpu.VMEM((2, rows, 128), F32)],
        compiler_params=_cparams(("arbitrary",)),
        name="mlstm_rev" if rev else "mlstm_fwd",
    )(z, z, z, z, z, z, z, z, conv_w, conv_b, gate_b_pad)


def _post_kernel(x_ref, g1_ref, sc_ref, sh_ref, u_ref, y_ref, o_ref, attn_ref, hf_ref, hb_ref,
                 d_ref, gw_ref, gb_ref, hn_ref, pm_ref, wo_ref, l1g_ref, l1b_ref, rw_ref,
                 x1_ref, f_ref, lg_ref):
    y = y_ref[0] + y_ref[1] + u_ref[...] * d_ref[...]
    g = jax.nn.gelu(y)
    s5 = g * jax.nn.sigmoid(jnp.dot(g.astype(BF16), gw_ref[...], preferred_element_type=F32) + gb_ref[...])
    hh = jax.nn.sigmoid(o_ref[...]) * (hf_ref[...] + hb_ref[...])
    pm = pm_ref[...]
    mu = jnp.dot(hh, pm, preferred_element_type=F32, precision=lax.Precision.HIGHEST)
    hc = hh - mu
    var = jnp.dot(hc * hc, pm, preferred_element_type=F32, precision=lax.Precision.HIGHEST)
    ml = hc * lax.rsqrt(var + LN_EPS) * hn_ref[...]
    mix = jnp.concatenate([s5.astype(BF16), attn_ref[...], ml.astype(BF16)], -1)
    yo = jnp.dot(mix, wo_ref[...], preferred_element_type=F32)
    x1 = _ln(DEEPNORM_ALPHA * x_ref[...] + _per_batch(yo, g1_ref[0])) * l1g_ref[...] + l1b_ref[...]
    x1_ref[...] = x1
    f = _modulate(_ln(x1), sc_ref[0], sh_ref[0])
    _store_tile_rows(f_ref, f)
    lg_ref[...] = lax.dot_general(rw_ref[...], f, (((1,), (1,)), ((), ())), preferred_element_type=F32,
                                  precision=lax.Precision.HIGHEST)


def _post(x, mod, z, y2, attn, hf, hb, p, n_lat_tiles):
    n = x.shape[0]
    tm = ROW_TILE
    sel = lambda i: (i >= n_lat_tiles).astype(jnp.int32)
    const = lambda i: (0, 0)
    row = lambda w: pl.BlockSpec((tm, w), lambda i: (i, 0))
    return pl.pallas_call(
        _post_kernel,
        grid=(n // tm,),
        in_specs=[row(D_MODEL),
                  pl.BlockSpec((1, SUBLANES, D_MODEL), lambda i: (sel(i), 0, 2)),
                  pl.BlockSpec((1, SUBLANES, D_MODEL), lambda i: (sel(i), 0, 4)),
                  pl.BlockSpec((1, SUBLANES, D_MODEL), lambda i: (sel(i), 0, 3)),
                  pl.BlockSpec((tm, S5_W), lambda i: (i, Z_U // S5_W)),
                  pl.BlockSpec((2, tm, S5_W), lambda i: (0, i, 0)),
                  pl.BlockSpec((tm, ML_W), lambda i: (i, Z_MLO // ML_W)),
                  row(MLA_W), row(ML_W), row(ML_W),
                  pl.BlockSpec((1, S5_W), const),
                  pl.BlockSpec((S5_W, S5_W), const),
                  pl.BlockSpec((1, S5_W), const),
                  pl.BlockSpec((1, ML_W), const),
                  pl.BlockSpec((ML_W, ML_W), const),
                  pl.BlockSpec((D_MODEL, D_MODEL), const),
                  pl.BlockSpec((1, D_MODEL), const),
                  pl.BlockSpec((1, D_MODEL), const),
                  pl.BlockSpec((N_EXPERTS, D_MODEL), const)],
        out_specs=[row(D_MODEL), pl.BlockSpec((tm,) + ROW_TILE_SHAPE, lambda i: (i, 0, 0)),
                   pl.BlockSpec((N_EXPERTS, tm), lambda i: (0, i))],
        out_shape=[jax.ShapeDtypeStruct((n, D_MODEL), F32),
                   jax.ShapeDtypeStruct((n,) + ROW_TILE_SHAPE, F32),
                   jax.ShapeDtypeStruct((N_EXPERTS, n), F32)],
        compiler_params=_cparams(("parallel",)),
        name="post_mix",
    )(x, mod, mod, mod, z, y2, z, attn, hf, hb,
      p["s5_d"], p["glu_w"], p["glu_b"], p["ml_norm_g"], p["head_mean"], p["w_out"], p["ln1_g"], p["ln1_b"],
      p["router_w"])


def _expert_kernel(be_ref, x_ref, wg_ref, wu_ref, wd_ref, o_ref):
    x = _load_tile_rows(x_ref).astype(BF16)
    a = jnp.dot(x, wg_ref[0], preferred_element_type=F32)
    u = jnp.dot(x, wu_ref[0], preferred_element_type=F32)
    hmid = (a * jax.nn.sigmoid(a) * u).astype(BF16)
    _store_tile_rows(o_ref, jnp.dot(hmid, wd_ref[0], preferred_element_type=F32))


def _experts(blk_expert, xs, wg, wu, wd):
    n_rows = xs.shape[0]
    tb = MOE_BLOCK
    rows = pl.BlockSpec((tb,) + ROW_TILE_SHAPE, lambda i, be: (i, 0, 0))
    grid_spec = pltpu.PrefetchScalarGridSpec(
        num_scalar_prefetch=1,
        grid=(n_rows // tb,),
        in_specs=[rows,
                  pl.BlockSpec((1, D_MODEL, EXPERT_F), lambda i, be: (be[i], 0, 0)),
                  pl.BlockSpec((1, D_MODEL, EXPERT_F), lambda i, be: (be[i], 0, 0)),
                  pl.BlockSpec((1, EXPERT_F, D_MODEL), lambda i, be: (be[i], 0, 0))],
        out_specs=rows,
    )
    return pl.pallas_call(
        _expert_kernel,
        grid_spec=grid_spec,
        out_shape=jax.ShapeDtypeStruct((n_rows,) + ROW_TILE_SHAPE, F32),
        compiler_params=_cparams(("arbitrary",)),
        name="moe_experts",
    )(blk_expert, xs, wg, wu, wd)


def _idx_copy(idx_hbm, idx_smem, sem, tile, slot, width):
    return pltpu.make_async_copy(idx_hbm.at[tile], idx_smem.at[pl.ds(pl.multiple_of(slot * width, width), width)],
                                 sem.at[slot])


def _row_groups(tm, base, fn):
    def trip(g, c):
        t0 = pl.multiple_of(g * SUBLANES, SUBLANES)
        for j in range(SUBLANES):
            for k in range(TOP_K):
                fn(k, t0 + j, base + (k * tm + j) + t0)
        return c

    lax.fori_loop(0, tm // SUBLANES, trip, 0)


def _dispatch_kernel(pe_ref, pc_ref, f_ref, idx_hbm, xs_hbm, idx_smem, zero_ref, isem, zsem, rsem, *, tm):
    i = pl.program_id(0)
    n = pl.num_programs(0)
    slot = i % 2
    tb = MOE_BLOCK
    width = TOP_K * tm

    def row_copy(t, r):
        return pltpu.make_async_copy(f_ref.at[t], xs_hbm.at[r], rsem)

    def pad_copy(e):
        start = pl.multiple_of(pe_ref[e] - tb, tb)
        return pltpu.make_async_copy(zero_ref, xs_hbm.at[pl.ds(start, tb)], zsem)

    @pl.when(i == 0)
    def _():
        _idx_copy(idx_hbm, idx_smem, isem, 0, 0, width).start()
        zero_ref[...] = jnp.zeros_like(zero_ref)

        def fill(e, c):
            @pl.when(pc_ref[e] > 0)
            def _():
                pad_copy(e).start()
            return c

        def drain(e, c):
            @pl.when(pc_ref[e] > 0)
            def _():
                pad_copy(e).wait()
            return c

        lax.fori_loop(0, N_EXPERTS, fill, 0)
        lax.fori_loop(0, N_EXPERTS, drain, 0)

    _idx_copy(idx_hbm, idx_smem, isem, i, slot, width).wait()

    @pl.when(i + 1 < n)
    def _():
        _idx_copy(idx_hbm, idx_smem, isem, i + 1, 1 - slot, width).start()

    base = slot * width
    _row_groups(tm, base, lambda k, t, a: row_copy(t, idx_smem[a]).start(priority=k % 2))
    _row_groups(tm, base, lambda k, t, a: row_copy(t, idx_smem[a]).wait())


def _dispatch_rows(f3, idx_tiles, pad_end, padded, n_rows):
    n = f3.shape[0]
    tm = MOE_TILE
    grid_spec = pltpu.PrefetchScalarGridSpec(
        num_scalar_prefetch=2,
        grid=(n // tm,),
        in_specs=[pl.BlockSpec((tm,) + ROW_TILE_SHAPE, lambda i, pe, pc: (i, 0, 0)),
                  pl.BlockSpec(memory_space=pl.ANY)],
        out_specs=pl.BlockSpec(memory_space=pl.ANY),
        scratch_shapes=[pltpu.SMEM((2 * TOP_K * tm,), jnp.int32),
                        pltpu.VMEM((MOE_BLOCK,) + ROW_TILE_SHAPE, F32),
                        pltpu.SemaphoreType.DMA((2,)),
                        pltpu.SemaphoreType.DMA,
                        pltpu.SemaphoreType.DMA],
    )
    return pl.pallas_call(
        functools.partial(_dispatch_kernel, tm=tm),
        grid_spec=grid_spec,
        out_shape=jax.ShapeDtypeStruct((n_rows,) + ROW_TILE_SHAPE, F32),
        compiler_params=_cparams(("arbitrary",)),
        name="moe_dispatch",
    )(pad_end, padded, f3, idx_tiles)


def _final_kernel(x1_ref, f_ref, gate_ref, g2_ref, wg_ref, wu_ref, wd_ref, lg_ref, lb_ref, idx_hbm, ys_hbm,
                  o_ref, idx_smem, buf_ref, isem, rsem, *, tm):
    i = pl.program_id(0)
    n = pl.num_programs(0)
    width = TOP_K * tm

    def row_copy(slot, k, t, r):
        return pltpu.make_async_copy(ys_hbm.at[r], buf_ref.at[slot, k, t], rsem.at[slot])

    def gather(tile_slot):
        base = tile_slot * width
        _row_groups(tm, base, lambda k, t, a: row_copy(tile_slot, k, t, idx_smem[a]).start(priority=k % 2))

    @pl.when(i == 0)
    def _():
        _idx_copy(idx_hbm, idx_smem, isem, 0, 0, width).start()
        _idx_copy(idx_hbm, idx_smem, isem, 0, 0, width).wait()
        gather(0)

        @pl.when(n > 1)
        def _():
            _idx_copy(idx_hbm, idx_smem, isem, 1, 1, width).start()

    cur = i % 2

    @pl.when(i + 1 < n)
    def _():
        _idx_copy(idx_hbm, idx_smem, isem, i + 1, 1 - cur, width).wait()
        gather(1 - cur)

    @pl.when(i + 2 < n)
    def _():
        _idx_copy(idx_hbm, idx_smem, isem, i + 2, cur, width).start()

    f = _load_tile_rows(f_ref).astype(BF16)
    a = jnp.dot(f, wg_ref[...], preferred_element_type=F32)
    u = jnp.dot(f, wu_ref[...], preferred_element_type=F32)
    ffn = jnp.dot((a * jax.nn.sigmoid(a) * u).astype(BF16), wd_ref[...], preferred_element_type=F32)
    _row_groups(tm, 0, lambda k, t, a: row_copy(cur, k, t, 0).wait())
    gate = gate_ref[...]
    for k in range(TOP_K):
        ffn = ffn + gate[:, k:k + 1] * _load_tile_rows(buf_ref.at[cur, k])
    o_ref[...] = _ln(DEEPNORM_ALPHA * x1_ref[...] + _per_batch(ffn, g2_ref[0])) * lg_ref[...] + lb_ref[...]


def _final(x1, f3, gate_t, idx_tiles, ys, mod, p, n_lat_tiles, n_out_tiles):
    tm = MOE_TILE
    n_lat_tiles = n_lat_tiles * ROW_TILE // tm
    n_out_tiles = n_out_tiles * ROW_TILE // tm
    sel = lambda i: (i >= n_lat_tiles).astype(jnp.int32)
    const = lambda i: (0, 0)
    row = lambda w: pl.BlockSpec((tm, w), lambda i: (i, 0))
    return pl.pallas_call(
        functools.partial(_final_kernel, tm=tm),
        grid=(n_out_tiles,),
        in_specs=[row(D_MODEL), pl.BlockSpec((tm,) + ROW_TILE_SHAPE, lambda i: (i, 0, 0)), row(TOP_K),
                  pl.BlockSpec((1, SUBLANES, D_MODEL), lambda i: (sel(i), 0, 5)),
                  pl.BlockSpec((D_MODEL, EXPERT_F), const),
                  pl.BlockSpec((D_MODEL, EXPERT_F), const),
                  pl.BlockSpec((EXPERT_F, D_MODEL), const),
                  pl.BlockSpec((1, D_MODEL), const),
                  pl.BlockSpec((1, D_MODEL), const),
                  pl.BlockSpec(memory_space=pl.ANY),
                  pl.BlockSpec(memory_space=pl.ANY)],
        out_specs=row(D_MODEL),
        out_shape=jax.ShapeDtypeStruct((n_out_tiles * tm, D_MODEL), F32),
        scratch_shapes=[pltpu.SMEM((2 * TOP_K * tm,), jnp.int32),
                        pltpu.VMEM((2, TOP_K, tm) + ROW_TILE_SHAPE, F32),
                        pltpu.SemaphoreType.DMA((2,)),
                        pltpu.SemaphoreType.DMA((2,))],
        compiler_params=_cparams(("arbitrary",)),
        name="final_ffn",
    )(x1, f3, gate_t, mod, p["sh_w_gate"], p["sh_w_up"], p["sh_w_down"], p["ln2_g"], p["ln2_b"], idx_tiles, ys)


def _route_kernel(lg_ref, bias_ref, tri_ref, e_ref, g_ref, r_ref, cnt_ref, carry_ref):
    @pl.when(pl.program_id(0) == 0)
    def _():
        carry_ref[...] = jnp.zeros_like(carry_ref)

    tm = lg_ref.shape[1]
    neg = -jnp.inf
    s = jax.nn.sigmoid(lg_ref[...])
    g3 = (s + bias_ref[...]).reshape(N_EXPERT_GROUPS, EPG, tm)
    io3 = lax.broadcasted_iota(jnp.int32, (N_EXPERT_GROUPS, EPG, tm), 1)
    m1 = jnp.max(g3, 1, keepdims=True)
    f1 = jnp.min(jnp.where(g3 == m1, io3, EPG), 1, keepdims=True)
    m2 = jnp.max(jnp.where(io3 == f1, neg, g3), 1, keepdims=True)
    gs = m1 + m2
    iog = lax.broadcasted_iota(jnp.int32, (N_EXPERT_GROUPS, 1, tm), 0)
    cur = gs
    kth = gs
    for _ in range(TOP_GROUPS):
        kth = jnp.max(cur, 0, keepdims=True)
        fi = jnp.min(jnp.where(cur == kth, iog, N_EXPERT_GROUPS), 0, keepdims=True)
        cur = jnp.where(iog == fi, neg, cur)
    cand = jnp.where(gs >= kth, g3, neg).reshape(N_EXPERTS, tm)
    io = lax.broadcasted_iota(jnp.int32, (N_EXPERTS, tm), 0)
    memb = jnp.zeros((N_EXPERTS, tm), F32)
    es, gates, hots = [], [], []
    for _ in range(TOP_K):
        mk = jnp.max(cand, 0, keepdims=True)
        ik = jnp.min(jnp.where(cand == mk, io, N_EXPERTS), 0, keepdims=True)
        oh = io == ik
        gates.append(jnp.sum(jnp.where(oh, s, 0.0), 0, keepdims=True))
        es.append(ik)
        hots.append(oh)
        cand = jnp.where(oh, neg, cand)
        memb = memb + jnp.where(oh, 1.0, 0.0)
    gsum = gates[0]
    for gk in gates[1:]:
        gsum = gsum + gk
    g_ref[...] = jnp.concatenate(gates, 0) / gsum * ROUTED_SCALE
    e_ref[...] = jnp.concatenate(es, 0)
    pref = jnp.dot(memb.astype(BF16), tri_ref[...], preferred_element_type=F32) + carry_ref[:, 0:1]
    ranks = [jnp.sum(jnp.where(oh, pref, 0.0), 0, keepdims=True) for oh in hots]
    r_ref[...] = jnp.concatenate(ranks, 0).astype(jnp.int32)
    total = carry_ref[...] + jnp.sum(memb, 1, keepdims=True)
    carry_ref[...] = total
    cnt_ref[...] = total


def _route(logits_t, router_bias):
    n = logits_t.shape[1]
    tm = ROW_TILE
    tri = (jnp.arange(tm)[:, None] < jnp.arange(tm)[None, :]).astype(BF16)
    const = lambda i: (0, 0)
    col = pl.BlockSpec((TOP_K, tm), lambda i: (0, i))
    top_e, gate, rank, cnt = pl.pallas_call(
        _route_kernel,
        grid=(n // tm,),
        in_specs=[pl.BlockSpec((N_EXPERTS, tm), lambda i: (0, i)),
                  pl.BlockSpec((N_EXPERTS, 1), const),
                  pl.BlockSpec((tm, tm), const)],
        out_specs=[col, col, col, pl.BlockSpec((N_EXPERTS, 128), const)],
        out_shape=[jax.ShapeDtypeStruct((TOP_K, n), jnp.int32),
                   jax.ShapeDtypeStruct((TOP_K, n), F32),
                   jax.ShapeDtypeStruct((TOP_K, n), jnp.int32),
                   jax.ShapeDtypeStruct((N_EXPERTS, 128), F32)],
        scratch_shapes=[pltpu.VMEM((N_EXPERTS, 128), F32)],
        compiler_params=_cparams(("arbitrary",)),
        name="moe_route",
    )(logits_t, router_bias.astype(F32).reshape(N_EXPERTS, 1), tri)
    return top_e, gate, rank, cnt[:, 0].astype(jnp.int32)


def _dispatch(top_e, rank, counts):
    n_tok = top_e.shape[1]
    tb = MOE_BLOCK
    padded = (counts + tb - 1) // tb * tb
    pad_end = jnp.cumsum(padded)
    pad_start = pad_end - padded
    hot = top_e[:, :, None] == jnp.arange(N_EXPERTS, dtype=top_e.dtype)
    dest = (jnp.sum(jnp.where(hot, pad_start.astype(jnp.int32), 0), -1) + rank).astype(jnp.int32)
    n_rows = -(-(n_tok * TOP_K + N_EXPERTS * (tb - 1)) // tb) * tb
    n_blocks = n_rows // tb
    blk_start = jnp.arange(n_blocks, dtype=jnp.int32) * tb
    blk_expert = jnp.minimum(jnp.sum((pad_end[None, :] <= blk_start[:, None]).astype(jnp.int32), 1),
                             N_EXPERTS - 1)
    tm = MOE_TILE
    idx_tiles = jnp.transpose(dest.reshape(TOP_K, n_tok // tm, tm), (1, 0, 2)).reshape(n_tok // tm, TOP_K * tm)
    return idx_tiles, blk_expert, pad_end.astype(jnp.int32), padded.astype(jnp.int32), n_rows


def _pad_w_in(w_in):
    s5u, qc, kvc, kr, mq, mk, mv, mo, gates = jnp.split(
        w_in, [256, 512, 640, 672, 928, 1184, 1440, 1696], axis=1)
    misc = jnp.zeros((D_MODEL, 128), w_in.dtype)
    misc = misc.at[:, MISC_GATES:MISC_GATES + 16].set(gates).at[:, MISC_ROPE:MISC_ROPE + MLA_ROPE].set(kr)
    return jnp.concatenate([s5u, qc, mq, mk, mv, mo, kvc, misc], axis=1).astype(BF16)


def _pad_heads(w, width):
    k = w.shape[0]
    w3 = w.reshape(k, MLA_H, width)
    return jnp.pad(w3, ((0, 0), (0, 0), (0, HEAD_PAD - width))).reshape(k, MLA_H * HEAD_PAD)


def _rope_tables(n_lat, n_ctx):
    t = jnp.arange(n_lat)
    half = ROPE_AXIS // 2
    inv_freq = ROPE_BASE ** (-jnp.arange(half, dtype=F32) / half)
    ang_r = (t // GRID_W).astype(F32)[:, None] * inv_freq
    ang_c = (t % GRID_W).astype(F32)[:, None] * inv_freq
    ang = jnp.concatenate([ang_r, ang_r, ang_c, ang_c], -1)
    ang = jnp.concatenate([ang, jnp.zeros((n_ctx, MLA_ROPE), F32)], 0)
    cos, sin = jnp.cos(ang), jnp.sin(ang)
    first = (jnp.arange(MLA_ROPE) % ROPE_AXIS) < half

    def place(v, fill):
        out = jnp.full((v.shape[0], HEAD_PAD), fill, F32)
        return out.at[:, MLA_NOPE:MLA_NOPE + MLA_ROPE].set(v)

    q_scale = MLA_SCALE * math.log2(math.e)
    c_q = place(cos, 1.0) * q_scale
    m_q = place(jnp.where(first, -sin, 0.0), 0.0) * q_scale
    p_q = place(jnp.where(first, 0.0, sin), 0.0) * q_scale
    c_k = place(cos, 0.0)
    m_k = place(jnp.where(first, -sin, 0.0), 0.0)
    p_k = place(jnp.where(first, 0.0, sin), 0.0)
    return jnp.stack([c_q, m_q, p_q, c_k, m_k, p_k], 0)


def _s5_params(lam_re, lam_im, log_step, b_re, b_im, c_re, c_im):
    lr, li = lam_re.astype(F32), lam_im.astype(F32)
    step = jnp.exp(log_step.astype(F32))[..., None]
    mag = jnp.exp(lr * step)
    ar, ai = mag * jnp.cos(li * step), mag * jnp.sin(li * step)
    nr, ni = ar - 1.0, ai
    den = lr * lr + li * li
    cr, ci = ((nr * lr + ni * li) / den)[..., None], ((ni * lr - nr * li) / den)[..., None]
    br, bi = b_re.astype(F32), b_im.astype(F32)
    bbr, bbi = cr * br - ci * bi, cr * bi + ci * br
    eye = jnp.eye(S5_G, dtype=F32)
    b_blk = lambda v: jnp.einsum("dgcp,gh->dgchp", jnp.transpose(v, (0, 1, 3, 2)), eye).reshape(2, S5_W, S5_STATE)
    bblk = jnp.concatenate([b_blk(bbr), b_blk(bbi)], -1)
    c_blk = lambda v: jnp.einsum("dgpc,gh->dgphc", jnp.transpose(v.astype(F32), (0, 1, 3, 2)), eye).reshape(
        2, S5_STATE, S5_W)
    cblk = jnp.concatenate([c_blk(c_re), -c_blk(c_im)], 1)
    a2 = jnp.stack([ar.reshape(2, S5_STATE), ai.reshape(2, S5_STATE)], 1)
    return bblk.astype(BF16), cblk.astype(BF16), a2


def kernel(x, c, ctx, c_ctx, ada_w, ada_b, w_in, s5_lambda_re, s5_lambda_im, s5_log_step, s5_b_re, s5_b_im, s5_c_re, s5_c_im, s5_d, s5_glu_w, s5_glu_b, mla_q_norm, mla_w_q_up, mla_kv_norm, mla_w_kv_up, ml_conv_w, ml_conv_b, ml_gate_b, ml_norm_g, w_out, ln1_g, ln1_b, ln2_g, ln2_b, router_w, router_bias, exp_w_gate, exp_w_up, exp_w_down, sh_w_gate, sh_w_up, sh_w_down):
    bsz, n_lat, d = x.shape
    n_ctx = ctx.shape[1]
    depth = ada_w.shape[0]
    assert bsz == SUBLANES and d == D_MODEL
    assert n_lat % ATT_TQ == 0 and n_lat % ML_CHUNK == 0 and n_ctx % ML_CHUNK == 0
    assert (n_lat * bsz) % ROW_TILE == 0 and (n_ctx * bsz) % ROW_TILE == 0
    n_steps = n_lat + n_ctx
    n_tok = n_steps * bsz
    n_lat_tiles = n_lat * bsz // ROW_TILE

    xs = jnp.concatenate([jnp.transpose(x, (1, 0, 2)), jnp.transpose(ctx, (1, 0, 2))], 0).reshape(n_tok, d)
    cc = jnp.concatenate([c, jnp.broadcast_to(c_ctx[None], (bsz, d))], 0)
    tabs = _rope_tables(n_lat, n_ctx)
    head_mean = jnp.kron(jnp.eye(ML_H, dtype=F32), jnp.full((ML_D, ML_D), 1.0 / ML_D, F32))

    for layer in range(depth):
        last = layer == depth - 1
        mod = _ada_mod(cc, ada_w[layer], ada_b[layer]).reshape(2, bsz, 6 * d)
        z = _inproj(xs, mod, _pad_w_in(w_in[layer]), n_lat_tiles)

        w_kv = mla_w_kv_up[layer].reshape(MLA_KVR, MLA_H, MLA_NOPE + MLA_V)
        wk_pad = _pad_heads(w_kv[:, :, :MLA_NOPE].reshape(MLA_KVR, MLA_H * MLA_NOPE), MLA_NOPE).astype(BF16)
        wv = _pad_heads(w_kv[:, :, MLA_NOPE:].reshape(MLA_KVR, MLA_W), MLA_V).astype(BF16)
        wq_pad = _pad_heads(mla_w_q_up[layer], MLA_NOPE + MLA_ROPE).astype(BF16)
        q, k, v = _mla_prep(z, tabs, mla_q_norm[layer].reshape(1, -1), wq_pad,
                            mla_kv_norm[layer].reshape(1, -1), wk_pad, wv)
        qt = q.reshape(n_steps, bsz * MLA_H * HEAD_PAD).T
        k2 = k.reshape(n_steps, bsz * MLA_H * HEAD_PAD)
        vt = v.reshape(n_steps, bsz * MLA_H * HEAD_PAD).T
        attn = _flash(qt, k2, vt, n_q_tiles=n_lat // ATT_TQ, tq=ATT_TQ, q_tile0=0, n_keys=n_steps, key_block=0)
        if last:
            attn_ctx = jnp.zeros((bsz * MLA_W, n_ctx), BF16)
        else:
            attn_ctx = _flash(qt, k2, vt, n_q_tiles=1, tq=n_ctx, q_tile0=n_lat // n_ctx, n_keys=n_ctx,
                              key_block=n_lat // n_ctx)
        attn = jnp.concatenate([attn, attn_ctx], 1).T.reshape(n_tok, MLA_W)

        bblk, cblk, a2 = _s5_params(s5_lambda_re[layer], s5_lambda_im[layer], s5_log_step[layer],
                                    s5_b_re[layer], s5_b_im[layer], s5_c_re[layer], s5_c_im[layer])
        y2 = _s5_scan(z, bblk, cblk, a2, n_lat, n_steps)

        gate_b_pad = jnp.zeros((1, 128), F32).at[0, :4 * ML_H].set(ml_gate_b[layer].reshape(4 * ML_H))
        cw, cb = ml_conv_w[layer], ml_conv_b[layer].reshape(1, -1)
        hf = _mlstm(z, cw, cb, gate_b_pad, rev=False, n_lat_steps=n_lat, n_steps=n_steps)
        hb = _mlstm(z, cw, cb, gate_b_pad, rev=True, n_lat_steps=n_lat, n_steps=n_steps)

        p = dict(s5_d=s5_d[layer].reshape(1, -1), glu_w=s5_glu_w[layer].astype(BF16),
                 glu_b=s5_glu_b[layer].reshape(1, -1), ml_norm_g=ml_norm_g[layer].reshape(1, -1),
                 head_mean=head_mean, w_out=w_out[layer].astype(BF16), ln1_g=ln1_g[layer].reshape(1, -1),
                 ln1_b=ln1_b[layer].reshape(1, -1), router_w=router_w[layer].T,
                 sh_w_gate=sh_w_gate[layer].astype(BF16), sh_w_up=sh_w_up[layer].astype(BF16),
                 sh_w_down=sh_w_down[layer].astype(BF16), ln2_g=ln2_g[layer].reshape(1, -1),
                 ln2_b=ln2_b[layer].reshape(1, -1))
        x1, f3, logits = _post(xs, mod, z, y2, attn, hf, hb, p, n_lat_tiles)

        top_e, gate, rank, counts = _route(logits, router_bias[layer])
        idx_tiles, blk_expert, pad_end, padded, n_rows = _dispatch(top_e, rank, counts)
        rows_in = _dispatch_rows(f3, idx_tiles, pad_end, padded, n_rows)
        rows_out = _experts(blk_expert, rows_in, exp_w_gate[layer].astype(BF16),
                            exp_w_up[layer].astype(BF16), exp_w_down[layer].astype(BF16))
        n_out_tiles = n_lat_tiles if last else n_tok // ROW_TILE
        xs = _final(x1, f3, gate.T, idx_tiles, rows_out, mod, p, n_lat_tiles, n_out_tiles)

    return jnp.transpose(xs[:n_lat * bsz].reshape(n_lat, bsz, d), (1, 0, 2))
```

```python
import functools
import math

import jax
import jax.numpy as jnp
from jax import lax
from jax.experimental import pallas as pl
from jax.experimental.pallas import tpu as pltpu

F32 = jnp.float32
BF16 = jnp.bfloat16

D_MODEL = 1024
GRID_W = 64
S5_W = 256
S5_GC = 16
S5_G = 16
S5_P = 64
S5_STATE = S5_G * S5_P
MLA_H = 8
MLA_NOPE = 64
MLA_ROPE = 32
MLA_V = 64
MLA_QR = 256
MLA_KVR = 128
MLA_W = MLA_H * MLA_V
MLA_SCALE = (MLA_NOPE + MLA_ROPE) ** -0.5
ROPE_AXIS = MLA_ROPE // 2
ROPE_BASE = 10000.0
HEAD_PAD = 128
ML_H = 4
ML_D = 64
ML_W = 256
ML_AUG = ML_W + 128
N_EXPERTS = 64
TOP_K = 8
N_EXPERT_GROUPS = 8
TOP_GROUPS = 4
EPG = 8
EXPERT_F = 256
ROUTED_SCALE = 2.5
DEPTH = 2
DEEPNORM_ALPHA = (2 * DEPTH) ** 0.25
LN_EPS = 1e-5
SUBLANES = 8

Z_U, Z_QC, Z_MLQ, Z_MLK, Z_MLV, Z_MLO, Z_KVC, Z_MISC = 0, 256, 512, 768, 1024, 1280, 1536, 1664
Z_W = 1792
MISC_GATES = 0
MISC_ROPE = 64

VMEM_LIMIT = 48 * 1024 * 1024
ROW_TILE = 512
S5_STEPS = 64
ML_CHUNK = 256
ATT_TQ = 2048
ATT_TK = 768
ATT_QSPLIT = 4
MOE_BLOCK = 512
MOE_TILE = 256
ROW_TILE_SHAPE = (SUBLANES, D_MODEL // SUBLANES)


def _cparams(sem):
    return pltpu.CompilerParams(dimension_semantics=sem, vmem_limit_bytes=VMEM_LIMIT)


def _ln(x):
    mu = jnp.mean(x, -1, keepdims=True)
    xc = x - mu
    var = jnp.mean(xc * xc, -1, keepdims=True)
    return xc * lax.rsqrt(var + LN_EPS)


def _store_tile_rows(ref, x):
    for s in range(SUBLANES):
        ref[:, s, :] = x[:, s * 128:(s + 1) * 128]


def _load_tile_rows(ref):
    return jnp.concatenate([ref[:, s, :] for s in range(SUBLANES)], -1)


def _modulate(h, sc, sh):
    tm, d = h.shape
    h3 = h.reshape(tm // SUBLANES, SUBLANES, d)
    return (h3 * (1.0 + sc)[None] + sh[None]).reshape(tm, d)


def _per_batch(v, g):
    tm, d = v.shape
    return (v.reshape(tm // SUBLANES, SUBLANES, d) * g[None]).reshape(tm, d)


def _ada_kernel(c_ref, w_ref, b_ref, o_ref):
    c = c_ref[...]
    s = c * jax.nn.sigmoid(c)
    o_ref[...] = jnp.dot(s.astype(BF16), w_ref[...].astype(BF16), preferred_element_type=F32) + b_ref[...]


def _ada_mod(cc, w, b):
    n = w.shape[1]
    tn = 1536
    return pl.pallas_call(
        _ada_kernel,
        grid=(n // tn,),
        in_specs=[pl.BlockSpec((16, D_MODEL), lambda j: (0, 0)),
                  pl.BlockSpec((D_MODEL, tn), lambda j: (0, j)),
                  pl.BlockSpec((1, tn), lambda j: (0, j))],
        out_specs=pl.BlockSpec((16, tn), lambda j: (0, j)),
        out_shape=jax.ShapeDtypeStruct((16, n), F32),
        compiler_params=_cparams(("arbitrary",)),
        name="ada_mod",
    )(cc, w, b.reshape(1, n))


def _inproj_kernel(x_ref, sc_ref, sh_ref, w_ref, z_ref):
    h = _modulate(_ln(x_ref[...]), sc_ref[0], sh_ref[0])
    z_ref[...] = jnp.dot(h.astype(BF16), w_ref[...], preferred_element_type=F32)


def _inproj(x, mod, w_pad, n_lat_tiles):
    n = x.shape[0]
    tm = ROW_TILE
    sel = lambda i: (i >= n_lat_tiles).astype(jnp.int32)
    return pl.pallas_call(
        _inproj_kernel,
        grid=(n // tm,),
        in_specs=[pl.BlockSpec((tm, D_MODEL), lambda i: (i, 0)),
                  pl.BlockSpec((1, SUBLANES, D_MODEL), lambda i: (sel(i), 0, 1)),
                  pl.BlockSpec((1, SUBLANES, D_MODEL), lambda i: (sel(i), 0, 0)),
                  pl.BlockSpec((D_MODEL, Z_W), lambda i: (0, 0))],
        out_specs=pl.BlockSpec((tm, Z_W), lambda i: (i, 0)),
        out_shape=jax.ShapeDtypeStruct((n, Z_W), F32),
        compiler_params=_cparams(("parallel",)),
        name="inproj",
    )(x, mod, mod, w_pad)


def _rope_block(x, tab_ref, base):
    tm = x.shape[0]

    def tab(i):
        t = tab_ref[base + i]
        return jnp.broadcast_to(t[:, None, :], (tm // SUBLANES, SUBLANES, HEAD_PAD)).reshape(tm, HEAD_PAD)

    return (x * tab(0) + pltpu.roll(x, HEAD_PAD - ROPE_AXIS // 2, 1) * tab(1)
            + pltpu.roll(x, ROPE_AXIS // 2, 1) * tab(2))


def _mla_prep_kernel(qc_ref, kvc_ref, misc_ref, tab_ref, qg_ref, wq_ref, kg_ref, wk_ref, wv_ref,
                     q_ref, k_ref, v_ref):
    qc = qc_ref[...]
    qn = qc * lax.rsqrt(jnp.mean(qc * qc, -1, keepdims=True) + 1e-6) * qg_ref[...]
    q = jnp.dot(qn.astype(BF16), wq_ref[...], preferred_element_type=F32)
    kvc = kvc_ref[...]
    kvn = (kvc * lax.rsqrt(jnp.mean(kvc * kvc, -1, keepdims=True) + 1e-6) * kg_ref[...]).astype(BF16)
    k = jnp.dot(kvn, wk_ref[...], preferred_element_type=F32)
    lane = lax.broadcasted_iota(jnp.int32, (1, MLA_H * HEAD_PAD), 1)
    ones_cols = jnp.where(lane % HEAD_PAD >= MLA_V, 1.0, 0.0)
    v_ref[...] = (jnp.dot(kvn, wv_ref[...], preferred_element_type=F32) + ones_cols).astype(BF16)
    kr = _rope_block(misc_ref[...], tab_ref, 3)
    for h in range(MLA_H):
        sl = slice(h * HEAD_PAD, (h + 1) * HEAD_PAD)
        q_ref[:, sl] = _rope_block(q[:, sl], tab_ref, 0).astype(BF16)
        k_ref[:, sl] = (k[:, sl] + kr).astype(BF16)


def _mla_prep(z, tabs, qg, wq_pad, kg, wk_pad, wv):
    n = z.shape[0]
    tm = ROW_TILE
    tt = tm // SUBLANES
    const = lambda i: (0, 0)
    return pl.pallas_call(
        _mla_prep_kernel,
        grid=(n // tm,),
        in_specs=[pl.BlockSpec((tm, MLA_QR), lambda i: (i, Z_QC // MLA_QR)),
                  pl.BlockSpec((tm, MLA_KVR), lambda i: (i, Z_KVC // MLA_KVR)),
                  pl.BlockSpec((tm, 128), lambda i: (i, Z_MISC // 128)),
                  pl.BlockSpec((6, tt, HEAD_PAD), lambda i: (0, i, 0)),
                  pl.BlockSpec((1, MLA_QR), const),
                  pl.BlockSpec((MLA_QR, MLA_H * HEAD_PAD), const),
                  pl.BlockSpec((1, MLA_KVR), const),
                  pl.BlockSpec((MLA_KVR, MLA_H * HEAD_PAD), const),
                  pl.BlockSpec((MLA_KVR, MLA_H * HEAD_PAD), const)],
        out_specs=[pl.BlockSpec((tm, MLA_H * HEAD_PAD), lambda i: (i, 0)),
                   pl.BlockSpec((tm, MLA_H * HEAD_PAD), lambda i: (i, 0)),
                   pl.BlockSpec((tm, MLA_H * HEAD_PAD), lambda i: (i, 0))],
        out_shape=[jax.ShapeDtypeStruct((n, MLA_H * HEAD_PAD), BF16),
                   jax.ShapeDtypeStruct((n, MLA_H * HEAD_PAD), BF16),
                   jax.ShapeDtypeStruct((n, MLA_H * HEAD_PAD), BF16)],
        compiler_params=_cparams(("parallel",)),
        name="mla_prep",
    )(z, z, z, tabs, qg, wq_pad, kg, wk_pad, wv)


def _flash_kernel(qt_ref, k_ref, vt_ref, o_ref, *, n_main, tk, tail, q_split):
    tq = qt_ref.shape[1]
    hq = tq // q_split
    chains = [(h, r) for h in range(2) for r in range(q_split)]
    nc = len(chains)

    def scores(start, size, c):
        h, r = chains[c]
        hs = slice(h * HEAD_PAD, (h + 1) * HEAD_PAD)
        return jnp.dot(k_ref[pl.ds(start, size), hs], qt_ref[hs, r * hq:(r + 1) * hq],
                       preferred_element_type=F32)

    def colmax(s):
        rows = s.shape[0]
        while rows % 16 == 0 and rows > 8:
            rows //= 2
            s = jnp.maximum(s[:rows], s[rows:])
        return jnp.max(s, 0, keepdims=True)

    def update(s, start, size, c, m, acc):
        h, _ = chains[c]
        hs = slice(h * HEAD_PAD, (h + 1) * HEAD_PAD)
        m_new = jnp.maximum(m, colmax(s))
        p = jnp.exp2(s - m_new).astype(BF16)
        acc = jnp.exp2(m - m_new) * acc + jnp.dot(vt_ref[hs, pl.ds(start, size)], p,
                                                  preferred_element_type=F32)
        return m_new, acc

    state = [(jnp.full((1, hq), -jnp.inf, F32), jnp.zeros((HEAD_PAD, hq), F32)) for _ in chains]
    if n_main:
        def body(j, carry):
            s_cur, st = carry[0], list(carry[1:])
            start = pl.multiple_of(j * tk, tk)
            nxt = pl.multiple_of(jnp.minimum(j + 1, n_main - 1) * tk, tk)
            for c in range(nc):
                s_next = scores(start, tk, c + 1) if c + 1 < nc else scores(nxt, tk, 0)
                st[c] = update(s_cur, start, tk, c, *st[c])
                s_cur = s_next
            return (s_cur, *st)

        carry = lax.fori_loop(0, n_main, body, (scores(0, tk, 0), *state))
        state = list(carry[1:])
    if tail:
        for c in range(nc):
            state[c] = update(scores(n_main * tk, tail, c), n_main * tk, tail, c, *state[c])
    for (h, r), (_, acc) in zip(chains, state):
        o_ref[h * MLA_V:(h + 1) * MLA_V, r * hq:(r + 1) * hq] = (
            acc[0:MLA_V] / acc[MLA_V:MLA_V + 1]).astype(BF16)


def _flash(qt, k2, vt, *, n_q_tiles, tq, q_tile0, n_keys, key_block):
    n_main, tail = divmod(n_keys, ATT_TK)
    pairs = MLA_H // 2
    q_split = min(ATT_QSPLIT, max(1, tq // 256))
    kern = functools.partial(_flash_kernel, n_main=n_main, tk=ATT_TK, tail=tail, q_split=q_split)
    return pl.pallas_call(
        kern,
        grid=(SUBLANES, pairs, n_q_tiles),
        in_specs=[pl.BlockSpec((2 * HEAD_PAD, tq), lambda b, p, i: (b * pairs + p, q_tile0 + i)),
                  pl.BlockSpec((n_keys, 2 * HEAD_PAD), lambda b, p, i: (key_block, b * pairs + p)),
                  pl.BlockSpec((2 * HEAD_PAD, n_keys), lambda b, p, i: (b * pairs + p, key_block))],
        out_specs=pl.BlockSpec((2 * MLA_V, tq), lambda b, p, i: (b * pairs + p, i)),
        out_shape=jax.ShapeDtypeStruct((SUBLANES * MLA_W, n_q_tiles * tq), BF16),
        compiler_params=_cparams(("parallel", "parallel", "arbitrary")),
        name="mla_flash",
    )(qt, k2, vt)


def _s5_kernel(u_ref, bb_ref, cc_ref, a_ref, y_ref, st_ref, bu_ref, *, steps):
    d = pl.program_id(0)

    @pl.when(pl.program_id(1) == 0)
    def _():
        st_ref[...] = jnp.zeros_like(st_ref)

    bu_ref[...] = jnp.dot(u_ref[...].astype(BF16), bb_ref[0], preferred_element_type=F32)
    ar = jnp.broadcast_to(a_ref[0, 0:1, :], (SUBLANES, S5_STATE))
    ai = jnp.broadcast_to(a_ref[0, 1:2, :], (SUBLANES, S5_STATE))

    def body(i, carry):
        re, im = carry
        t = jnp.where(d == 0, i, steps - 1 - i)
        row = pl.multiple_of(t * SUBLANES, SUBLANES)
        bur = bu_ref[pl.ds(row, SUBLANES), 0:S5_STATE]
        bui = bu_ref[pl.ds(row, SUBLANES), S5_STATE:2 * S5_STATE]
        nre = ar * re - ai * im + bur
        nim = ar * im + ai * re + bui
        bu_ref[pl.ds(row, SUBLANES), 0:S5_STATE] = nre
        bu_ref[pl.ds(row, SUBLANES), S5_STATE:2 * S5_STATE] = nim
        return nre, nim

    re, im = lax.fori_loop(0, steps, body, (st_ref[0], st_ref[1]), unroll=4)
    st_ref[0] = re
    st_ref[1] = im
    y_ref[0] = jnp.dot(bu_ref[...].astype(BF16), cc_ref[0], preferred_element_type=F32)


def _s5_scan(z, bblk, cblk, a2, n_lat_steps, n_steps):
    n = z.shape[0]
    rows = S5_STEPS * SUBLANES
    n_chunks = n_steps // S5_STEPS
    n_lat_chunks = n_lat_steps // S5_STEPS

    def blk(d, s):
        return jnp.where(d == 0, (s + n_lat_chunks) % n_chunks, n_chunks - 1 - s)

    return pl.pallas_call(
        functools.partial(_s5_kernel, steps=S5_STEPS),
        grid=(2, n_chunks),
        in_specs=[pl.BlockSpec((rows, S5_W), lambda d, s: (blk(d, s), Z_U // S5_W)),
                  pl.BlockSpec((1, S5_W, 2 * S5_STATE), lambda d, s: (d, 0, 0)),
                  pl.BlockSpec((1, 2 * S5_STATE, S5_W), lambda d, s: (d, 0, 0)),
                  pl.BlockSpec((1, 2, S5_STATE), lambda d, s: (d, 0, 0))],
        out_specs=pl.BlockSpec((1, rows, S5_W), lambda d, s: (d, blk(d, s), 0)),
        out_shape=jax.ShapeDtypeStruct((2, n, S5_W), F32),
        scratch_shapes=[pltpu.VMEM((2, SUBLANES, S5_STATE), F32),
                        pltpu.VMEM((rows, 2 * S5_STATE), F32)],
        compiler_params=_cparams(("arbitrary", "arbitrary")),
        name="s5_scan",
    )(z, bblk, cblk, a2)


def _shift_rows(x, sh, rev):
    z = jnp.zeros((sh, x.shape[1]), x.dtype)
    if rev:
        return jnp.concatenate([x[sh:], z], 0)
    return jnp.concatenate([z, x[:-sh]], 0)


def _mlstm_kernel(q_ref, k_ref, v_ref, g_ref, qp_ref, qn_ref, kp_ref, kn_ref, cw_ref, cb_ref, gb_ref,
                  h_ref, cst_ref, mst_ref, qk_ref, qs_ref, ks_ref, vs_ref, hs_ref, bc_ref, *, rev, n_lat_chunks, n_chunks):
    T = ML_CHUNK
    rows = T * SUBLANES
    s = pl.program_id(0)
    c = (n_chunks - 1 - s) if rev else (s + n_lat_chunks) % n_chunks

    @pl.when(s == 0)
    def _():
        cst_ref[...] = jnp.zeros_like(cst_ref)
        mst_ref[...] = jnp.zeros_like(mst_ref)

    first = jnp.logical_or(c == 0, c == n_lat_chunks)
    last = jnp.logical_or(c == n_lat_chunks - 1, c == n_chunks - 1)
    keep_p = jnp.where(first, 0.0, 1.0)
    keep_n = jnp.where(last, 0.0, 1.0)
    qk_ref[0:SUBLANES, 0:ML_W] = qp_ref[...] * keep_p
    qk_ref[0:SUBLANES, ML_W:2 * ML_W] = kp_ref[...] * keep_p
    qk_ref[SUBLANES:SUBLANES + rows, 0:ML_W] = q_ref[...]
    qk_ref[SUBLANES:SUBLANES + rows, ML_W:2 * ML_W] = k_ref[...]
    qk_ref[SUBLANES + rows:2 * SUBLANES + rows, 0:ML_W] = qn_ref[...] * keep_n
    qk_ref[SUBLANES + rows:2 * SUBLANES + rows, ML_W:2 * ML_W] = kn_ref[...] * keep_n
    cw = cw_ref[...]
    conv = (cb_ref[...] + qk_ref[0:rows, :] * cw[0:1] + qk_ref[SUBLANES:SUBLANES + rows, :] * cw[1:2]
            + qk_ref[2 * SUBLANES:2 * SUBLANES + rows, :] * cw[2:3])
    conv = conv * jax.nn.sigmoid(conv)
    for j in range(2):
        qs_ref[j] = conv[:, j * 128:(j + 1) * 128]
        ks_ref[j] = conv[:, ML_W + j * 128:ML_W + (j + 1) * 128] * (ML_D ** -0.5)
        vs_ref[j] = v_ref[:, j * 128:(j + 1) * 128]

    g = g_ref[...] + gb_ref[...]
    lane = lax.broadcasted_iota(jnp.int32, g.shape, 1)
    is_f = jnp.logical_and(lane >= 2 * ML_H, lane < 4 * ML_H)
    gm = jnp.where(is_f, jax.nn.log_sigmoid(g), g)
    cum = jnp.where(is_f, gm, 0.0)
    sh = SUBLANES
    while sh < rows:
        cum = cum + _shift_rows(cum, sh, rev)
        sh *= 2
    bc_ref[0] = gm
    bc_ref[1] = cum

    i_lane0 = ML_H if rev else 0
    f_lane0 = 3 * ML_H if rev else 2 * ML_H
    end_row = 0 if rev else T - 1
    ti = lax.broadcasted_iota(jnp.int32, (T, T), 0)
    si = lax.broadcasted_iota(jnp.int32, (T, T), 1)
    order = (si >= ti) if rev else (si <= ti)
    lane_w = lax.broadcasted_iota(jnp.int32, (1, ML_W), 1) // ML_D
    rowhead = lax.broadcasted_iota(jnp.int32, (ML_W, 1), 0) // ML_D
    colhead = lax.broadcasted_iota(jnp.int32, (1, ML_AUG), 1)
    colhead = jnp.where(colhead < ML_W, colhead // ML_D, colhead - ML_W)
    blockmask = rowhead == colhead
    ones_aug = jnp.ones((T, ML_AUG - ML_W), F32)

    def per_batch(b, _):
        sl = pl.ds(b, T, stride=SUBLANES)
        qb = jnp.concatenate([qs_ref[0, sl, :], qs_ref[1, sl, :]], -1)
        kb = jnp.concatenate([ks_ref[0, sl, :], ks_ref[1, sl, :]], -1)
        vb = jnp.concatenate([vs_ref[0, sl, :], vs_ref[1, sl, :]], -1)
        gcol = bc_ref[0, sl, :]
        ccol = bc_ref[1, sl, :]
        grow = gcol.T
        crow = ccol.T
        cst = cst_ref[b]
        qc = jnp.dot(qb.astype(BF16), cst.astype(BF16), preferred_element_type=F32)
        vaug = jnp.concatenate([vb, ones_aug], -1).astype(BF16)
        kb16 = kb.astype(BF16)
        num = jnp.zeros((T, ML_W), F32)
        den = jnp.zeros((T, ML_W), F32)
        stab = jnp.zeros((T, ML_W), F32)
        ke_scale = jnp.zeros((T, ML_W), F32)
        a_col = jnp.zeros((ML_W, 1), F32)
        s_col = jnp.zeros((ML_W, 1), F32)
        for h in range(ML_H):
            hm = lane_w == h
            il, fl = i_lane0 + h, f_lane0 + h
            b_col = ccol[:, fl:fl + 1]
            i_col = gcol[:, il:il + 1]
            b_row = crow[fl:fl + 1, :]
            i_row = grow[il:il + 1, :]
            b_tot = ccol[end_row:end_row + 1, fl:fl + 1]
            m_in = mst_ref[b, h][0:1, 0:1]
            ld = jnp.where(order, b_col - b_row + i_row, -jnp.inf)
            m_t = jnp.maximum(b_col + m_in, jnp.max(ld, -1, keepdims=True))
            dw = jnp.exp(ld - m_t)
            w_inter = jnp.exp(b_col + m_in - m_t)
            qh = jnp.where(hm, qb, 0.0).astype(BF16)
            sc = lax.dot_general(qh, kb16, (((1,), (1,)), ((), ())), preferred_element_type=F32) * dw
            pv = jnp.dot(sc.astype(BF16), vaug[:, 0:ML_W], preferred_element_type=F32)
            den_h = jnp.sum(sc, -1, keepdims=True) + w_inter * qc[:, ML_W + h:ML_W + h + 1]
            num = num + jnp.where(hm, pv + w_inter * qc[:, 0:ML_W], 0.0)
            den = den + jnp.where(hm, den_h, 0.0)
            stab = stab + jnp.where(hm, jnp.exp(-m_t), 0.0)
            w_end = b_tot - b_col + i_col
            m_loc = jnp.max(w_end, 0, keepdims=True)
            m_new = jnp.maximum(b_tot + m_in, m_loc)
            ke_scale = ke_scale + jnp.where(hm, jnp.exp(w_end - m_loc), 0.0)
            a_col = a_col + jnp.where(rowhead == h, jnp.exp(b_tot + m_in - m_new), 0.0)
            s_col = s_col + jnp.where(rowhead == h, jnp.exp(m_loc - m_new), 0.0)
            mst_ref[b, h] = jnp.broadcast_to(m_new, (SUBLANES, 128))
        hout = num / jnp.maximum(jnp.abs(den), stab)
        hs_ref[0, sl, :] = hout[:, 0:128]
        hs_ref[1, sl, :] = hout[:, 128:256]
        ke = (kb * ke_scale).astype(BF16)
        upd = lax.dot_general(ke, vaug, (((0,), (0,)), ((), ())), preferred_element_type=F32)
        cst_ref[b] = a_col * cst + jnp.where(blockmask, s_col * upd, 0.0)
        return 0

    lax.fori_loop(0, SUBLANES, per_batch, 0)
    h_ref[...] = jnp.concatenate([hs_ref[0], hs_ref[1]], -1)


def _mlstm(z, conv_w, conv_b, gate_b_pad, *, rev, n_lat_steps, n_steps):
    n = z.shape[0]
    T = ML_CHUNK
    rows = T * SUBLANES
    n_chunks = n_steps // T
    n_lat_chunks = n_lat_steps // T
    hb = rows // SUBLANES
    n_hblk = n // SUBLANES

    def blk(s):
        return (n_chunks - 1 - s) if rev else (s + n_lat_chunks) % n_chunks

    prev = lambda s: jnp.maximum(blk(s) * hb - 1, 0)
    nxt = lambda s: jnp.minimum((blk(s) + 1) * hb, n_hblk - 1)
    const = lambda s: (0, 0)
    kern = functools.partial(_mlstm_kernel, rev=rev, n_lat_chunks=n_lat_chunks, n_chunks=n_chunks)
    return pl.pallas_call(
        kern,
        grid=(n_chunks,),
        in_specs=[pl.BlockSpec((rows, ML_W), lambda s: (blk(s), Z_MLQ // ML_W)),
                  pl.BlockSpec((rows, ML_W), lambda s: (blk(s), Z_MLK // ML_W)),
                  pl.BlockSpec((rows, ML_W), lambda s: (blk(s), Z_MLV // ML_W)),
                  pl.BlockSpec((rows, 128), lambda s: (blk(s), Z_MISC // 128)),
                  pl.BlockSpec((SUBLANES, ML_W), lambda s: (prev(s), Z_MLQ // ML_W)),
                  pl.BlockSpec((SUBLANES, ML_W), lambda s: (nxt(s), Z_MLQ // ML_W)),
                  pl.BlockSpec((SUBLANES, ML_W), lambda s: (prev(s), Z_MLK // ML_W)),
                  pl.BlockSpec((SUBLANES, ML_W), lambda s: (nxt(s), Z_MLK // ML_W)),
                  pl.BlockSpec((3, 2 * ML_W), const),
                  pl.BlockSpec((1, 2 * ML_W), const),
                  pl.BlockSpec((1, 128), const)],
        out_specs=pl.BlockSpec((rows, ML_W), lambda s: (blk(s), 0)),
        out_shape=jax.ShapeDtypeStruct((n, ML_W), F32),
        scratch_shapes=[pltpu.VMEM((SUBLANES, ML_W, ML_AUG), F32),
                        pltpu.VMEM((SUBLANES, ML_H, SUBLANES, 128), F32),
                        pltpu.VMEM((rows + 2 * SUBLANES, 2 * ML_W), F32),
                        pltpu.VMEM((2, rows, 128), F32),
                        pltpu.VMEM((2, rows, 128), F32),
                        pltpu.VMEM((2, rows, 128), F32),
                        pltpu.VMEM((2, rows, 128), F32),
                        pltpu.VMEM((2, rows, 128), F32)],
        compiler_params=_cparams(("arbitrary",)),
        name="mlstm_rev" if rev else "mlstm_fwd",
    )(z, z, z, z, z, z, z, z, conv_w, conv_b, gate_b_pad)


def _post_kernel(x_ref, g1_ref, sc_ref, sh_ref, u_ref, y_ref, o_ref, attn_ref, hf_ref, hb_ref,
                 d_ref, gw_ref, gb_ref, hn_ref, pm_ref, wo_ref, l1g_ref, l1b_ref, rw_ref,
                 x1_ref, f_ref, lg_ref):
    y = y_ref[0] + y_ref[1] + u_ref[...] * d_ref[...]
    g = jax.nn.gelu(y)
    s5 = g * jax.nn.sigmoid(jnp.dot(g.astype(BF16), gw_ref[...], preferred_element_type=F32) + gb_ref[...])
    hh = jax.nn.sigmoid(o_ref[...]) * (hf_ref[...] + hb_ref[...])
    pm = pm_ref[...]
    mu = jnp.dot(hh, pm, preferred_element_type=F32, precision=lax.Precision.HIGHEST)
    hc = hh - mu
    var = jnp.dot(hc * hc, pm, preferred_element_type=F32, precision=lax.Precision.HIGHEST)
    ml = hc * lax.rsqrt(var + LN_EPS) * hn_ref[...]
    mix = jnp.concatenate([s5.astype(BF16), attn_ref[...], ml.astype(BF16)], -1)
    yo = jnp.dot(mix, wo_ref[...], preferred_element_type=F32)
    x1 = _ln(DEEPNORM_ALPHA * x_ref[...] + _per_batch(yo, g1_ref[0])) * l1g_ref[...] + l1b_ref[...]
    x1_ref[...] = x1
    f = _modulate(_ln(x1), sc_ref[0], sh_ref[0])
    _store_tile_rows(f_ref, f)
    lg_ref[...] = lax.dot_general(rw_ref[...], f, (((1,), (1,)), ((), ())), preferred_element_type=F32,
                                  precision=lax.Precision.HIGHEST)


def _post(x, mod, z, y2, attn, hf, hb, p, n_lat_tiles):
    n = x.shape[0]
    tm = ROW_TILE
    sel = lambda i: (i >= n_lat_tiles).astype(jnp.int32)
    const = lambda i: (0, 0)
    row = lambda w: pl.BlockSpec((tm, w), lambda i: (i, 0))
    return pl.pallas_call(
        _post_kernel,
        grid=(n // tm,),
        in_specs=[row(D_MODEL),
                  pl.BlockSpec((1, SUBLANES, D_MODEL), lambda i: (sel(i), 0, 2)),
                  pl.BlockSpec((1, SUBLANES, D_MODEL), lambda i: (sel(i), 0, 4)),
                  pl.BlockSpec((1, SUBLANES, D_MODEL), lambda i: (sel(i), 0, 3)),
                  pl.BlockSpec((tm, S5_W), lambda i: (i, Z_U // S5_W)),
                  pl.BlockSpec((2, tm, S5_W), lambda i: (0, i, 0)),
                  pl.BlockSpec((tm, ML_W), lambda i: (i, Z_MLO // ML_W)),
                  row(MLA_W), row(ML_W), row(ML_W),
                  pl.BlockSpec((1, S5_W), const),
                  pl.BlockSpec((S5_W, S5_W), const),
                  pl.BlockSpec((1, S5_W), const),
                  pl.BlockSpec((1, ML_W), const),
                  pl.BlockSpec((ML_W, ML_W), const),
                  pl.BlockSpec((D_MODEL, D_MODEL), const),
                  pl.BlockSpec((1, D_MODEL), const),
                  pl.BlockSpec((1, D_MODEL), const),
                  pl.BlockSpec((N_EXPERTS, D_MODEL), const)],
        out_specs=[row(D_MODEL), pl.BlockSpec((tm,) + ROW_TILE_SHAPE, lambda i: (i, 0, 0)),
                   pl.BlockSpec((N_EXPERTS, tm), lambda i: (0, i))],
        out_shape=[jax.ShapeDtypeStruct((n, D_MODEL), F32),
                   jax.ShapeDtypeStruct((n,) + ROW_TILE_SHAPE, F32),
                   jax.ShapeDtypeStruct((N_EXPERTS, n), F32)],
        compiler_params=_cparams(("parallel",)),
        name="post_mix",
    )(x, mod, mod, mod, z, y2, z, attn, hf, hb,
      p["s5_d"], p["glu_w"], p["glu_b"], p["ml_norm_g"], p["head_mean"], p["w_out"], p["ln1_g"], p["ln1_b"],
      p["router_w"])


def _expert_kernel(be_ref, x_ref, wg_ref, wu_ref, wd_ref, o_ref):
    x = _load_tile_rows(x_ref).astype(BF16)
    a = jnp.dot(x, wg_ref[0], preferred_element_type=F32)
    u = jnp.dot(x, wu_ref[0], preferred_element_type=F32)
    hmid = (a * jax.nn.sigmoid(a) * u).astype(BF16)
    _store_tile_rows(o_ref, jnp.dot(hmid, wd_ref[0], preferred_element_type=F32))


def _experts(blk_expert, xs, wg, wu, wd):
    n_rows = xs.shape[0]
    tb = MOE_BLOCK
    rows = pl.BlockSpec((tb,) + ROW_TILE_SHAPE, lambda i, be: (i, 0, 0))
    grid_spec = pltpu.PrefetchScalarGridSpec(
        num_scalar_prefetch=1,
        grid=(n_rows // tb,),
        in_specs=[rows,
                  pl.BlockSpec((1, D_MODEL, EXPERT_F), lambda i, be: (be[i], 0, 0)),
                  pl.BlockSpec((1, D_MODEL, EXPERT_F), lambda i, be: (be[i], 0, 0)),
                  pl.BlockSpec((1, EXPERT_F, D_MODEL), lambda i, be: (be[i], 0, 0))],
        out_specs=rows,
    )
    return pl.pallas_call(
        _expert_kernel,
        grid_spec=grid_spec,
        out_shape=jax.ShapeDtypeStruct((n_rows,) + ROW_TILE_SHAPE, F32),
        compiler_params=_cparams(("arbitrary",)),
        name="moe_experts",
    )(blk_expert, xs, wg, wu, wd)


def _idx_copy(idx_hbm, idx_smem, sem, tile, slot, width):
    return pltpu.make_async_copy(idx_hbm.at[tile], idx_smem.at[pl.ds(pl.multiple_of(slot * width, width), width)],
                                 sem.at[slot])


def _row_groups(tm, base, fn):
    def trip(g, c):
        t0 = pl.multiple_of(g * SUBLANES, SUBLANES)
        for j in range(SUBLANES):
            for k in range(TOP_K):
                fn(k, t0 + j, base + (k * tm + j) + t0)
        return c

    lax.fori_loop(0, tm // SUBLANES, trip, 0)


def _dispatch_kernel(pe_ref, pc_ref, f_ref, idx_hbm, xs_hbm, idx_smem, zero_ref, isem, zsem, rsem, *, tm):
    i = pl.program_id(0)
    n = pl.num_programs(0)
    slot = i % 2
    tb = MOE_BLOCK
    width = TOP_K * tm

    def row_copy(t, r):
        return pltpu.make_async_copy(f_ref.at[t], xs_hbm.at[r], rsem)

    def pad_copy(e):
        start = pl.multiple_of(pe_ref[e] - tb, tb)
        return pltpu.make_async_copy(zero_ref, xs_hbm.at[pl.ds(start, tb)], zsem)

    @pl.when(i == 0)
    def _():
        _idx_copy(idx_hbm, idx_smem, isem, 0, 0, width).start()
        zero_ref[...] = jnp.zeros_like(zero_ref)

        def fill(e, c):
            @pl.when(pc_ref[e] > 0)
            def _():
                pad_copy(e).start()
            return c

        def drain(e, c):
            @pl.when(pc_ref[e] > 0)
            def _():
                pad_copy(e).wait()
            return c

        lax.fori_loop(0, N_EXPERTS, fill, 0)
        lax.fori_loop(0, N_EXPERTS, drain, 0)

    _idx_copy(idx_hbm, idx_smem, isem, i, slot, width).wait()

    @pl.when(i + 1 < n)
    def _():
        _idx_copy(idx_hbm, idx_smem, isem, i + 1, 1 - slot, width).start()

    base = slot * width
    _row_groups(tm, base, lambda k, t, a: row_copy(t, idx_smem[a]).start(priority=k % 2))
    _row_groups(tm, base, lambda k, t, a: row_copy(t, idx_smem[a]).wait())


def _dispatch_rows(f3, idx_tiles, pad_end, padded, n_rows):
    n = f3.shape[0]
    tm = MOE_TILE
    grid_spec = pltpu.PrefetchScalarGridSpec(
        num_scalar_prefetch=2,
        grid=(n // tm,),
        in_specs=[pl.BlockSpec((tm,) + ROW_TILE_SHAPE, lambda i, pe, pc: (i, 0, 0)),
                  pl.BlockSpec(memory_space=pl.ANY)],
        out_specs=pl.BlockSpec(memory_space=pl.ANY),
        scratch_shapes=[pltpu.SMEM((2 * TOP_K * tm,), jnp.int32),
                        pltpu.VMEM((MOE_BLOCK,) + ROW_TILE_SHAPE, F32),
                        pltpu.SemaphoreType.DMA((2,)),
                        pltpu.SemaphoreType.DMA,
                        pltpu.SemaphoreType.DMA],
    )
    return pl.pallas_call(
        functools.partial(_dispatch_kernel, tm=tm),
        grid_spec=grid_spec,
        out_shape=jax.ShapeDtypeStruct((n_rows,) + ROW_TILE_SHAPE, F32),
        compiler_params=_cparams(("arbitrary",)),
        name="moe_dispatch",
    )(pad_end, padded, f3, idx_tiles)


def _final_kernel(x1_ref, f_ref, gate_ref, g2_ref, wg_ref, wu_ref, wd_ref, lg_ref, lb_ref, idx_hbm, ys_hbm,
                  o_ref, idx_smem, buf_ref, isem, rsem, *, tm):
    i = pl.program_id(0)
    n = pl.num_programs(0)
    width = TOP_K * tm

    def row_copy(slot, k, t, r):
        return pltpu.make_async_copy(ys_hbm.at[r], buf_ref.at[slot, k, t], rsem.at[slot])

    def gather(tile_slot):
        base = tile_slot * width
        _row_groups(tm, base, lambda k, t, a: row_copy(tile_slot, k, t, idx_smem[a]).start(priority=k % 2))

    @pl.when(i == 0)
    def _():
        _idx_copy(idx_hbm, idx_smem, isem, 0, 0, width).start()
        _idx_copy(idx_hbm, idx_smem, isem, 0, 0, width).wait()
        gather(0)

        @pl.when(n > 1)
        def _():
            _idx_copy(idx_hbm, idx_smem, isem, 1, 1, width).start()

    cur = i % 2

    @pl.when(i + 1 < n)
    def _():
        _idx_copy(idx_hbm, idx_smem, isem, i + 1, 1 - cur, width).wait()
        gather(1 - cur)

    @pl.when(i + 2 < n)
    def _():
        _idx_copy(idx_hbm, idx_smem, isem, i + 2, cur, width).start()

    f = _load_tile_rows(f_ref).astype(BF16)
    a = jnp.dot(f, wg_ref[...], preferred_element_type=F32)
    u = jnp.dot(f, wu_ref[...], preferred_element_type=F32)
    ffn = jnp.dot((a * jax.nn.sigmoid(a) * u).astype(BF16), wd_ref[...], preferred_element_type=F32)
    _row_groups(tm, 0, lambda k, t, a: row_copy(cur, k, t, 0).wait())
    gate = gate_ref[...]
    for k in range(TOP_K):
        ffn = ffn + gate[:, k:k + 1] * _load_tile_rows(buf_ref.at[cur, k])
    o_ref[...] = _ln(DEEPNORM_ALPHA * x1_ref[...] + _per_batch(ffn, g2_ref[0])) * lg_ref[...] + lb_ref[...]


def _final(x1, f3, gate_t, idx_tiles, ys, mod, p, n_lat_tiles, n_out_tiles):
    tm = MOE_TILE
    n_lat_tiles = n_lat_tiles * ROW_TILE // tm
    n_out_tiles = n_out_tiles * ROW_TILE // tm
    sel = lambda i: (i >= n_lat_tiles).astype(jnp.int32)
    const = lambda i: (0, 0)
    row = lambda w: pl.BlockSpec((tm, w), lambda i: (i, 0))
    return pl.pallas_call(
        functools.partial(_final_kernel, tm=tm),
        grid=(n_out_tiles,),
        in_specs=[row(D_MODEL), pl.BlockSpec((tm,) + ROW_TILE_SHAPE, lambda i: (i, 0, 0)), row(TOP_K),
                  pl.BlockSpec((1, SUBLANES, D_MODEL), lambda i: (sel(i), 0, 5)),
                  pl.BlockSpec((D_MODEL, EXPERT_F), const),
                  pl.BlockSpec((D_MODEL, EXPERT_F), const),
                  pl.BlockSpec((EXPERT_F, D_MODEL), const),
                  pl.BlockSpec((1, D_MODEL), const),
                  pl.BlockSpec((1, D_MODEL), const),
                  pl.BlockSpec(memory_space=pl.ANY),
                  pl.BlockSpec(memory_space=pl.ANY)],
        out_specs=row(D_MODEL),
        out_shape=jax.ShapeDtypeStruct((n_out_tiles * tm, D_MODEL), F32),
        scratch_shapes=[pltpu.SMEM((2 * TOP_K * tm,), jnp.int32),
                        pltpu.VMEM((2, TOP_K, tm) + ROW_TILE_SHAPE, F32),
                        pltpu.SemaphoreType.DMA((2,)),
                        pltpu.SemaphoreType.DMA((2,))],
        compiler_params=_cparams(("arbitrary",)),
        name="final_ffn",
    )(x1, f3, gate_t, mod, p["sh_w_gate"], p["sh_w_up"], p["sh_w_down"], p["ln2_g"], p["ln2_b"], idx_tiles, ys)


def _route_kernel(lg_ref, bias_ref, tri_ref, e_ref, g_ref, r_ref, cnt_ref, carry_ref):
    @pl.when(pl.program_id(0) == 0)
    def _():
        carry_ref[...] = jnp.zeros_like(carry_ref)

    tm = lg_ref.shape[1]
    neg = -jnp.inf
    s = jax.nn.sigmoid(lg_ref[...])
    g3 = (s + bias_ref[...]).reshape(N_EXPERT_GROUPS, EPG, tm)
    io3 = lax.broadcasted_iota(jnp.int32, (N_EXPERT_GROUPS, EPG, tm), 1)
    m1 = jnp.max(g3, 1, keepdims=True)
    f1 = jnp.min(jnp.where(g3 == m1, io3, EPG), 1, keepdims=True)
    m2 = jnp.max(jnp.where(io3 == f1, neg, g3), 1, keepdims=True)
    gs = m1 + m2
    iog = lax.broadcasted_iota(jnp.int32, (N_EXPERT_GROUPS, 1, tm), 0)
    cur = gs
    kth = gs
    for _ in range(TOP_GROUPS):
        kth = jnp.max(cur, 0, keepdims=True)
        fi = jnp.min(jnp.where(cur == kth, iog, N_EXPERT_GROUPS), 0, keepdims=True)
        cur = jnp.where(iog == fi, neg, cur)
    cand = jnp.where(gs >= kth, g3, neg).reshape(N_EXPERTS, tm)
    io = lax.broadcasted_iota(jnp.int32, (N_EXPERTS, tm), 0)
    memb = jnp.zeros((N_EXPERTS, tm), F32)
    es, gates, hots = [], [], []
    for _ in range(TOP_K):
        mk = jnp.max(cand, 0, keepdims=True)
        ik = jnp.min(jnp.where(cand == mk, io, N_EXPERTS), 0, keepdims=True)
        oh = io == ik
        gates.append(jnp.sum(jnp.where(oh, s, 0.0), 0, keepdims=True))
        es.append(ik)
        hots.append(oh)
        cand = jnp.where(oh, neg, cand)
        memb = memb + jnp.where(oh, 1.0, 0.0)
    gsum = gates[0]
    for gk in gates[1:]:
        gsum = gsum + gk
    g_ref[...] = jnp.concatenate(gates, 0) / gsum * ROUTED_SCALE
    e_ref[...] = jnp.concatenate(es, 0)
    pref = jnp.dot(memb.astype(BF16), tri_ref[...], preferred_element_type=F32) + carry_ref[:, 0:1]
    ranks = [jnp.sum(jnp.where(oh, pref, 0.0), 0, keepdims=True) for oh in hots]
    r_ref[...] = jnp.concatenate(ranks, 0).astype(jnp.int32)
    total = carry_ref[...] + jnp.sum(memb, 1, keepdims=True)
    carry_ref[...] = total
    cnt_ref[...] = total


def _route(logits_t, router_bias):
    n = logits_t.shape[1]
    tm = ROW_TILE
    tri = (jnp.arange(tm)[:, None] < jnp.arange(tm)[None, :]).astype(BF16)
    const = lambda i: (0, 0)
    col = pl.BlockSpec((TOP_K, tm), lambda i: (0, i))
    top_e, gate, rank, cnt = pl.pallas_call(
        _route_kernel,
        grid=(n // tm,),
        in_specs=[pl.BlockSpec((N_EXPERTS, tm), lambda i: (0, i)),
                  pl.BlockSpec((N_EXPERTS, 1), const),
                  pl.BlockSpec((tm, tm), const)],
        out_specs=[col, col, col, pl.BlockSpec((N_EXPERTS, 128), const)],
        out_shape=[jax.ShapeDtypeStruct((TOP_K, n), jnp.int32),
                   jax.ShapeDtypeStruct((TOP_K, n), F32),
                   jax.ShapeDtypeStruct((TOP_K, n), jnp.int32),
                   jax.ShapeDtypeStruct((N_EXPERTS, 128), F32)],
        scratch_shapes=[pltpu.VMEM((N_EXPERTS, 128), F32)],
        compiler_params=_cparams(("arbitrary",)),
        name="moe_route",
    )(logits_t, router_bias.astype(F32).reshape(N_EXPERTS, 1), tri)
    return top_e, gate, rank, cnt[:, 0].astype(jnp.int32)


def _dispatch(top_e, rank, counts):
    n_tok = top_e.shape[1]
    tb = MOE_BLOCK
    padded = (counts + tb - 1) // tb * tb
    pad_end = jnp.cumsum(padded)
    pad_start = pad_end - padded
    hot = top_e[:, :, None] == jnp.arange(N_EXPERTS, dtype=top_e.dtype)
    dest = (jnp.sum(jnp.where(hot, pad_start.astype(jnp.int32), 0), -1) + rank).astype(jnp.int32)
    n_rows = -(-(n_tok * TOP_K + N_EXPERTS * (tb - 1)) // tb) * tb
    n_blocks = n_rows // tb
    blk_start = jnp.arange(n_blocks, dtype=jnp.int32) * tb
    blk_expert = jnp.minimum(jnp.sum((pad_end[None, :] <= blk_start[:, None]).astype(jnp.int32), 1),
                             N_EXPERTS - 1)
    tm = MOE_TILE
    idx_tiles = jnp.transpose(dest.reshape(TOP_K, n_tok // tm, tm), (1, 0, 2)).reshape(n_tok // tm, TOP_K * tm)
    return idx_tiles, blk_expert, pad_end.astype(jnp.int32), padded.astype(jnp.int32), n_rows


def _pad_w_in(w_in):
    s5u, qc, kvc, kr, mq, mk, mv, mo, gates = jnp.split(
        w_in, [256, 512, 640, 672, 928, 1184, 1440, 1696], axis=1)
    misc = jnp.zeros((D_MODEL, 128), w_in.dtype)
    misc = misc.at[:, MISC_GATES:MISC_GATES + 16].set(gates).at[:, MISC_ROPE:MISC_ROPE + MLA_ROPE].set(kr)
    return jnp.concatenate([s5u, qc, mq, mk, mv, mo, kvc, misc], axis=1).astype(BF16)


def _pad_heads(w, width):
    k = w.shape[0]
    w3 = w.reshape(k, MLA_H, width)
    return jnp.pad(w3, ((0, 0), (0, 0), (0, HEAD_PAD - width))).reshape(k, MLA_H * HEAD_PAD)


def _rope_tables(n_lat, n_ctx):
    t = jnp.arange(n_lat)
    half = ROPE_AXIS // 2
    inv_freq = ROPE_BASE ** (-jnp.arange(half, dtype=F32) / half)
    ang_r = (t // GRID_W).astype(F32)[:, None] * inv_freq
    ang_c = (t % GRID_W).astype(F32)[:, None] * inv_freq
    ang = jnp.concatenate([ang_r, ang_r, ang_c, ang_c], -1)
    ang = jnp.concatenate([ang, jnp.zeros((n_ctx, MLA_ROPE), F32)], 0)
    cos, sin = jnp.cos(ang), jnp.sin(ang)
    first = (jnp.arange(MLA_ROPE) % ROPE_AXIS) < half

    def place(v, fill):
        out = jnp.full((v.shape[0], HEAD_PAD), fill, F32)
        return out.at[:, MLA_NOPE:MLA_NOPE + MLA_ROPE].set(v)

    q_scale = MLA_SCALE * math.log2(math.e)
    c_q = place(cos, 1.0) * q_scale
    m_q = place(jnp.where(first, -sin, 0.0), 0.0) * q_scale
    p_q = place(jnp.where(first, 0.0, sin), 0.0) * q_scale
    c_k = place(cos, 0.0)
    m_k = place(jnp.where(first, -sin, 0.0), 0.0)
    p_k = place(jnp.where(first, 0.0, sin), 0.0)
    return jnp.stack([c_q, m_q, p_q, c_k, m_k, p_k], 0)


def _s5_params(lam_re, lam_im, log_step, b_re, b_im, c_re, c_im):
    lr, li = lam_re.astype(F32), lam_im.astype(F32)
    step = jnp.exp(log_step.astype(F32))[..., None]
    mag = jnp.exp(lr * step)
    ar, ai = mag * jnp.cos(li * step), mag * jnp.sin(li * step)
    nr, ni = ar - 1.0, ai
    den = lr * lr + li * li
    cr, ci = ((nr * lr + ni * li) / den)[..., None], ((ni * lr - nr * li) / den)[..., None]
    br, bi = b_re.astype(F32), b_im.astype(F32)
    bbr, bbi = cr * br - ci * bi, cr * bi + ci * br
    eye = jnp.eye(S5_G, dtype=F32)
    b_blk = lambda v: jnp.einsum("dgcp,gh->dgchp", jnp.transpose(v, (0, 1, 3, 2)), eye).reshape(2, S5_W, S5_STATE)
    bblk = jnp.concatenate([b_blk(bbr), b_blk(bbi)], -1)
    c_blk = lambda v: jnp.einsum("dgpc,gh->dgphc", jnp.transpose(v.astype(F32), (0, 1, 3, 2)), eye).reshape(
        2, S5_STATE, S5_W)
    cblk = jnp.concatenate([c_blk(c_re), -c_blk(c_im)], 1)
    a2 = jnp.stack([ar.reshape(2, S5_STATE), ai.reshape(2, S5_STATE)], 1)
    return bblk.astype(BF16), cblk.astype(BF16), a2


def kernel(x, c, ctx, c_ctx, ada_w, ada_b, w_in, s5_lambda_re, s5_lambda_im, s5_log_step, s5_b_re, s5_b_im, s5_c_re, s5_c_im, s5_d, s5_glu_w, s5_glu_b, mla_q_norm, mla_w_q_up, mla_kv_norm, mla_w_kv_up, ml_conv_w, ml_conv_b, ml_gate_b, ml_norm_g, w_out, ln1_g, ln1_b, ln2_g, ln2_b, router_w, router_bias, exp_w_gate, exp_w_up, exp_w_down, sh_w_gate, sh_w_up, sh_w_down):
    bsz, n_lat, d = x.shape
    n_ctx = ctx.shape[1]
    depth = ada_w.shape[0]
    assert bsz == SUBLANES and d == D_MODEL
    assert n_lat % ATT_TQ == 0 and n_lat % ML_CHUNK == 0 and n_ctx % ML_CHUNK == 0
    assert (n_lat * bsz) % ROW_TILE == 0 and (n_ctx * bsz) % ROW_TILE == 0
    n_steps = n_lat + n_ctx
    n_tok = n_steps * bsz
    n_lat_tiles = n_lat * bsz // ROW_TILE

    xs = jnp.concatenate([jnp.transpose(x, (1, 0, 2)), jnp.transpose(ctx, (1, 0, 2))], 0).reshape(n_tok, d)
    cc = jnp.concatenate([c, jnp.broadcast_to(c_ctx[None], (bsz, d))], 0)
    tabs = _rope_tables(n_lat, n_ctx)
    head_mean = jnp.kron(jnp.eye(ML_H, dtype=F32), jnp.full((ML_D, ML_D), 1.0 / ML_D, F32))

    for layer in range(depth):
        last = layer == depth - 1
        mod = _ada_mod(cc, ada_w[layer], ada_b[layer]).reshape(2, bsz, 6 * d)
        z = _inproj(xs, mod, _pad_w_in(w_in[layer]), n_lat_tiles)

        w_kv = mla_w_kv_up[layer].reshape(MLA_KVR, MLA_H, MLA_NOPE + MLA_V)
        wk_pad = _pad_heads(w_kv[:, :, :MLA_NOPE].reshape(MLA_KVR, MLA_H * MLA_NOPE), MLA_NOPE).astype(BF16)
        wv = _pad_heads(w_kv[:, :, MLA_NOPE:].reshape(MLA_KVR, MLA_W), MLA_V).astype(BF16)
        wq_pad = _pad_heads(mla_w_q_up[layer], MLA_NOPE + MLA_ROPE).astype(BF16)
        q, k, v = _mla_prep(z, tabs, mla_q_norm[layer].reshape(1, -1), wq_pad,
                            mla_kv_norm[layer].reshape(1, -1), wk_pad, wv)
        qt = q.reshape(n_steps, bsz * MLA_H * HEAD_PAD).T
        k2 = k.reshape(n_steps, bsz * MLA_H * HEAD_PAD)
        vt = v.reshape(n_steps, bsz * MLA_H * HEAD_PAD).T
        attn = _flash(qt, k2, vt, n_q_tiles=n_lat // ATT_TQ, tq=ATT_TQ, q_tile0=0, n_keys=n_steps, key_block=0)
        if last:
            attn_ctx = jnp.zeros((bsz * MLA_W, n_ctx), BF16)
        else:
            attn_ctx = _flash(qt, k2, vt, n_q_tiles=1, tq=n_ctx, q_tile0=n_lat // n_ctx, n_keys=n_ctx,
                              key_block=n_lat // n_ctx)
        attn = jnp.concatenate([attn, attn_ctx], 1).T.reshape(n_tok, MLA_W)

        bblk, cblk, a2 = _s5_params(s5_lambda_re[layer], s5_lambda_im[layer], s5_log_step[layer],
                                    s5_b_re[layer], s5_b_im[layer], s5_c_re[layer], s5_c_im[layer])
        y2 = _s5_scan(z, bblk, cblk, a2, n_lat, n_steps)

        gate_b_pad = jnp.zeros((1, 128), F32).at[0, :4 * ML_H].set(ml_gate_b[layer].reshape(4 * ML_H))
        cw, cb = ml_conv_w[layer], ml_conv_b[layer].reshape(1, -1)
        hf = _mlstm(z, cw, cb, gate_b_pad, rev=False, n_lat_steps=n_lat, n_steps=n_steps)
        hb = _mlstm(z, cw, cb, gate_b_pad, rev=True, n_lat_steps=n_lat, n_steps=n_steps)

        p = dict(s5_d=s5_d[layer].reshape(1, -1), glu_w=s5_glu_w[layer].astype(BF16),
                 glu_b=s5_glu_b[layer].reshape(1, -1), ml_norm_g=ml_norm_g[layer].reshape(1, -1),
                 head_mean=head_mean, w_out=w_out[layer].astype(BF16), ln1_g=ln1_g[layer].reshape(1, -1),
                 ln1_b=ln1_b[layer].reshape(1, -1), router_w=router_w[layer].T,
                 sh_w_gate=sh_w_gate[layer].astype(BF16), sh_w_up=sh_w_up[layer].astype(BF16),
                 sh_w_down=sh_w_down[layer].astype(BF16), ln2_g=ln2_g[layer].reshape(1, -1),
                 ln2_b=ln2_b[layer].reshape(1, -1))
        x1, f3, logits = _post(xs, mod, z, y2, attn, hf, hb, p, n_lat_tiles)

        top_e, gate, rank, counts = _route(logits, router_bias[layer])
        idx_tiles, blk_expert, pad_end, padded, n_rows = _dispatch(top_e, rank, counts)
        rows_in = _dispatch_rows(f3, idx_tiles, pad_end, padded, n_rows)
        rows_out = _experts(blk_expert, rows_in, exp_w_gate[layer].astype(BF16),
                            exp_w_up[layer].astype(BF16), exp_w_down[layer].astype(BF16))
        n_out_tiles = n_lat_tiles if last else n_tok // ROW_TILE
        xs = _final(x1, f3, gate.T, idx_tiles, rows_out, mod, p, n_lat_tiles, n_out_tiles)

    return jnp.transpose(xs[:n_lat * bsz].reshape(n_lat, bsz, d), (1, 0, 2))
```

```python
import functools
import math

import jax
import jax.numpy as jnp
from jax import lax
from jax.experimental import pallas as pl
from jax.experimental.pallas import tpu as pltpu

F32 = jnp.float32
BF16 = jnp.bfloat16

D_MODEL = 1024
GRID_W = 64
S5_W = 256
S5_GC = 16
S5_G = 16
S5_P = 64
S5_STATE = S5_G * S5_P
MLA_H = 8
MLA_NOPE = 64
MLA_ROPE = 32
MLA_V = 64
MLA_QR = 256
MLA_KVR = 128
MLA_W = MLA_H * MLA_V
MLA_SCALE = (MLA_NOPE + MLA_ROPE) ** -0.5
ROPE_AXIS = MLA_ROPE // 2
ROPE_BASE = 10000.0
HEAD_PAD = 128
ML_H = 4
ML_D = 64
ML_W = 256
ML_AUG = ML_W + 128
N_EXPERTS = 64
TOP_K = 8
N_EXPERT_GROUPS = 8
TOP_GROUPS = 4
EPG = 8
EXPERT_F = 256
ROUTED_SCALE = 2.5
DEPTH = 2
DEEPNORM_ALPHA = (2 * DEPTH) ** 0.25
LN_EPS = 1e-5
SUBLANES = 8

Z_U, Z_QC, Z_MLQ, Z_MLK, Z_MLV, Z_MLO, Z_KVC, Z_MISC = 0, 256, 512, 768, 1024, 1280, 1536, 1664
Z_W = 1792
MISC_GATES = 0
MISC_ROPE = 64

VMEM_LIMIT = 48 * 1024 * 1024
ROW_TILE = 512
S5_STEPS = 64
ML_CHUNK = 256
ATT_TQ = 2048
ATT_TK = 768
ATT_QSPLIT = 4
MOE_BLOCK = 512
MOE_TILE = 256
ROW_TILE_SHAPE = (SUBLANES, D_MODEL // SUBLANES)


def _cparams(sem):
    return pltpu.CompilerParams(dimension_semantics=sem, vmem_limit_bytes=VMEM_LIMIT)


def _ln(x):
    mu = jnp.mean(x, -1, keepdims=True)
    xc = x - mu
    var = jnp.mean(xc * xc, -1, keepdims=True)
    return xc * lax.rsqrt(var + LN_EPS)


def _store_tile_rows(ref, x):
    rows = x.shape[0]
    for s in range(SUBLANES):
        ref[pl.ds(s, rows, stride=SUBLANES), :] = x[:, s * 128:(s + 1) * 128]


def _load_tile_rows(ref):
    rows = ref.shape[0] // SUBLANES
    return jnp.concatenate([ref[pl.ds(s, rows, stride=SUBLANES), :] for s in range(SUBLANES)], -1)


def _modulate(h, sc, sh):
    tm, d = h.shape
    h3 = h.reshape(tm // SUBLANES, SUBLANES, d)
    return (h3 * (1.0 + sc)[None] + sh[None]).reshape(tm, d)


def _per_batch(v, g):
    tm, d = v.shape
    return (v.reshape(tm // SUBLANES, SUBLANES, d) * g[None]).reshape(tm, d)


def _ada_kernel(c_ref, w_ref, b_ref, o_ref):
    c = c_ref[...]
    s = c * jax.nn.sigmoid(c)
    o_ref[...] = jnp.dot(s.astype(BF16), w_ref[...].astype(BF16), preferred_element_type=F32) + b_ref[...]


def _ada_mod(cc, w, b):
    n = w.shape[1]
    tn = 1536
    return pl.pallas_call(
        _ada_kernel,
        grid=(n // tn,),
        in_specs=[pl.BlockSpec((16, D_MODEL), lambda j: (0, 0)),
                  pl.BlockSpec((D_MODEL, tn), lambda j: (0, j)),
                  pl.BlockSpec((1, tn), lambda j: (0, j))],
        out_specs=pl.BlockSpec((16, tn), lambda j: (0, j)),
        out_shape=jax.ShapeDtypeStruct((16, n), F32),
        compiler_params=_cparams(("arbitrary",)),
        name="ada_mod",
    )(cc, w, b.reshape(1, n))


def _inproj_kernel(x_ref, sc_ref, sh_ref, w_ref, z_ref):
    h = _modulate(_ln(x_ref[...]), sc_ref[0], sh_ref[0])
    z_ref[...] = jnp.dot(h.astype(BF16), w_ref[...], preferred_element_type=F32)


def _inproj(x, mod, w_pad, n_lat_tiles):
    n = x.shape[0]
    tm = ROW_TILE
    sel = lambda i: (i >= n_lat_tiles).astype(jnp.int32)
    return pl.pallas_call(
        _inproj_kernel,
        grid=(n // tm,),
        in_specs=[pl.BlockSpec((tm, D_MODEL), lambda i: (i, 0)),
                  pl.BlockSpec((1, SUBLANES, D_MODEL), lambda i: (sel(i), 0, 1)),
                  pl.BlockSpec((1, SUBLANES, D_MODEL), lambda i: (sel(i), 0, 0)),
                  pl.BlockSpec((D_MODEL, Z_W), lambda i: (0, 0))],
        out_specs=pl.BlockSpec((tm, Z_W), lambda i: (i, 0)),
        out_shape=jax.ShapeDtypeStruct((n, Z_W), F32),
        compiler_params=_cparams(("parallel",)),
        name="inproj",
    )(x, mod, mod, w_pad)


def _rope_block(x, tab_ref, base):
    tm = x.shape[0]

    def tab(i):
        t = tab_ref[base + i]
        return jnp.broadcast_to(t[:, None, :], (tm // SUBLANES, SUBLANES, HEAD_PAD)).reshape(tm, HEAD_PAD)

    return (x * tab(0) + pltpu.roll(x, HEAD_PAD - ROPE_AXIS // 2, 1) * tab(1)
            + pltpu.roll(x, ROPE_AXIS // 2, 1) * tab(2))


def _mla_prep_kernel(qc_ref, kvc_ref, misc_ref, tab_ref, qg_ref, wq_ref, kg_ref, wk_ref, wv_ref,
                     q_ref, k_ref, v_ref):
    qc = qc_ref[...]
    qn = qc * lax.rsqrt(jnp.mean(qc * qc, -1, keepdims=True) + 1e-6) * qg_ref[...]
    q = jnp.dot(qn.astype(BF16), wq_ref[...], preferred_element_type=F32)
    kvc = kvc_ref[...]
    kvn = (kvc * lax.rsqrt(jnp.mean(kvc * kvc, -1, keepdims=True) + 1e-6) * kg_ref[...]).astype(BF16)
    k = jnp.dot(kvn, wk_ref[...], preferred_element_type=F32)
    lane = lax.broadcasted_iota(jnp.int32, (1, MLA_H * HEAD_PAD), 1)
    ones_cols = jnp.where(lane % HEAD_PAD >= MLA_V, 1.0, 0.0)
    v_ref[...] = (jnp.dot(kvn, wv_ref[...], preferred_element_type=F32) + ones_cols).astype(BF16)
    kr = _rope_block(misc_ref[...], tab_ref, 3)
    for h in range(MLA_H):
        sl = slice(h * HEAD_PAD, (h + 1) * HEAD_PAD)
        q_ref[:, sl] = _rope_block(q[:, sl], tab_ref, 0).astype(BF16)
        k_ref[:, sl] = (k[:, sl] + kr).astype(BF16)


def _mla_prep(z, tabs, qg, wq_pad, kg, wk_pad, wv):
    n = z.shape[0]
    tm = ROW_TILE
    tt = tm // SUBLANES
    const = lambda i: (0, 0)
    return pl.pallas_call(
        _mla_prep_kernel,
        grid=(n // tm,),
        in_specs=[pl.BlockSpec((tm, MLA_QR), lambda i: (i, Z_QC // MLA_QR)),
                  pl.BlockSpec((tm, MLA_KVR), lambda i: (i, Z_KVC // MLA_KVR)),
                  pl.BlockSpec((tm, 128), lambda i: (i, Z_MISC // 128)),
                  pl.BlockSpec((6, tt, HEAD_PAD), lambda i: (0, i, 0)),
                  pl.BlockSpec((1, MLA_QR), const),
                  pl.BlockSpec((MLA_QR, MLA_H * HEAD_PAD), const),
                  pl.BlockSpec((1, MLA_KVR), const),
                  pl.BlockSpec((MLA_KVR, MLA_H * HEAD_PAD), const),
                  pl.BlockSpec((MLA_KVR, MLA_H * HEAD_PAD), const)],
        out_specs=[pl.BlockSpec((tm, MLA_H * HEAD_PAD), lambda i: (i, 0)),
                   pl.BlockSpec((tm, MLA_H * HEAD_PAD), lambda i: (i, 0)),
                   pl.BlockSpec((tm, MLA_H * HEAD_PAD), lambda i: (i, 0))],
        out_shape=[jax.ShapeDtypeStruct((n, MLA_H * HEAD_PAD), BF16),
                   jax.ShapeDtypeStruct((n, MLA_H * HEAD_PAD), BF16),
                   jax.ShapeDtypeStruct((n, MLA_H * HEAD_PAD), BF16)],
        compiler_params=_cparams(("parallel",)),
        name="mla_prep",
    )(z, z, z, tabs, qg, wq_pad, kg, wk_pad, wv)


def _flash_kernel(qt_ref, k_ref, vt_ref, o_ref, *, n_main, tk, tail, q_split):
    tq = qt_ref.shape[1]
    hq = tq // q_split
    chains = [(h, r) for h in range(2) for r in range(q_split)]
    nc = len(chains)

    def scores(start, size, c):
        h, r = chains[c]
        hs = slice(h * HEAD_PAD, (h + 1) * HEAD_PAD)
        return jnp.dot(k_ref[pl.ds(start, size), hs], qt_ref[hs, r * hq:(r + 1) * hq],
                       preferred_element_type=F32)

    def colmax(s):
        rows = s.shape[0]
        while rows % 16 == 0 and rows > 8:
            rows //= 2
            s = jnp.maximum(s[:rows], s[rows:])
        return jnp.max(s, 0, keepdims=True)

    def update(s, start, size, c, m, acc):
        h, _ = chains[c]
        hs = slice(h * HEAD_PAD, (h + 1) * HEAD_PAD)
        m_new = jnp.maximum(m, colmax(s))
        p = jnp.exp2(s - m_new).astype(BF16)
        acc = jnp.exp2(m - m_new) * acc + jnp.dot(vt_ref[hs, pl.ds(start, size)], p,
                                                  preferred_element_type=F32)
        return m_new, acc

    state = [(jnp.full((1, hq), -jnp.inf, F32), jnp.zeros((HEAD_PAD, hq), F32)) for _ in chains]
    if n_main:
        def body(j, carry):
            s_cur, st = carry[0], list(carry[1:])
            start = pl.multiple_of(j * tk, tk)
            nxt = pl.multiple_of(jnp.minimum(j + 1, n_main - 1) * tk, tk)
            for c in range(nc):
                s_next = scores(start, tk, c + 1) if c + 1 < nc else scores(nxt, tk, 0)
                st[c] = update(s_cur, start, tk, c, *st[c])
                s_cur = s_next
            return (s_cur, *st)

        carry = lax.fori_loop(0, n_main, body, (scores(0, tk, 0), *state))
        state = list(carry[1:])
    if tail:
        for c in range(nc):
            state[c] = update(scores(n_main * tk, tail, c), n_main * tk, tail, c, *state[c])
    for (h, r), (_, acc) in zip(chains, state):
        o_ref[h * MLA_V:(h + 1) * MLA_V, r * hq:(r + 1) * hq] = (
            acc[0:MLA_V] / acc[MLA_V:MLA_V + 1]).astype(BF16)


def _flash(qt, k2, vt, *, n_q_tiles, tq, q_tile0, n_keys, key_block):
    n_main, tail = divmod(n_keys, ATT_TK)
    pairs = MLA_H // 2
    q_split = min(ATT_QSPLIT, max(1, tq // 256))
    kern = functools.partial(_flash_kernel, n_main=n_main, tk=ATT_TK, tail=tail, q_split=q_split)
    return pl.pallas_call(
        kern,
        grid=(SUBLANES, pairs, n_q_tiles),
        in_specs=[pl.BlockSpec((2 * HEAD_PAD, tq), lambda b, p, i: (b * pairs + p, q_tile0 + i)),
                  pl.BlockSpec((n_keys, 2 * HEAD_PAD), lambda b, p, i: (key_block, b * pairs + p)),
                  pl.BlockSpec((2 * HEAD_PAD, n_keys), lambda b, p, i: (b * pairs + p, key_block))],
        out_specs=pl.BlockSpec((2 * MLA_V, tq), lambda b, p, i: (b * pairs + p, i)),
        out_shape=jax.ShapeDtypeStruct((SUBLANES * MLA_W, n_q_tiles * tq), BF16),
        compiler_params=_cparams(("parallel", "parallel", "arbitrary")),
        name="mla_flash",
    )(qt, k2, vt)


def _s5_kernel(u_ref, bb_ref, cc_ref, a_ref, y_ref, st_ref, bu_ref, *, steps):
    d = pl.program_id(0)

    @pl.when(pl.program_id(1) == 0)
    def _():
        st_ref[...] = jnp.zeros_like(st_ref)

    bu_ref[...] = jnp.dot(u_ref[...].astype(BF16), bb_ref[0], preferred_element_type=F32)
    ar = jnp.broadcast_to(a_ref[0, 0:1, :], (SUBLANES, S5_STATE))
    ai = jnp.broadcast_to(a_ref[0, 1:2, :], (SUBLANES, S5_STATE))

    def body(i, carry):
        re, im = carry
        t = jnp.where(d == 0, i, steps - 1 - i)
        row = pl.multiple_of(t * SUBLANES, SUBLANES)
        bur = bu_ref[pl.ds(row, SUBLANES), 0:S5_STATE]
        bui = bu_ref[pl.ds(row, SUBLANES), S5_STATE:2 * S5_STATE]
        nre = ar * re - ai * im + bur
        nim = ar * im + ai * re + bui
        bu_ref[pl.ds(row, SUBLANES), 0:S5_STATE] = nre
        bu_ref[pl.ds(row, SUBLANES), S5_STATE:2 * S5_STATE] = nim
        return nre, nim

    re, im = lax.fori_loop(0, steps, body, (st_ref[0], st_ref[1]), unroll=4)
    st_ref[0] = re
    st_ref[1] = im
    y_ref[0] = jnp.dot(bu_ref[...].astype(BF16), cc_ref[0], preferred_element_type=F32)


def _s5_scan(z, bblk, cblk, a2, n_lat_steps, n_steps):
    n = z.shape[0]
    rows = S5_STEPS * SUBLANES
    n_chunks = n_steps // S5_STEPS
    n_lat_chunks = n_lat_steps // S5_STEPS

    def blk(d, s):
        return jnp.where(d == 0, (s + n_lat_chunks) % n_chunks, n_chunks - 1 - s)

    return pl.pallas_call(
        functools.partial(_s5_kernel, steps=S5_STEPS),
        grid=(2, n_chunks),
        in_specs=[pl.BlockSpec((rows, S5_W), lambda d, s: (blk(d, s), Z_U // S5_W)),
                  pl.BlockSpec((1, S5_W, 2 * S5_STATE), lambda d, s: (d, 0, 0)),
                  pl.BlockSpec((1, 2 * S5_STATE, S5_W), lambda d, s: (d, 0, 0)),
                  pl.BlockSpec((1, 2, S5_STATE), lambda d, s: (d, 0, 0))],
        out_specs=pl.BlockSpec((1, rows, S5_W), lambda d, s: (d, blk(d, s), 0)),
        out_shape=jax.ShapeDtypeStruct((2, n, S5_W), F32),
        scratch_shapes=[pltpu.VMEM((2, SUBLANES, S5_STATE), F32),
                        pltpu.VMEM((rows, 2 * S5_STATE), F32)],
        compiler_params=_cparams(("arbitrary", "arbitrary")),
        name="s5_scan",
    )(z, bblk, cblk, a2)


def _shift_rows(x, sh, rev):
    z = jnp.zeros((sh, x.shape[1]), x.dtype)
    if rev:
        return jnp.concatenate([x[sh:], z], 0)
    return jnp.concatenate([z, x[:-sh]], 0)


def _mlstm_kernel(q_ref, k_ref, v_ref, g_ref, qp_ref, qn_ref, kp_ref, kn_ref, cw_ref, cb_ref, gb_ref,
                  h_ref, cst_ref, mst_ref, qk_ref, qs_ref, ks_ref, vs_ref, hs_ref, bc_ref, *, rev, n_lat_chunks, n_chunks):
    T = ML_CHUNK
    rows = T * SUBLANES
    s = pl.program_id(0)
    c = (n_chunks - 1 - s) if rev else (s + n_lat_chunks) % n_chunks

    @pl.when(s == 0)
    def _():
        cst_ref[...] = jnp.zeros_like(cst_ref)
        mst_ref[...] = jnp.zeros_like(mst_ref)

    first = jnp.logical_or(c == 0, c == n_lat_chunks)
    last = jnp.logical_or(c == n_lat_chunks - 1, c == n_chunks - 1)
    keep_p = jnp.where(first, 0.0, 1.0)
    keep_n = jnp.where(last, 0.0, 1.0)
    qk_ref[0:SUBLANES, 0:ML_W] = qp_ref[...] * keep_p
    qk_ref[0:SUBLANES, ML_W:2 * ML_W] = kp_ref[...] * keep_p
    qk_ref[SUBLANES:SUBLANES + rows, 0:ML_W] = q_ref[...]
    qk_ref[SUBLANES:SUBLANES + rows, ML_W:2 * ML_W] = k_ref[...]
    qk_ref[SUBLANES + rows:2 * SUBLANES + rows, 0:ML_W] = qn_ref[...] * keep_n
    qk_ref[SUBLANES + rows:2 * SUBLANES + rows, ML_W:2 * ML_W] = kn_ref[...] * keep_n
    cw = cw_ref[...]
    conv = (cb_ref[...] + qk_ref[0:rows, :] * cw[0:1] + qk_ref[SUBLANES:SUBLANES + rows, :] * cw[1:2]
            + qk_ref[2 * SUBLANES:2 * SUBLANES + rows, :] * cw[2:3])
    conv = conv * jax.nn.sigmoid(conv)
    for j in range(2):
        qs_ref[j] = conv[:, j * 128:(j + 1) * 128]
        ks_ref[j] = conv[:, ML_W + j * 128:ML_W + (j + 1) * 128] * (ML_D ** -0.5)
        vs_ref[j] = v_ref[:, j * 128:(j + 1) * 128]

    g = g_ref[...] + gb_ref[...]
    lane = lax.broadcasted_iota(jnp.int32, g.shape, 1)
    is_f = jnp.logical_and(lane >= 2 * ML_H, lane < 4 * ML_H)
    gm = jnp.where(is_f, jax.nn.log_sigmoid(g), g)
    cum = jnp.where(is_f, gm, 0.0)
    sh = SUBLANES
    while sh < rows:
        cum = cum + _shift_rows(cum, sh, rev)
        sh *= 2
    bc_ref[0] = gm
    bc_ref[1] = cum

    i_lane0 = ML_H if rev else 0
    f_lane0 = 3 * ML_H if rev else 2 * ML_H
    end_row = 0 if rev else T - 1
    ti = lax.broadcasted_iota(jnp.int32, (T, T), 0)
    si = lax.broadcasted_iota(jnp.int32, (T, T), 1)
    order = (si >= ti) if rev else (si <= ti)
    lane_w = lax.broadcasted_iota(jnp.int32, (1, ML_W), 1) // ML_D
    rowhead = lax.broadcasted_iota(jnp.int32, (ML_W, 1), 0) // ML_D
    colhead = lax.broadcasted_iota(jnp.int32, (1, ML_AUG), 1)
    colhead = jnp.where(colhead < ML_W, colhead // ML_D, colhead - ML_W)
    blockmask = rowhead == colhead
    ones_aug = jnp.ones((T, ML_AUG - ML_W), F32)

    def per_batch(b, _):
        sl = pl.ds(b, T, stride=SUBLANES)
        qb = jnp.concatenate([qs_ref[0, sl, :], qs_ref[1, sl, :]], -1)
        kb = jnp.concatenate([ks_ref[0, sl, :], ks_ref[1, sl, :]], -1)
        vb = jnp.concatenate([vs_ref[0, sl, :], vs_ref[1, sl, :]], -1)
        gcol = bc_ref[0, sl, :]
        ccol = bc_ref[1, sl, :]
        grow = gcol.T
        crow = ccol.T
        cst = cst_ref[b]
        qc = jnp.dot(qb.astype(BF16), cst.astype(BF16), preferred_element_type=F32)
        vaug = jnp.concatenate([vb, ones_aug], -1).astype(BF16)
        kb16 = kb.astype(BF16)
        num = jnp.zeros((T, ML_W), F32)
        den = jnp.zeros((T, ML_W), F32)
        stab = jnp.zeros((T, ML_W), F32)
        ke_scale = jnp.zeros((T, ML_W), F32)
        a_col = jnp.zeros((ML_W, 1), F32)
        s_col = jnp.zeros((ML_W, 1), F32)
        for h in range(ML_H):
            hm = lane_w == h
            il, fl = i_lane0 + h, f_lane0 + h
            b_col = ccol[:, fl:fl + 1]
            i_col = gcol[:, il:il + 1]
            b_row = crow[fl:fl + 1, :]
            i_row = grow[il:il + 1, :]
            b_tot = ccol[end_row:end_row + 1, fl:fl + 1]
            m_in = mst_ref[b, h][0:1, 0:1]
            ld = jnp.where(order, b_col - b_row + i_row, -jnp.inf)
            m_t = jnp.maximum(b_col + m_in, jnp.max(ld, -1, keepdims=True))
            dw = jnp.exp(ld - m_t)
            w_inter = jnp.exp(b_col + m_in - m_t)
            qh = jnp.where(hm, qb, 0.0).astype(BF16)
            sc = lax.dot_general(qh, kb16, (((1,), (1,)), ((), ())), preferred_element_type=F32) * dw
            pv = jnp.dot(sc.astype(BF16), vaug[:, 0:ML_W], preferred_element_type=F32)
            den_h = jnp.sum(sc, -1, keepdims=True) + w_inter * qc[:, ML_W + h:ML_W + h + 1]
            num = num + jnp.where(hm, pv + w_inter * qc[:, 0:ML_W], 0.0)
            den = den + jnp.where(hm, den_h, 0.0)
            stab = stab + jnp.where(hm, jnp.exp(-m_t), 0.0)
            w_end = b_tot - b_col + i_col
            m_loc = jnp.max(w_end, 0, keepdims=True)
            m_new = jnp.maximum(b_tot + m_in, m_loc)
            ke_scale = ke_scale + jnp.where(hm, jnp.exp(w_end - m_loc), 0.0)
            a_col = a_col + jnp.where(rowhead == h, jnp.exp(b_tot + m_in - m_new), 0.0)
            s_col = s_col + jnp.where(rowhead == h, jnp.exp(m_loc - m_new), 0.0)
            mst_ref[b, h] = jnp.broadcast_to(m_new, (SUBLANES, 128))
        hout = num / jnp.maximum(jnp.abs(den), stab)
        hs_ref[0, sl, :] = hout[:, 0:128]
        hs_ref[1, sl, :] = hout[:, 128:256]
        ke = (kb * ke_scale).astype(BF16)
        upd = lax.dot_general(ke, vaug, (((0,), (0,)), ((), ())), preferred_element_type=F32)
        cst_ref[b] = a_col * cst + jnp.where(blockmask, s_col * upd, 0.0)
        return 0

    lax.fori_loop(0, SUBLANES, per_batch, 0)
    h_ref[...] = jnp.concatenate([hs_ref[0], hs_ref[1]], -1)


def _mlstm(z, conv_w, conv_b, gate_b_pad, *, rev, n_lat_steps, n_steps):
    n = z.shape[0]
    T = ML_CHUNK
    rows = T * SUBLANES
    n_chunks = n_steps // T
    n_lat_chunks = n_lat_steps // T
    hb = rows // SUBLANES
    n_hblk = n // SUBLANES

    def blk(s):
        return (n_chunks - 1 - s) if rev else (s + n_lat_chunks) % n_chunks

    prev = lambda s: jnp.maximum(blk(s) * hb - 1, 0)
    nxt = lambda s: jnp.minimum((blk(s) + 1) * hb, n_hblk - 1)
    const = lambda s: (0, 0)
    kern = functools.partial(_mlstm_kernel, rev=rev, n_lat_chunks=n_lat_chunks, n_chunks=n_chunks)
    return pl.pallas_call(
        kern,
        grid=(n_chunks,),
        in_specs=[pl.BlockSpec((rows, ML_W), lambda s: (blk(s), Z_MLQ // ML_W)),
                  pl.BlockSpec((rows, ML_W), lambda s: (blk(s), Z_MLK // ML_W)),
                  pl.BlockSpec((rows, ML_W), lambda s: (blk(s), Z_MLV // ML_W)),
                  pl.BlockSpec((rows, 128), lambda s: (blk(s), Z_MISC // 128)),
                  pl.BlockSpec((SUBLANES, ML_W), lambda s: (prev(s), Z_MLQ // ML_W)),
                  pl.BlockSpec((SUBLANES, ML_W), lambda s: (nxt(s), Z_MLQ // ML_W)),
                  pl.BlockSpec((SUBLANES, ML_W), lambda s: (prev(s), Z_MLK // ML_W)),
                  pl.BlockSpec((SUBLANES, ML_W), lambda s: (nxt(s), Z_MLK // ML_W)),
                  pl.BlockSpec((3, 2 * ML_W), const),
                  pl.BlockSpec((1, 2 * ML_W), const),
                  pl.BlockSpec((1, 128), const)],
        out_specs=pl.BlockSpec((rows, ML_W), lambda s: (blk(s), 0)),
        out_shape=jax.ShapeDtypeStruct((n, ML_W), F32),
        scratch_shapes=[pltpu.VMEM((SUBLANES, ML_W, ML_AUG), F32),
                        pltpu.VMEM((SUBLANES, ML_H, SUBLANES, 128), F32),
                        pltpu.VMEM((rows + 2 * SUBLANES, 2 * ML_W), F32),
                        pltpu.VMEM((2, rows, 128), F32),
                        pltpu.VMEM((2, rows, 128), F32),
                        pltpu.VMEM((2, rows, 128), F32),
                        pltpu.VMEM((2, rows, 128), F32),
                        pltpu.VMEM((2, rows, 128), F32)],
        compiler_params=_cparams(("arbitrary",)),
        name="mlstm_rev" if rev else "mlstm_fwd",
    )(z, z, z, z, z, z, z, z, conv_w, conv_b, gate_b_pad)


def _post_kernel(x_ref, g1_ref, sc_ref, sh_ref, u_ref, y_ref, o_ref, attn_ref, hf_ref, hb_ref,
                 d_ref, gw_ref, gb_ref, hn_ref, pm_ref, wo_ref, l1g_ref, l1b_ref, rw_ref,
                 x1_ref, f_ref, lg_ref):
    y = y_ref[0] + y_ref[1] + u_ref[...] * d_ref[...]
    g = jax.nn.gelu(y)
    s5 = g * jax.nn.sigmoid(jnp.dot(g.astype(BF16), gw_ref[...], preferred_element_type=F32) + gb_ref[...])
    hh = jax.nn.sigmoid(o_ref[...]) * (hf_ref[...] + hb_ref[...])
    pm = pm_ref[...]
    mu = jnp.dot(hh, pm, preferred_element_type=F32, precision=lax.Precision.HIGHEST)
    hc = hh - mu
    var = jnp.dot(hc * hc, pm, preferred_element_type=F32, precision=lax.Precision.HIGHEST)
    ml = hc * lax.rsqrt(var + LN_EPS) * hn_ref[...]
    mix = jnp.concatenate([s5.astype(BF16), attn_ref[...], ml.astype(BF16)], -1)
    yo = jnp.dot(mix, wo_ref[...], preferred_element_type=F32)
    x1 = _ln(DEEPNORM_ALPHA * x_ref[...] + _per_batch(yo, g1_ref[0])) * l1g_ref[...] + l1b_ref[...]
    x1_ref[...] = x1
    f = _modulate(_ln(x1), sc_ref[0], sh_ref[0])
    _store_tile_rows(f_ref, f)
    lg_ref[...] = lax.dot_general(rw_ref[...], f, (((1,), (1,)), ((), ())), preferred_element_type=F32,
                                  precision=lax.Precision.HIGHEST)


def _post(x, mod, z, y2, attn, hf, hb, p, n_lat_tiles):
    n = x.shape[0]
    tm = ROW_TILE
    sel = lambda i: (i >= n_lat_tiles).astype(jnp.int32)
    const = lambda i: (0, 0)
    row = lambda w: pl.BlockSpec((tm, w), lambda i: (i, 0))
    return pl.pallas_call(
        _post_kernel,
        grid=(n // tm,),
        in_specs=[row(D_MODEL),
                  pl.BlockSpec((1, SUBLANES, D_MODEL), lambda i: (sel(i), 0, 2)),
                  pl.BlockSpec((1, SUBLANES, D_MODEL), lambda i: (sel(i), 0, 4)),
                  pl.BlockSpec((1, SUBLANES, D_MODEL), lambda i: (sel(i), 0, 3)),
                  pl.BlockSpec((tm, S5_W), lambda i: (i, Z_U // S5_W)),
                  pl.BlockSpec((2, tm, S5_W), lambda i: (0, i, 0)),
                  pl.BlockSpec((tm, ML_W), lambda i: (i, Z_MLO // ML_W)),
                  row(MLA_W), row(ML_W), row(ML_W),
                  pl.BlockSpec((1, S5_W), const),
                  pl.BlockSpec((S5_W, S5_W), const),
                  pl.BlockSpec((1, S5_W), const),
                  pl.BlockSpec((1, ML_W), const),
                  pl.BlockSpec((ML_W, ML_W), const),
                  pl.BlockSpec((D_MODEL, D_MODEL), const),
                  pl.BlockSpec((1, D_MODEL), const),
                  pl.BlockSpec((1, D_MODEL), const),
                  pl.BlockSpec((N_EXPERTS, D_MODEL), const)],
        out_specs=[row(D_MODEL), pl.BlockSpec((tm * SUBLANES, 128), lambda i: (i, 0)),
                   pl.BlockSpec((N_EXPERTS, tm), lambda i: (0, i))],
        out_shape=[jax.ShapeDtypeStruct((n, D_MODEL), F32),
                   jax.ShapeDtypeStruct((n * SUBLANES, 128), F32),
                   jax.ShapeDtypeStruct((N_EXPERTS, n), F32)],
        compiler_params=_cparams(("parallel",)),
        name="post_mix",
    )(x, mod, mod, mod, z, y2, z, attn, hf, hb,
      p["s5_d"], p["glu_w"], p["glu_b"], p["ml_norm_g"], p["head_mean"], p["w_out"], p["ln1_g"], p["ln1_b"],
      p["router_w"])


def _expert_kernel(be_ref, x_ref, wg_ref, wu_ref, wd_ref, o_ref):
    x = _load_tile_rows(x_ref).astype(BF16)
    a = jnp.dot(x, wg_ref[0], preferred_element_type=F32)
    u = jnp.dot(x, wu_ref[0], preferred_element_type=F32)
    hmid = (a * jax.nn.sigmoid(a) * u).astype(BF16)
    _store_tile_rows(o_ref, jnp.dot(hmid, wd_ref[0], preferred_element_type=F32))


def _experts(blk_expert, xs, wg, wu, wd):
    n_rows = xs.shape[0] // SUBLANES
    tb = MOE_BLOCK
    rows = pl.BlockSpec((tb * SUBLANES, 128), lambda i, be: (i, 0))
    grid_spec = pltpu.PrefetchScalarGridSpec(
        num_scalar_prefetch=1,
        grid=(n_rows // tb,),
        in_specs=[rows,
                  pl.BlockSpec((1, D_MODEL, EXPERT_F), lambda i, be: (be[i], 0, 0)),
                  pl.BlockSpec((1, D_MODEL, EXPERT_F), lambda i, be: (be[i], 0, 0)),
                  pl.BlockSpec((1, EXPERT_F, D_MODEL), lambda i, be: (be[i], 0, 0))],
        out_specs=rows,
    )
    return pl.pallas_call(
        _expert_kernel,
        grid_spec=grid_spec,
        out_shape=jax.ShapeDtypeStruct((n_rows * SUBLANES, 128), F32),
        compiler_params=_cparams(("arbitrary",)),
        name="moe_experts",
    )(blk_expert, xs, wg, wu, wd)


def _idx_copy(idx_hbm, idx_smem, sem, tile, slot, width):
    return pltpu.make_async_copy(idx_hbm.at[tile], idx_smem.at[pl.ds(pl.multiple_of(slot * width, width), width)],
                                 sem.at[slot])


def _row_groups(tm, base, fn):
    def trip(g, c):
        t0 = pl.multiple_of(g * SUBLANES, SUBLANES)
        for j in range(SUBLANES):
            for k in range(TOP_K):
                fn(k, t0 + j, base + (k * tm + j) + t0)
        return c

    lax.fori_loop(0, tm // SUBLANES, trip, 0)


def _dispatch_kernel(pe_ref, pc_ref, f_ref, idx_hbm, xs_hbm, idx_smem, zero_ref, isem, zsem, rsem, *, tm):
    i = pl.program_id(0)
    n = pl.num_programs(0)
    slot = i % 2
    tb = MOE_BLOCK
    width = TOP_K * tm

    def row_copy(t, r):
        return pltpu.make_async_copy(f_ref.at[t], xs_hbm.at[r], rsem)

    def pad_copy(e):
        start = pl.multiple_of(pe_ref[e] - tb, tb)
        return pltpu.make_async_copy(zero_ref, xs_hbm.at[pl.ds(start, tb)], zsem)

    @pl.when(i == 0)
    def _():
        _idx_copy(idx_hbm, idx_smem, isem, 0, 0, width).start()
        zero_ref[...] = jnp.zeros_like(zero_ref)

        def fill(e, c):
            @pl.when(pc_ref[e] > 0)
            def _():
                pad_copy(e).start()
            return c

        def drain(e, c):
            @pl.when(pc_ref[e] > 0)
            def _():
                pad_copy(e).wait()
            return c

        lax.fori_loop(0, N_EXPERTS, fill, 0)
        lax.fori_loop(0, N_EXPERTS, drain, 0)

    _idx_copy(idx_hbm, idx_smem, isem, i, slot, width).wait()

    @pl.when(i + 1 < n)
    def _():
        _idx_copy(idx_hbm, idx_smem, isem, i + 1, 1 - slot, width).start()

    base = slot * width
    _row_groups(tm, base, lambda k, t, a: row_copy(t, idx_smem[a]).start(priority=k % 2))
    _row_groups(tm, base, lambda k, t, a: row_copy(t, idx_smem[a]).wait())


def _dispatch_rows(f3, idx_tiles, pad_end, padded, n_rows):
    n = f3.shape[0]
    tm = MOE_TILE
    grid_spec = pltpu.PrefetchScalarGridSpec(
        num_scalar_prefetch=2,
        grid=(n // tm,),
        in_specs=[pl.BlockSpec((tm,) + ROW_TILE_SHAPE, lambda i, pe, pc: (i, 0, 0)),
                  pl.BlockSpec(memory_space=pl.ANY)],
        out_specs=pl.BlockSpec(memory_space=pl.ANY),
        scratch_shapes=[pltpu.SMEM((2 * TOP_K * tm,), jnp.int32),
                        pltpu.VMEM((MOE_BLOCK,) + ROW_TILE_SHAPE, F32),
                        pltpu.SemaphoreType.DMA((2,)),
                        pltpu.SemaphoreType.DMA,
                        pltpu.SemaphoreType.DMA],
    )
    return pl.pallas_call(
        functools.partial(_dispatch_kernel, tm=tm),
        grid_spec=grid_spec,
        out_shape=jax.ShapeDtypeStruct((n_rows,) + ROW_TILE_SHAPE, F32),
        compiler_params=_cparams(("arbitrary",)),
        name="moe_dispatch",
    )(pad_end, padded, f3, idx_tiles)


def _final_kernel(x1_ref, f_ref, gate_ref, g2_ref, wg_ref, wu_ref, wd_ref, lg_ref, lb_ref, idx_hbm, ys_hbm,
                  o_ref, idx_smem, buf_ref, isem, rsem, *, tm):
    i = pl.program_id(0)
    n = pl.num_programs(0)
    width = TOP_K * tm

    def row_copy(slot, k, t, r):
        dst = buf_ref.at[slot, k, pl.ds(pl.multiple_of(t * SUBLANES, SUBLANES), SUBLANES)]
        return pltpu.make_async_copy(ys_hbm.at[r], dst, rsem.at[slot])

    def gather(tile_slot):
        base = tile_slot * width
        _row_groups(tm, base, lambda k, t, a: row_copy(tile_slot, k, t, idx_smem[a]).start(priority=k % 2))

    @pl.when(i == 0)
    def _():
        _idx_copy(idx_hbm, idx_smem, isem, 0, 0, width).start()
        _idx_copy(idx_hbm, idx_smem, isem, 0, 0, width).wait()
        gather(0)

        @pl.when(n > 1)
        def _():
            _idx_copy(idx_hbm, idx_smem, isem, 1, 1, width).start()

    cur = i % 2

    @pl.when(i + 1 < n)
    def _():
        _idx_copy(idx_hbm, idx_smem, isem, i + 1, 1 - cur, width).wait()
        gather(1 - cur)

    @pl.when(i + 2 < n)
    def _():
        _idx_copy(idx_hbm, idx_smem, isem, i + 2, cur, width).start()

    f = _load_tile_rows(f_ref).astype(BF16)
    a = jnp.dot(f, wg_ref[...], preferred_element_type=F32)
    u = jnp.dot(f, wu_ref[...], preferred_element_type=F32)
    ffn = jnp.dot((a * jax.nn.sigmoid(a) * u).astype(BF16), wd_ref[...], preferred_element_type=F32)
    _row_groups(tm, 0, lambda k, t, a: row_copy(cur, k, t, 0).wait())
    gate = gate_ref[...]
    for k in range(TOP_K):
        ffn = ffn + gate[:, k:k + 1] * _load_tile_rows(buf_ref.at[cur, k])
    o_ref[...] = _ln(DEEPNORM_ALPHA * x1_ref[...] + _per_batch(ffn, g2_ref[0])) * lg_ref[...] + lb_ref[...]


def _final(x1, f3, gate_t, idx_tiles, ys, mod, p, n_lat_tiles, n_out_tiles):
    tm = MOE_TILE
    n_lat_tiles = n_lat_tiles * ROW_TILE // tm
    n_out_tiles = n_out_tiles * ROW_TILE // tm
    sel = lambda i: (i >= n_lat_tiles).astype(jnp.int32)
    const = lambda i: (0, 0)
    row = lambda w: pl.BlockSpec((tm, w), lambda i: (i, 0))
    return pl.pallas_call(
        functools.partial(_final_kernel, tm=tm),
        grid=(n_out_tiles,),
        in_specs=[row(D_MODEL), pl.BlockSpec((tm * SUBLANES, 128), lambda i: (i, 0)), row(TOP_K),
                  pl.BlockSpec((1, SUBLANES, D_MODEL), lambda i: (sel(i), 0, 5)),
                  pl.BlockSpec((D_MODEL, EXPERT_F), const),
                  pl.BlockSpec((D_MODEL, EXPERT_F), const),
                  pl.BlockSpec((EXPERT_F, D_MODEL), const),
                  pl.BlockSpec((1, D_MODEL), const),
                  pl.BlockSpec((1, D_MODEL), const),
                  pl.BlockSpec(memory_space=pl.ANY),
                  pl.BlockSpec(memory_space=pl.ANY)],
        out_specs=row(D_MODEL),
        out_shape=jax.ShapeDtypeStruct((n_out_tiles * tm, D_MODEL), F32),
        scratch_shapes=[pltpu.SMEM((2 * TOP_K * tm,), jnp.int32),
                        pltpu.VMEM((2, TOP_K, tm * SUBLANES, 128), F32),
                        pltpu.SemaphoreType.DMA((2,)),
                        pltpu.SemaphoreType.DMA((2,))],
        compiler_params=_cparams(("arbitrary",)),
        name="final_ffn",
    )(x1, f3, gate_t, mod, p["sh_w_gate"], p["sh_w_up"], p["sh_w_down"], p["ln2_g"], p["ln2_b"], idx_tiles, ys)


def _route_kernel(lg_ref, bias_ref, tri_ref, e_ref, g_ref, r_ref, cnt_ref, carry_ref):
    @pl.when(pl.program_id(0) == 0)
    def _():
        carry_ref[...] = jnp.zeros_like(carry_ref)

    tm = lg_ref.shape[1]
    neg = -jnp.inf
    s = jax.nn.sigmoid(lg_ref[...])
    g3 = (s + bias_ref[...]).reshape(N_EXPERT_GROUPS, EPG, tm)
    io3 = lax.broadcasted_iota(jnp.int32, (N_EXPERT_GROUPS, EPG, tm), 1)
    m1 = jnp.max(g3, 1, keepdims=True)
    f1 = jnp.min(jnp.where(g3 == m1, io3, EPG), 1, keepdims=True)
    m2 = jnp.max(jnp.where(io3 == f1, neg, g3), 1, keepdims=True)
    gs = m1 + m2
    iog = lax.broadcasted_iota(jnp.int32, (N_EXPERT_GROUPS, 1, tm), 0)
    cur = gs
    kth = gs
    for _ in range(TOP_GROUPS):
        kth = jnp.max(cur, 0, keepdims=True)
        fi = jnp.min(jnp.where(cur == kth, iog, N_EXPERT_GROUPS), 0, keepdims=True)
        cur = jnp.where(iog == fi, neg, cur)
    cand = jnp.where(gs >= kth, g3, neg).reshape(N_EXPERTS, tm)
    io = lax.broadcasted_iota(jnp.int32, (N_EXPERTS, tm), 0)
    memb = jnp.zeros((N_EXPERTS, tm), F32)
    es, gates, hots = [], [], []
    for _ in range(TOP_K):
        mk = jnp.max(cand, 0, keepdims=True)
        ik = jnp.min(jnp.where(cand == mk, io, N_EXPERTS), 0, keepdims=True)
        oh = io == ik
        gates.append(jnp.sum(jnp.where(oh, s, 0.0), 0, keepdims=True))
        es.append(ik)
        hots.append(oh)
        cand = jnp.where(oh, neg, cand)
        memb = memb + jnp.where(oh, 1.0, 0.0)
    gsum = gates[0]
    for gk in gates[1:]:
        gsum = gsum + gk
    g_ref[...] = jnp.concatenate(gates, 0) / gsum * ROUTED_SCALE
    e_ref[...] = jnp.concatenate(es, 0)
    pref = jnp.dot(memb.astype(BF16), tri_ref[...], preferred_element_type=F32) + carry_ref[:, 0:1]
    ranks = [jnp.sum(jnp.where(oh, pref, 0.0), 0, keepdims=True) for oh in hots]
    r_ref[...] = jnp.concatenate(ranks, 0).astype(jnp.int32)
    total = carry_ref[...] + jnp.sum(memb, 1, keepdims=True)
    carry_ref[...] = total
    cnt_ref[...] = total


def _route(logits_t, router_bias):
    n = logits_t.shape[1]
    tm = ROW_TILE
    tri = (jnp.arange(tm)[:, None] < jnp.arange(tm)[None, :]).astype(BF16)
    const = lambda i: (0, 0)
    col = pl.BlockSpec((TOP_K, tm), lambda i: (0, i))
    top_e, gate, rank, cnt = pl.pallas_call(
        _route_kernel,
        grid=(n // tm,),
        in_specs=[pl.BlockSpec((N_EXPERTS, tm), lambda i: (0, i)),
                  pl.BlockSpec((N_EXPERTS, 1), const),
                  pl.BlockSpec((tm, tm), const)],
        out_specs=[col, col, col, pl.BlockSpec((N_EXPERTS, 128), const)],
        out_shape=[jax.ShapeDtypeStruct((TOP_K, n), jnp.int32),
                   jax.ShapeDtypeStruct((TOP_K, n), F32),
                   jax.ShapeDtypeStruct((TOP_K, n), jnp.int32),
                   jax.ShapeDtypeStruct((N_EXPERTS, 128), F32)],
        scratch_shapes=[pltpu.VMEM((N_EXPERTS, 128), F32)],
        compiler_params=_cparams(("arbitrary",)),
        name="moe_route",
    )(logits_t, router_bias.astype(F32).reshape(N_EXPERTS, 1), tri)
    return top_e, gate, rank, cnt[:, 0].astype(jnp.int32)


def _dispatch(top_e, rank, counts):
    n_tok = top_e.shape[1]
    tb = MOE_BLOCK
    padded = (counts + tb - 1) // tb * tb
    pad_end = jnp.cumsum(padded)
    pad_start = pad_end - padded
    hot = top_e[:, :, None] == jnp.arange(N_EXPERTS, dtype=top_e.dtype)
    dest = (jnp.sum(jnp.where(hot, pad_start.astype(jnp.int32), 0), -1) + rank).astype(jnp.int32)
    n_rows = -(-(n_tok * TOP_K + N_EXPERTS * (tb - 1)) // tb) * tb
    n_blocks = n_rows // tb
    blk_start = jnp.arange(n_blocks, dtype=jnp.int32) * tb
    blk_expert = jnp.minimum(jnp.sum((pad_end[None, :] <= blk_start[:, None]).astype(jnp.int32), 1),
                             N_EXPERTS - 1)
    tm = MOE_TILE
    idx_tiles = jnp.transpose(dest.reshape(TOP_K, n_tok // tm, tm), (1, 0, 2)).reshape(n_tok // tm, TOP_K * tm)
    return idx_tiles, blk_expert, pad_end.astype(jnp.int32), padded.astype(jnp.int32), n_rows


def _pad_w_in(w_in):
    s5u, qc, kvc, kr, mq, mk, mv, mo, gates = jnp.split(
        w_in, [256, 512, 640, 672, 928, 1184, 1440, 1696], axis=1)
    misc = jnp.zeros((D_MODEL, 128), w_in.dtype)
    misc = misc.at[:, MISC_GATES:MISC_GATES + 16].set(gates).at[:, MISC_ROPE:MISC_ROPE + MLA_ROPE].set(kr)
    return jnp.concatenate([s5u, qc, mq, mk, mv, mo, kvc, misc], axis=1).astype(BF16)


def _pad_heads(w, width):
    k = w.shape[0]
    w3 = w.reshape(k, MLA_H, width)
    return jnp.pad(w3, ((0, 0), (0, 0), (0, HEAD_PAD - width))).reshape(k, MLA_H * HEAD_PAD)


def _rope_tables(n_lat, n_ctx):
    t = jnp.arange(n_lat)
    half = ROPE_AXIS // 2
    inv_freq = ROPE_BASE ** (-jnp.arange(half, dtype=F32) / half)
    ang_r = (t // GRID_W).astype(F32)[:, None] * inv_freq
    ang_c = (t % GRID_W).astype(F32)[:, None] * inv_freq
    ang = jnp.concatenate([ang_r, ang_r, ang_c, ang_c], -1)
    ang = jnp.concatenate([ang, jnp.zeros((n_ctx, MLA_ROPE), F32)], 0)
    cos, sin = jnp.cos(ang), jnp.sin(ang)
    first = (jnp.arange(MLA_ROPE) % ROPE_AXIS) < half

    def place(v, fill):
        out = jnp.full((v.shape[0], HEAD_PAD), fill, F32)
        return out.at[:, MLA_NOPE:MLA_NOPE + MLA_ROPE].set(v)

    q_scale = MLA_SCALE * math.log2(math.e)
    c_q = place(cos, 1.0) * q_scale
    m_q = place(jnp.where(first, -sin, 0.0), 0.0) * q_scale
    p_q = place(jnp.where(first, 0.0, sin), 0.0) * q_scale
    c_k = place(cos, 0.0)
    m_k = place(jnp.where(first, -sin, 0.0), 0.0)
    p_k = place(jnp.where(first, 0.0, sin), 0.0)
    return jnp.stack([c_q, m_q, p_q, c_k, m_k, p_k], 0)


def _s5_params(lam_re, lam_im, log_step, b_re, b_im, c_re, c_im):
    lr, li = lam_re.astype(F32), lam_im.astype(F32)
    step = jnp.exp(log_step.astype(F32))[..., None]
    mag = jnp.exp(lr * step)
    ar, ai = mag * jnp.cos(li * step), mag * jnp.sin(li * step)
    nr, ni = ar - 1.0, ai
    den = lr * lr + li * li
    cr, ci = ((nr * lr + ni * li) / den)[..., None], ((ni * lr - nr * li) / den)[..., None]
    br, bi = b_re.astype(F32), b_im.astype(F32)
    bbr, bbi = cr * br - ci * bi, cr * bi + ci * br
    eye = jnp.eye(S5_G, dtype=F32)
    b_blk = lambda v: jnp.einsum("dgcp,gh->dgchp", jnp.transpose(v, (0, 1, 3, 2)), eye).reshape(2, S5_W, S5_STATE)
    bblk = jnp.concatenate([b_blk(bbr), b_blk(bbi)], -1)
    c_blk = lambda v: jnp.einsum("dgpc,gh->dgphc", jnp.transpose(v.astype(F32), (0, 1, 3, 2)), eye).reshape(
        2, S5_STATE, S5_W)
    cblk = jnp.concatenate([c_blk(c_re), -c_blk(c_im)], 1)
    a2 = jnp.stack([ar.reshape(2, S5_STATE), ai.reshape(2, S5_STATE)], 1)
    return bblk.astype(BF16), cblk.astype(BF16), a2


def kernel(x, c, ctx, c_ctx, ada_w, ada_b, w_in, s5_lambda_re, s5_lambda_im, s5_log_step, s5_b_re, s5_b_im, s5_c_re, s5_c_im, s5_d, s5_glu_w, s5_glu_b, mla_q_norm, mla_w_q_up, mla_kv_norm, mla_w_kv_up, ml_conv_w, ml_conv_b, ml_gate_b, ml_norm_g, w_out, ln1_g, ln1_b, ln2_g, ln2_b, router_w, router_bias, exp_w_gate, exp_w_up, exp_w_down, sh_w_gate, sh_w_up, sh_w_down):
    bsz, n_lat, d = x.shape
    n_ctx = ctx.shape[1]
    depth = ada_w.shape[0]
    assert bsz == SUBLANES and d == D_MODEL
    assert n_lat % ATT_TQ == 0 and n_lat % ML_CHUNK == 0 and n_ctx % ML_CHUNK == 0
    assert (n_lat * bsz) % ROW_TILE == 0 and (n_ctx * bsz) % ROW_TILE == 0
    n_steps = n_lat + n_ctx
    n_tok = n_steps * bsz
    n_lat_tiles = n_lat * bsz // ROW_TILE

    xs = jnp.concatenate([jnp.transpose(x, (1, 0, 2)), jnp.transpose(ctx, (1, 0, 2))], 0).reshape(n_tok, d)
    cc = jnp.concatenate([c, jnp.broadcast_to(c_ctx[None], (bsz, d))], 0)
    tabs = _rope_tables(n_lat, n_ctx)
    head_mean = jnp.kron(jnp.eye(ML_H, dtype=F32), jnp.full((ML_D, ML_D), 1.0 / ML_D, F32))

    for layer in range(depth):
        last = layer == depth - 1
        mod = _ada_mod(cc, ada_w[layer], ada_b[layer]).reshape(2, bsz, 6 * d)
        z = _inproj(xs, mod, _pad_w_in(w_in[layer]), n_lat_tiles)

        w_kv = mla_w_kv_up[layer].reshape(MLA_KVR, MLA_H, MLA_NOPE + MLA_V)
        wk_pad = _pad_heads(w_kv[:, :, :MLA_NOPE].reshape(MLA_KVR, MLA_H * MLA_NOPE), MLA_NOPE).astype(BF16)
        wv = _pad_heads(w_kv[:, :, MLA_NOPE:].reshape(MLA_KVR, MLA_W), MLA_V).astype(BF16)
        wq_pad = _pad_heads(mla_w_q_up[layer], MLA_NOPE + MLA_ROPE).astype(BF16)
        q, k, v = _mla_prep(z, tabs, mla_q_norm[layer].reshape(1, -1), wq_pad,
                            mla_kv_norm[layer].reshape(1, -1), wk_pad, wv)
        qt = q.reshape(n_steps, bsz * MLA_H * HEAD_PAD).T
        k2 = k.reshape(n_steps, bsz * MLA_H * HEAD_PAD)
        vt = v.reshape(n_steps, bsz * MLA_H * HEAD_PAD).T
        attn = _flash(qt, k2, vt, n_q_tiles=n_lat // ATT_TQ, tq=ATT_TQ, q_tile0=0, n_keys=n_steps, key_block=0)
        if last:
            attn_ctx = jnp.zeros((bsz * MLA_W, n_ctx), BF16)
        else:
            attn_ctx = _flash(qt, k2, vt, n_q_tiles=1, tq=n_ctx, q_tile0=n_lat // n_ctx, n_keys=n_ctx,
                              key_block=n_lat // n_ctx)
        attn = jnp.concatenate([attn, attn_ctx], 1).T.reshape(n_tok, MLA_W)

        bblk, cblk, a2 = _s5_params(s5_lambda_re[layer], s5_lambda_im[layer], s5_log_step[layer],
                                    s5_b_re[layer], s5_b_im[layer], s5_c_re[layer], s5_c_im[layer])
        y2 = _s5_scan(z, bblk, cblk, a2, n_lat, n_steps)

        gate_b_pad = jnp.zeros((1, 128), F32).at[0, :4 * ML_H].set(ml_gate_b[layer].reshape(4 * ML_H))
        cw, cb = ml_conv_w[layer], ml_conv_b[layer].reshape(1, -1)
        hf = _mlstm(z, cw, cb, gate_b_pad, rev=False, n_lat_steps=n_lat, n_steps=n_steps)
        hb = _mlstm(z, cw, cb, gate_b_pad, rev=True, n_lat_steps=n_lat, n_steps=n_steps)

        p = dict(s5_d=s5_d[layer].reshape(1, -1), glu_w=s5_glu_w[layer].astype(BF16),
                 glu_b=s5_glu_b[layer].reshape(1, -1), ml_norm_g=ml_norm_g[layer].reshape(1, -1),
                 head_mean=head_mean, w_out=w_out[layer].astype(BF16), ln1_g=ln1_g[layer].reshape(1, -1),
                 ln1_b=ln1_b[layer].reshape(1, -1), router_w=router_w[layer].T,
                 sh_w_gate=sh_w_gate[layer].astype(BF16), sh_w_up=sh_w_up[layer].astype(BF16),
                 sh_w_down=sh_w_down[layer].astype(BF16), ln2_g=ln2_g[layer].reshape(1, -1),
                 ln2_b=ln2_b[layer].reshape(1, -1))
        x1, f3, logits = _post(xs, mod, z, y2, attn, hf, hb, p, n_lat_tiles)

        top_e, gate, rank, counts = _route(logits, router_bias[layer])
        idx_tiles, blk_expert, pad_end, padded, n_rows = _dispatch(top_e, rank, counts)
        rows_in = _dispatch_rows(f3.reshape((n_tok,) + ROW_TILE_SHAPE), idx_tiles, pad_end, padded, n_rows)
        rows_out = _experts(blk_expert, rows_in.reshape(n_rows * SUBLANES, 128), exp_w_gate[layer].astype(BF16),
                            exp_w_up[layer].astype(BF16), exp_w_down[layer].astype(BF16))
        n_out_tiles = n_lat_tiles if last else n_tok // ROW_TILE
        xs = _final(x1, f3, gate.T, idx_tiles, rows_out.reshape((n_rows,) + ROW_TILE_SHAPE), mod, p,
                    n_lat_tiles, n_out_tiles)

    return jnp.transpose(xs[:n_lat * bsz].reshape(n_lat, bsz, d), (1, 0, 2))
```

```python
import functools
import math

import jax
import jax.numpy as jnp
from jax import lax
from jax.experimental import pallas as pl
from jax.experimental.pallas import tpu as pltpu

F32 = jnp.float32
BF16 = jnp.bfloat16

D_MODEL = 1024
GRID_W = 64
S5_W = 256
S5_GC = 16
S5_G = 16
S5_P = 64
S5_STATE = S5_G * S5_P
MLA_H = 8
MLA_NOPE = 64
MLA_ROPE = 32
MLA_V = 64
MLA_QR = 256
MLA_KVR = 128
MLA_W = MLA_H * MLA_V
MLA_SCALE = (MLA_NOPE + MLA_ROPE) ** -0.5
ROPE_AXIS = MLA_ROPE // 2
ROPE_BASE = 10000.0
HEAD_PAD = 128
ML_H = 4
ML_D = 64
ML_W = 256
ML_AUG = ML_W + 128
N_EXPERTS = 64
TOP_K = 8
N_EXPERT_GROUPS = 8
TOP_GROUPS = 4
EPG = 8
EXPERT_F = 256
ROUTED_SCALE = 2.5
DEPTH = 2
DEEPNORM_ALPHA = (2 * DEPTH) ** 0.25
LN_EPS = 1e-5
SUBLANES = 8

Z_U, Z_QC, Z_MLQ, Z_MLK, Z_MLV, Z_MLO, Z_KVC, Z_MISC = 0, 256, 512, 768, 1024, 1280, 1536, 1664
Z_W = 1792
MISC_GATES = 0
MISC_ROPE = 64

VMEM_LIMIT = 48 * 1024 * 1024
ROW_TILE = 512
S5_STEPS = 64
ML_CHUNK = 256
ATT_TQ = 2048
ATT_TK = 768
ATT_QSPLIT = 4
MOE_BLOCK = 512
MOE_TILE = 256
ROW_TILE_SHAPE = (SUBLANES, D_MODEL // SUBLANES)


def _cparams(sem):
    return pltpu.CompilerParams(dimension_semantics=sem, vmem_limit_bytes=VMEM_LIMIT)


def _ln(x):
    mu = jnp.mean(x, -1, keepdims=True)
    xc = x - mu
    var = jnp.mean(xc * xc, -1, keepdims=True)
    return xc * lax.rsqrt(var + LN_EPS)


def _store_tile_rows(ref, x):
    rows = x.shape[0]
    for s in range(SUBLANES):
        ref[pl.ds(s, rows, stride=SUBLANES), :] = x[:, s * 128:(s + 1) * 128]


def _load_tile_rows(ref):
    rows = ref.shape[0] // SUBLANES
    return jnp.concatenate([ref[pl.ds(s, rows, stride=SUBLANES), :] for s in range(SUBLANES)], -1)


def _modulate(h, sc, sh):
    tm, d = h.shape
    h3 = h.reshape(tm // SUBLANES, SUBLANES, d)
    return (h3 * (1.0 + sc)[None] + sh[None]).reshape(tm, d)


def _per_batch(v, g):
    tm, d = v.shape
    return (v.reshape(tm // SUBLANES, SUBLANES, d) * g[None]).reshape(tm, d)


def _ada_kernel(c_ref, w_ref, b_ref, o_ref):
    c = c_ref[...]
    s = c * jax.nn.sigmoid(c)
    o_ref[...] = jnp.dot(s.astype(BF16), w_ref[...].astype(BF16), preferred_element_type=F32) + b_ref[...]


def _ada_mod(cc, w, b):
    n = w.shape[1]
    tn = 1536
    return pl.pallas_call(
        _ada_kernel,
        grid=(n // tn,),
        in_specs=[pl.BlockSpec((16, D_MODEL), lambda j: (0, 0)),
                  pl.BlockSpec((D_MODEL, tn), lambda j: (0, j)),
                  pl.BlockSpec((1, tn), lambda j: (0, j))],
        out_specs=pl.BlockSpec((16, tn), lambda j: (0, j)),
        out_shape=jax.ShapeDtypeStruct((16, n), F32),
        compiler_params=_cparams(("arbitrary",)),
        name="ada_mod",
    )(cc, w, b.reshape(1, n))


def _inproj_kernel(x_ref, sc_ref, sh_ref, w_ref, z_ref):
    h = _modulate(_ln(x_ref[...]), sc_ref[0], sh_ref[0])
    z_ref[...] = jnp.dot(h.astype(BF16), w_ref[...], preferred_element_type=F32)


def _inproj(x, mod, w_pad, n_lat_tiles):
    n = x.shape[0]
    tm = ROW_TILE
    sel = lambda i: (i >= n_lat_tiles).astype(jnp.int32)
    return pl.pallas_call(
        _inproj_kernel,
        grid=(n // tm,),
        in_specs=[pl.BlockSpec((tm, D_MODEL), lambda i: (i, 0)),
                  pl.BlockSpec((1, SUBLANES, D_MODEL), lambda i: (sel(i), 0, 1)),
                  pl.BlockSpec((1, SUBLANES, D_MODEL), lambda i: (sel(i), 0, 0)),
                  pl.BlockSpec((D_MODEL, Z_W), lambda i: (0, 0))],
        out_specs=pl.BlockSpec((tm, Z_W), lambda i: (i, 0)),
        out_shape=jax.ShapeDtypeStruct((n, Z_W), F32),
        compiler_params=_cparams(("parallel",)),
        name="inproj",
    )(x, mod, mod, w_pad)


def _rope_block(x, tab_ref, base):
    tm = x.shape[0]

    def tab(i):
        t = tab_ref[base + i]
        return jnp.broadcast_to(t[:, None, :], (tm // SUBLANES, SUBLANES, HEAD_PAD)).reshape(tm, HEAD_PAD)

    return (x * tab(0) + pltpu.roll(x, HEAD_PAD - ROPE_AXIS // 2, 1) * tab(1)
            + pltpu.roll(x, ROPE_AXIS // 2, 1) * tab(2))


def _mla_prep_kernel(qc_ref, kvc_ref, misc_ref, tab_ref, qg_ref, wq_ref, kg_ref, wk_ref, wv_ref,
                     q_ref, k_ref, v_ref):
    qc = qc_ref[...]
    qn = qc * lax.rsqrt(jnp.mean(qc * qc, -1, keepdims=True) + 1e-6) * qg_ref[...]
    q = jnp.dot(qn.astype(BF16), wq_ref[...], preferred_element_type=F32)
    kvc = kvc_ref[...]
    kvn = (kvc * lax.rsqrt(jnp.mean(kvc * kvc, -1, keepdims=True) + 1e-6) * kg_ref[...]).astype(BF16)
    k = jnp.dot(kvn, wk_ref[...], preferred_element_type=F32)
    lane = lax.broadcasted_iota(jnp.int32, (1, MLA_H * HEAD_PAD), 1)
    ones_cols = jnp.where(lane % HEAD_PAD >= MLA_V, 1.0, 0.0)
    v_ref[...] = (jnp.dot(kvn, wv_ref[...], preferred_element_type=F32) + ones_cols).astype(BF16)
    kr = _rope_block(misc_ref[...], tab_ref, 3)
    for h in range(MLA_H):
        sl = slice(h * HEAD_PAD, (h + 1) * HEAD_PAD)
        q_ref[:, sl] = _rope_block(q[:, sl], tab_ref, 0).astype(BF16)
        k_ref[:, sl] = (k[:, sl] + kr).astype(BF16)


def _mla_prep(z, tabs, qg, wq_pad, kg, wk_pad, wv):
    n = z.shape[0]
    tm = ROW_TILE
    tt = tm // SUBLANES
    const = lambda i: (0, 0)
    return pl.pallas_call(
        _mla_prep_kernel,
        grid=(n // tm,),
        in_specs=[pl.BlockSpec((tm, MLA_QR), lambda i: (i, Z_QC // MLA_QR)),
                  pl.BlockSpec((tm, MLA_KVR), lambda i: (i, Z_KVC // MLA_KVR)),
                  pl.BlockSpec((tm, 128), lambda i: (i, Z_MISC // 128)),
                  pl.BlockSpec((6, tt, HEAD_PAD), lambda i: (0, i, 0)),
                  pl.BlockSpec((1, MLA_QR), const),
                  pl.BlockSpec((MLA_QR, MLA_H * HEAD_PAD), const),
                  pl.BlockSpec((1, MLA_KVR), const),
                  pl.BlockSpec((MLA_KVR, MLA_H * HEAD_PAD), const),
                  pl.BlockSpec((MLA_KVR, MLA_H * HEAD_PAD), const)],
        out_specs=[pl.BlockSpec((tm, MLA_H * HEAD_PAD), lambda i: (i, 0)),
                   pl.BlockSpec((tm, MLA_H * HEAD_PAD), lambda i: (i, 0)),
                   pl.BlockSpec((tm, MLA_H * HEAD_PAD), lambda i: (i, 0))],
        out_shape=[jax.ShapeDtypeStruct((n, MLA_H * HEAD_PAD), BF16),
                   jax.ShapeDtypeStruct((n, MLA_H * HEAD_PAD), BF16),
                   jax.ShapeDtypeStruct((n, MLA_H * HEAD_PAD), BF16)],
        compiler_params=_cparams(("parallel",)),
        name="mla_prep",
    )(z, z, z, tabs, qg, wq_pad, kg, wk_pad, wv)


def _flash_kernel(qt_ref, k_ref, vt_ref, o_ref, *, n_main, tk, tail, q_split):
    tq = qt_ref.shape[1]
    hq = tq // q_split
    chains = [(h, r) for h in range(2) for r in range(q_split)]
    nc = len(chains)

    def scores(start, size, c):
        h, r = chains[c]
        hs = slice(h * HEAD_PAD, (h + 1) * HEAD_PAD)
        return jnp.dot(k_ref[pl.ds(start, size), hs], qt_ref[hs, r * hq:(r + 1) * hq],
                       preferred_element_type=F32)

    def colmax(s):
        rows = s.shape[0]
        while rows % 16 == 0 and rows > 8:
            rows //= 2
            s = jnp.maximum(s[:rows], s[rows:])
        return jnp.max(s, 0, keepdims=True)

    def update(s, start, size, c, m, acc):
        h, _ = chains[c]
        hs = slice(h * HEAD_PAD, (h + 1) * HEAD_PAD)
        m_new = jnp.maximum(m, colmax(s))
        p = jnp.exp2(s - m_new).astype(BF16)
        acc = jnp.exp2(m - m_new) * acc + jnp.dot(vt_ref[hs, pl.ds(start, size)], p,
                                                  preferred_element_type=F32)
        return m_new, acc

    state = [(jnp.full((1, hq), -jnp.inf, F32), jnp.zeros((HEAD_PAD, hq), F32)) for _ in chains]
    if n_main:
        def body(j, carry):
            s_cur, st = carry[0], list(carry[1:])
            start = pl.multiple_of(j * tk, tk)
            nxt = pl.multiple_of(jnp.minimum(j + 1, n_main - 1) * tk, tk)
            for c in range(nc):
                s_next = scores(start, tk, c + 1) if c + 1 < nc else scores(nxt, tk, 0)
                st[c] = update(s_cur, start, tk, c, *st[c])
                s_cur = s_next
            return (s_cur, *st)

        carry = lax.fori_loop(0, n_main, body, (scores(0, tk, 0), *state))
        state = list(carry[1:])
    if tail:
        for c in range(nc):
            state[c] = update(scores(n_main * tk, tail, c), n_main * tk, tail, c, *state[c])
    for (h, r), (_, acc) in zip(chains, state):
        o_ref[h * MLA_V:(h + 1) * MLA_V, r * hq:(r + 1) * hq] = (
            acc[0:MLA_V] / acc[MLA_V:MLA_V + 1]).astype(BF16)


def _flash(qt, k2, vt, *, n_q_tiles, tq, q_tile0, n_keys, key_block):
    n_main, tail = divmod(n_keys, ATT_TK)
    pairs = MLA_H // 2
    q_split = min(ATT_QSPLIT, max(1, tq // 256))
    kern = functools.partial(_flash_kernel, n_main=n_main, tk=ATT_TK, tail=tail, q_split=q_split)
    return pl.pallas_call(
        kern,
        grid=(SUBLANES, pairs, n_q_tiles),
        in_specs=[pl.BlockSpec((2 * HEAD_PAD, tq), lambda b, p, i: (b * pairs + p, q_tile0 + i)),
                  pl.BlockSpec((n_keys, 2 * HEAD_PAD), lambda b, p, i: (key_block, b * pairs + p)),
                  pl.BlockSpec((2 * HEAD_PAD, n_keys), lambda b, p, i: (b * pairs + p, key_block))],
        out_specs=pl.BlockSpec((2 * MLA_V, tq), lambda b, p, i: (b * pairs + p, i)),
        out_shape=jax.ShapeDtypeStruct((SUBLANES * MLA_W, n_q_tiles * tq), BF16),
        compiler_params=_cparams(("parallel", "parallel", "arbitrary")),
        name="mla_flash",
    )(qt, k2, vt)


def _s5_kernel(u_ref, bb_ref, cc_ref, a_ref, y_ref, st_ref, bu_ref, *, steps):
    d = pl.program_id(0)

    @pl.when(pl.program_id(1) == 0)
    def _():
        st_ref[...] = jnp.zeros_like(st_ref)

    bu_ref[...] = jnp.dot(u_ref[...].astype(BF16), bb_ref[0], preferred_element_type=F32)
    ar = jnp.broadcast_to(a_ref[0, 0:1, :], (SUBLANES, S5_STATE))
    ai = jnp.broadcast_to(a_ref[0, 1:2, :], (SUBLANES, S5_STATE))

    def body(i, carry):
        re, im = carry
        t = jnp.where(d == 0, i, steps - 1 - i)
        row = pl.multiple_of(t * SUBLANES, SUBLANES)
        bur = bu_ref[pl.ds(row, SUBLANES), 0:S5_STATE]
        bui = bu_ref[pl.ds(row, SUBLANES), S5_STATE:2 * S5_STATE]
        nre = ar * re - ai * im + bur
        nim = ar * im + ai * re + bui
        bu_ref[pl.ds(row, SUBLANES), 0:S5_STATE] = nre
        bu_ref[pl.ds(row, SUBLANES), S5_STATE:2 * S5_STATE] = nim
        return nre, nim

    re, im = lax.fori_loop(0, steps, body, (st_ref[0], st_ref[1]), unroll=4)
    st_ref[0] = re
    st_ref[1] = im
    y_ref[0] = jnp.dot(bu_ref[...].astype(BF16), cc_ref[0], preferred_element_type=F32)


def _s5_scan(z, bblk, cblk, a2, n_lat_steps, n_steps):
    n = z.shape[0]
    rows = S5_STEPS * SUBLANES
    n_chunks = n_steps // S5_STEPS
    n_lat_chunks = n_lat_steps // S5_STEPS

    def blk(d, s):
        return jnp.where(d == 0, (s + n_lat_chunks) % n_chunks, n_chunks - 1 - s)

    return pl.pallas_call(
        functools.partial(_s5_kernel, steps=S5_STEPS),
        grid=(2, n_chunks),
        in_specs=[pl.BlockSpec((rows, S5_W), lambda d, s: (blk(d, s), Z_U // S5_W)),
                  pl.BlockSpec((1, S5_W, 2 * S5_STATE), lambda d, s: (d, 0, 0)),
                  pl.BlockSpec((1, 2 * S5_STATE, S5_W), lambda d, s: (d, 0, 0)),
                  pl.BlockSpec((1, 2, S5_STATE), lambda d, s: (d, 0, 0))],
        out_specs=pl.BlockSpec((1, rows, S5_W), lambda d, s: (d, blk(d, s), 0)),
        out_shape=jax.ShapeDtypeStruct((2, n, S5_W), F32),
        scratch_shapes=[pltpu.VMEM((2, SUBLANES, S5_STATE), F32),
                        pltpu.VMEM((rows, 2 * S5_STATE), F32)],
        compiler_params=_cparams(("arbitrary", "arbitrary")),
        name="s5_scan",
    )(z, bblk, cblk, a2)


def _shift_rows(x, sh, rev):
    z = jnp.zeros((sh, x.shape[1]), x.dtype)
    if rev:
        return jnp.concatenate([x[sh:], z], 0)
    return jnp.concatenate([z, x[:-sh]], 0)


def _mlstm_kernel(q_ref, k_ref, v_ref, g_ref, qp_ref, qn_ref, kp_ref, kn_ref, cw_ref, cb_ref, gb_ref,
                  h_ref, cst_ref, mst_ref, qk_ref, qs_ref, ks_ref, vs_ref, hs_ref, bc_ref, *, rev, n_lat_chunks, n_chunks):
    T = ML_CHUNK
    rows = T * SUBLANES
    s = pl.program_id(0)
    c = (n_chunks - 1 - s) if rev else (s + n_lat_chunks) % n_chunks

    @pl.when(s == 0)
    def _():
        cst_ref[...] = jnp.zeros_like(cst_ref)
        mst_ref[...] = jnp.zeros_like(mst_ref)

    first = jnp.logical_or(c == 0, c == n_lat_chunks)
    last = jnp.logical_or(c == n_lat_chunks - 1, c == n_chunks - 1)
    keep_p = jnp.where(first, 0.0, 1.0)
    keep_n = jnp.where(last, 0.0, 1.0)
    qk_ref[0:SUBLANES, 0:ML_W] = qp_ref[...] * keep_p
    qk_ref[0:SUBLANES, ML_W:2 * ML_W] = kp_ref[...] * keep_p
    qk_ref[SUBLANES:SUBLANES + rows, 0:ML_W] = q_ref[...]
    qk_ref[SUBLANES:SUBLANES + rows, ML_W:2 * ML_W] = k_ref[...]
    qk_ref[SUBLANES + rows:2 * SUBLANES + rows, 0:ML_W] = qn_ref[...] * keep_n
    qk_ref[SUBLANES + rows:2 * SUBLANES + rows, ML_W:2 * ML_W] = kn_ref[...] * keep_n
    cw = cw_ref[...]
    conv = (cb_ref[...] + qk_ref[0:rows, :] * cw[0:1] + qk_ref[SUBLANES:SUBLANES + rows, :] * cw[1:2]
            + qk_ref[2 * SUBLANES:2 * SUBLANES + rows, :] * cw[2:3])
    conv = conv * jax.nn.sigmoid(conv)
    for j in range(2):
        qs_ref[j] = conv[:, j * 128:(j + 1) * 128]
        ks_ref[j] = conv[:, ML_W + j * 128:ML_W + (j + 1) * 128] * (ML_D ** -0.5)
        vs_ref[j] = v_ref[:, j * 128:(j + 1) * 128]

    g = g_ref[...] + gb_ref[...]
    lane = lax.broadcasted_iota(jnp.int32, g.shape, 1)
    is_f = jnp.logical_and(lane >= 2 * ML_H, lane < 4 * ML_H)
    gm = jnp.where(is_f, jax.nn.log_sigmoid(g), g)
    cum = jnp.where(is_f, gm, 0.0)
    sh = SUBLANES
    while sh < rows:
        cum = cum + _shift_rows(cum, sh, rev)
        sh *= 2
    bc_ref[0] = gm
    bc_ref[1] = cum

    i_lane0 = ML_H if rev else 0
    f_lane0 = 3 * ML_H if rev else 2 * ML_H
    end_row = 0 if rev else T - 1
    ti = lax.broadcasted_iota(jnp.int32, (T, T), 0)
    si = lax.broadcasted_iota(jnp.int32, (T, T), 1)
    order = (si >= ti) if rev else (si <= ti)
    lane_w = lax.broadcasted_iota(jnp.int32, (1, ML_W), 1) // ML_D
    rowhead = lax.broadcasted_iota(jnp.int32, (ML_W, 1), 0) // ML_D
    colhead = lax.broadcasted_iota(jnp.int32, (1, ML_AUG), 1)
    colhead = jnp.where(colhead < ML_W, colhead // ML_D, colhead - ML_W)
    blockmask = rowhead == colhead
    ones_aug = jnp.ones((T, ML_AUG - ML_W), F32)

    def per_batch(b, _):
        sl = pl.ds(b, T, stride=SUBLANES)
        qb = jnp.concatenate([qs_ref[0, sl, :], qs_ref[1, sl, :]], -1)
        kb = jnp.concatenate([ks_ref[0, sl, :], ks_ref[1, sl, :]], -1)
        vb = jnp.concatenate([vs_ref[0, sl, :], vs_ref[1, sl, :]], -1)
        gcol = bc_ref[0, sl, :]
        ccol = bc_ref[1, sl, :]
        grow = gcol.T
        crow = ccol.T
        cst = cst_ref[b]
        qc = jnp.dot(qb.astype(BF16), cst.astype(BF16), preferred_element_type=F32)
        vaug = jnp.concatenate([vb, ones_aug], -1).astype(BF16)
        kb16 = kb.astype(BF16)
        num = jnp.zeros((T, ML_W), F32)
        den = jnp.zeros((T, ML_W), F32)
        stab = jnp.zeros((T, ML_W), F32)
        ke_scale = jnp.zeros((T, ML_W), F32)
        a_col = jnp.zeros((ML_W, 1), F32)
        s_col = jnp.zeros((ML_W, 1), F32)
        for h in range(ML_H):
            hm = lane_w == h
            il, fl = i_lane0 + h, f_lane0 + h
            b_col = ccol[:, fl:fl + 1]
            i_col = gcol[:, il:il + 1]
            b_row = crow[fl:fl + 1, :]
            i_row = grow[il:il + 1, :]
            b_tot = ccol[end_row:end_row + 1, fl:fl + 1]
            m_in = mst_ref[b, h][0:1, 0:1]
            ld = jnp.where(order, b_col - b_row + i_row, -jnp.inf)
            m_t = jnp.maximum(b_col + m_in, jnp.max(ld, -1, keepdims=True))
            dw = jnp.exp(ld - m_t)
            w_inter = jnp.exp(b_col + m_in - m_t)
            qh = jnp.where(hm, qb, 0.0).astype(BF16)
            sc = lax.dot_general(qh, kb16, (((1,), (1,)), ((), ())), preferred_element_type=F32) * dw
            pv = jnp.dot(sc.astype(BF16), vaug[:, 0:ML_W], preferred_element_type=F32)
            den_h = jnp.sum(sc, -1, keepdims=True) + w_inter * qc[:, ML_W + h:ML_W + h + 1]
            num = num + jnp.where(hm, pv + w_inter * qc[:, 0:ML_W], 0.0)
            den = den + jnp.where(hm, den_h, 0.0)
            stab = stab + jnp.where(hm, jnp.exp(-m_t), 0.0)
            w_end = b_tot - b_col + i_col
            m_loc = jnp.max(w_end, 0, keepdims=True)
            m_new = jnp.maximum(b_tot + m_in, m_loc)
            ke_scale = ke_scale + jnp.where(hm, jnp.exp(w_end - m_loc), 0.0)
            a_col = a_col + jnp.where(rowhead == h, jnp.exp(b_tot + m_in - m_new), 0.0)
            s_col = s_col + jnp.where(rowhead == h, jnp.exp(m_loc - m_new), 0.0)
            mst_ref[b, h] = jnp.broadcast_to(m_new, (SUBLANES, 128))
        hout = num / jnp.maximum(jnp.abs(den), stab)
        hs_ref[0, sl, :] = hout[:, 0:128]
        hs_ref[1, sl, :] = hout[:, 128:256]
        ke = (kb * ke_scale).astype(BF16)
        upd = lax.dot_general(ke, vaug, (((0,), (0,)), ((), ())), preferred_element_type=F32)
        cst_ref[b] = a_col * cst + jnp.where(blockmask, s_col * upd, 0.0)
        return 0

    lax.fori_loop(0, SUBLANES, per_batch, 0)
    h_ref[...] = jnp.concatenate([hs_ref[0], hs_ref[1]], -1)


def _mlstm(z, conv_w, conv_b, gate_b_pad, *, rev, n_lat_steps, n_steps):
    n = z.shape[0]
    T = ML_CHUNK
    rows = T * SUBLANES
    n_chunks = n_steps // T
    n_lat_chunks = n_lat_steps // T
    hb = rows // SUBLANES
    n_hblk = n // SUBLANES

    def blk(s):
        return (n_chunks - 1 - s) if rev else (s + n_lat_chunks) % n_chunks

    prev = lambda s: jnp.maximum(blk(s) * hb - 1, 0)
    nxt = lambda s: jnp.minimum((blk(s) + 1) * hb, n_hblk - 1)
    const = lambda s: (0, 0)
    kern = functools.partial(_mlstm_kernel, rev=rev, n_lat_chunks=n_lat_chunks, n_chunks=n_chunks)
    return pl.pallas_call(
        kern,
        grid=(n_chunks,),
        in_specs=[pl.BlockSpec((rows, ML_W), lambda s: (blk(s), Z_MLQ // ML_W)),
                  pl.BlockSpec((rows, ML_W), lambda s: (blk(s), Z_MLK // ML_W)),
                  pl.BlockSpec((rows, ML_W), lambda s: (blk(s), Z_MLV // ML_W)),
                  pl.BlockSpec((rows, 128), lambda s: (blk(s), Z_MISC // 128)),
                  pl.BlockSpec((SUBLANES, ML_W), lambda s: (prev(s), Z_MLQ // ML_W)),
                  pl.BlockSpec((SUBLANES, ML_W), lambda s: (nxt(s), Z_MLQ // ML_W)),
                  pl.BlockSpec((SUBLANES, ML_W), lambda s: (prev(s), Z_MLK // ML_W)),
                  pl.BlockSpec((SUBLANES, ML_W), lambda s: (nxt(s), Z_MLK // ML_W)),
                  pl.BlockSpec((3, 2 * ML_W), const),
                  pl.BlockSpec((1, 2 * ML_W), const),
                  pl.BlockSpec((1, 128), const)],
        out_specs=pl.BlockSpec((rows, ML_W), lambda s: (blk(s), 0)),
        out_shape=jax.ShapeDtypeStruct((n, ML_W), F32),
        scratch_shapes=[pltpu.VMEM((SUBLANES, ML_W, ML_AUG), F32),
                        pltpu.VMEM((SUBLANES, ML_H, SUBLANES, 128), F32),
                        pltpu.VMEM((rows + 2 * SUBLANES, 2 * ML_W), F32),
                        pltpu.VMEM((2, rows, 128), F32),
                        pltpu.VMEM((2, rows, 128), F32),
                        pltpu.VMEM((2, rows, 128), F32),
                        pltpu.VMEM((2, rows, 128), F32),
                        pltpu.VMEM((2, rows, 128), F32)],
        compiler_params=_cparams(("arbitrary",)),
        name="mlstm_rev" if rev else "mlstm_fwd",
    )(z, z, z, z, z, z, z, z, conv_w, conv_b, gate_b_pad)


def _post_kernel(x_ref, g1_ref, sc_ref, sh_ref, u_ref, y_ref, o_ref, attn_ref, hf_ref, hb_ref,
                 d_ref, gw_ref, gb_ref, hn_ref, pm_ref, wo_ref, l1g_ref, l1b_ref, rw_ref,
                 x1_ref, f_ref, lg_ref):
    y = y_ref[0] + y_ref[1] + u_ref[...] * d_ref[...]
    g = jax.nn.gelu(y)
    s5 = g * jax.nn.sigmoid(jnp.dot(g.astype(BF16), gw_ref[...], preferred_element_type=F32) + gb_ref[...])
    hh = jax.nn.sigmoid(o_ref[...]) * (hf_ref[...] + hb_ref[...])
    pm = pm_ref[...]
    mu = jnp.dot(hh, pm, preferred_element_type=F32, precision=lax.Precision.HIGHEST)
    hc = hh - mu
    var = jnp.dot(hc * hc, pm, preferred_element_type=F32, precision=lax.Precision.HIGHEST)
    ml = hc * lax.rsqrt(var + LN_EPS) * hn_ref[...]
    mix = jnp.concatenate([s5.astype(BF16), attn_ref[...], ml.astype(BF16)], -1)
    yo = jnp.dot(mix, wo_ref[...], preferred_element_type=F32)
    x1 = _ln(DEEPNORM_ALPHA * x_ref[...] + _per_batch(yo, g1_ref[0])) * l1g_ref[...] + l1b_ref[...]
    x1_ref[...] = x1
    f = _modulate(_ln(x1), sc_ref[0], sh_ref[0])
    _store_tile_rows(f_ref, f)
    lg_ref[...] = lax.dot_general(rw_ref[...], f, (((1,), (1,)), ((), ())), preferred_element_type=F32,
                                  precision=lax.Precision.HIGHEST)


def _post(x, mod, z, y2, attn, hf, hb, p, n_lat_tiles):
    n = x.shape[0]
    tm = ROW_TILE
    sel = lambda i: (i >= n_lat_tiles).astype(jnp.int32)
    const = lambda i: (0, 0)
    row = lambda w: pl.BlockSpec((tm, w), lambda i: (i, 0))
    return pl.pallas_call(
        _post_kernel,
        grid=(n // tm,),
        in_specs=[row(D_MODEL),
                  pl.BlockSpec((1, SUBLANES, D_MODEL), lambda i: (sel(i), 0, 2)),
                  pl.BlockSpec((1, SUBLANES, D_MODEL), lambda i: (sel(i), 0, 4)),
                  pl.BlockSpec((1, SUBLANES, D_MODEL), lambda i: (sel(i), 0, 3)),
                  pl.BlockSpec((tm, S5_W), lambda i: (i, Z_U // S5_W)),
                  pl.BlockSpec((2, tm, S5_W), lambda i: (0, i, 0)),
                  pl.BlockSpec((tm, ML_W), lambda i: (i, Z_MLO // ML_W)),
                  row(MLA_W), row(ML_W), row(ML_W),
                  pl.BlockSpec((1, S5_W), const),
                  pl.BlockSpec((S5_W, S5_W), const),
                  pl.BlockSpec((1, S5_W), const),
                  pl.BlockSpec((1, ML_W), const),
                  pl.BlockSpec((ML_W, ML_W), const),
                  pl.BlockSpec((D_MODEL, D_MODEL), const),
                  pl.BlockSpec((1, D_MODEL), const),
                  pl.BlockSpec((1, D_MODEL), const),
                  pl.BlockSpec((N_EXPERTS, D_MODEL), const)],
        out_specs=[row(D_MODEL), pl.BlockSpec((tm * SUBLANES, 128), lambda i: (i, 0)),
                   pl.BlockSpec((N_EXPERTS, tm), lambda i: (0, i))],
        out_shape=[jax.ShapeDtypeStruct((n, D_MODEL), F32),
                   jax.ShapeDtypeStruct((n * SUBLANES, 128), F32),
                   jax.ShapeDtypeStruct((N_EXPERTS, n), F32)],
        compiler_params=_cparams(("parallel",)),
        name="post_mix",
    )(x, mod, mod, mod, z, y2, z, attn, hf, hb,
      p["s5_d"], p["glu_w"], p["glu_b"], p["ml_norm_g"], p["head_mean"], p["w_out"], p["ln1_g"], p["ln1_b"],
      p["router_w"])


def _expert_kernel(be_ref, x_ref, wgu_ref, wd_ref, o_ref):
    x = _load_tile_rows(x_ref).astype(BF16)
    au = jnp.dot(x, wgu_ref[0], preferred_element_type=F32)
    a, u = au[:, :EXPERT_F], au[:, EXPERT_F:]
    hmid = (a * jax.nn.sigmoid(a) * u).astype(BF16)
    _store_tile_rows(o_ref, jnp.dot(hmid, wd_ref[0], preferred_element_type=F32))


def _experts(blk_expert, xs, wg, wu, wd):
    n_rows = xs.shape[0] // SUBLANES
    tb = MOE_BLOCK
    rows = pl.BlockSpec((tb * SUBLANES, 128), lambda i, be: (i, 0))
    grid_spec = pltpu.PrefetchScalarGridSpec(
        num_scalar_prefetch=1,
        grid=(n_rows // tb,),
        in_specs=[rows,
                  pl.BlockSpec((1, D_MODEL, 2 * EXPERT_F), lambda i, be: (be[i], 0, 0)),
                  pl.BlockSpec((1, EXPERT_F, D_MODEL), lambda i, be: (be[i], 0, 0))],
        out_specs=rows,
    )
    return pl.pallas_call(
        _expert_kernel,
        grid_spec=grid_spec,
        out_shape=jax.ShapeDtypeStruct((n_rows * SUBLANES, 128), F32),
        compiler_params=_cparams(("arbitrary",)),
        name="moe_experts",
    )(blk_expert, xs, jnp.concatenate([wg, wu], -1), wd)


def _idx_copy(idx_hbm, idx_smem, sem, tile, slot, width):
    return pltpu.make_async_copy(idx_hbm.at[tile], idx_smem.at[pl.ds(pl.multiple_of(slot * width, width), width)],
                                 sem.at[slot])


def _row_groups(tm, base, fn):
    def trip(g, c):
        t0 = pl.multiple_of(g * SUBLANES, SUBLANES)
        for j in range(SUBLANES):
            for k in range(TOP_K):
                fn(k, t0 + j, base + (k * tm + j) + t0)
        return c

    lax.fori_loop(0, tm // SUBLANES, trip, 0)


def _dispatch_kernel(pe_ref, pc_ref, f_ref, idx_hbm, xs_hbm, idx_smem, zero_ref, isem, zsem, rsem, *, tm):
    i = pl.program_id(0)
    n = pl.num_programs(0)
    slot = i % 2
    tb = MOE_BLOCK
    width = TOP_K * tm

    def row_copy(t, r):
        return pltpu.make_async_copy(f_ref.at[t], xs_hbm.at[r], rsem)

    def pad_copy(e):
        start = pl.multiple_of(pe_ref[e] - tb, tb)
        return pltpu.make_async_copy(zero_ref, xs_hbm.at[pl.ds(start, tb)], zsem)

    @pl.when(i == 0)
    def _():
        _idx_copy(idx_hbm, idx_smem, isem, 0, 0, width).start()
        zero_ref[...] = jnp.zeros_like(zero_ref)

        def fill(e, c):
            @pl.when(pc_ref[e] > 0)
            def _():
                pad_copy(e).start()
            return c

        def drain(e, c):
            @pl.when(pc_ref[e] > 0)
            def _():
                pad_copy(e).wait()
            return c

        lax.fori_loop(0, N_EXPERTS, fill, 0)
        lax.fori_loop(0, N_EXPERTS, drain, 0)

    _idx_copy(idx_hbm, idx_smem, isem, i, slot, width).wait()

    @pl.when(i + 1 < n)
    def _():
        _idx_copy(idx_hbm, idx_smem, isem, i + 1, 1 - slot, width).start()

    base = slot * width
    _row_groups(tm, base, lambda k, t, a: row_copy(t, idx_smem[a]).start(priority=k % 2))
    _row_groups(tm, base, lambda k, t, a: row_copy(t, idx_smem[a]).wait())


def _dispatch_rows(f3, idx_tiles, pad_end, padded, n_rows):
    n = f3.shape[0]
    tm = MOE_TILE
    grid_spec = pltpu.PrefetchScalarGridSpec(
        num_scalar_prefetch=2,
        grid=(n // tm,),
        in_specs=[pl.BlockSpec((tm,) + ROW_TILE_SHAPE, lambda i, pe, pc: (i, 0, 0)),
                  pl.BlockSpec(memory_space=pl.ANY)],
        out_specs=pl.BlockSpec(memory_space=pl.ANY),
        scratch_shapes=[pltpu.SMEM((2 * TOP_K * tm,), jnp.int32),
                        pltpu.VMEM((MOE_BLOCK,) + ROW_TILE_SHAPE, F32),
                        pltpu.SemaphoreType.DMA((2,)),
                        pltpu.SemaphoreType.DMA,
                        pltpu.SemaphoreType.DMA],
    )
    return pl.pallas_call(
        functools.partial(_dispatch_kernel, tm=tm),
        grid_spec=grid_spec,
        out_shape=jax.ShapeDtypeStruct((n_rows,) + ROW_TILE_SHAPE, F32),
        compiler_params=_cparams(("arbitrary",)),
        name="moe_dispatch",
    )(pad_end, padded, f3, idx_tiles)


def _final_kernel(x1_ref, f_ref, gate_ref, g2_ref, wg_ref, wu_ref, wd_ref, lg_ref, lb_ref, idx_hbm, ys_hbm,
                  o_ref, idx_smem, buf_ref, isem, rsem, *, tm):
    i = pl.program_id(0)
    n = pl.num_programs(0)
    width = TOP_K * tm

    def row_copy(slot, k, t, r):
        dst = buf_ref.at[slot, k, pl.ds(pl.multiple_of(t * SUBLANES, SUBLANES), SUBLANES)]
        return pltpu.make_async_copy(ys_hbm.at[r], dst, rsem.at[slot])

    def gather(tile_slot):
        base = tile_slot * width
        _row_groups(tm, base, lambda k, t, a: row_copy(tile_slot, k, t, idx_smem[a]).start(priority=k % 2))

    @pl.when(i == 0)
    def _():
        _idx_copy(idx_hbm, idx_smem, isem, 0, 0, width).start()
        _idx_copy(idx_hbm, idx_smem, isem, 0, 0, width).wait()
        gather(0)

        @pl.when(n > 1)
        def _():
            _idx_copy(idx_hbm, idx_smem, isem, 1, 1, width).start()

    cur = i % 2

    @pl.when(i + 1 < n)
    def _():
        _idx_copy(idx_hbm, idx_smem, isem, i + 1, 1 - cur, width).wait()
        gather(1 - cur)

    @pl.when(i + 2 < n)
    def _():
        _idx_copy(idx_hbm, idx_smem, isem, i + 2, cur, width).start()

    f = _load_tile_rows(f_ref).astype(BF16)
    a = jnp.dot(f, wg_ref[...], preferred_element_type=F32)
    u = jnp.dot(f, wu_ref[...], preferred_element_type=F32)
    ffn = jnp.dot((a * jax.nn.sigmoid(a) * u).astype(BF16), wd_ref[...], preferred_element_type=F32)
    _row_groups(tm, 0, lambda k, t, a: row_copy(cur, k, t, 0).wait())
    gate = gate_ref[...]
    for k in range(TOP_K):
        ffn = ffn + gate[:, k:k + 1] * _load_tile_rows(buf_ref.at[cur, k])
    o_ref[...] = _ln(DEEPNORM_ALPHA * x1_ref[...] + _per_batch(ffn, g2_ref[0])) * lg_ref[...] + lb_ref[...]


def _final(x1, f3, gate_t, idx_tiles, ys, mod, p, n_lat_tiles, n_out_tiles):
    tm = MOE_TILE
    n_lat_tiles = n_lat_tiles * ROW_TILE // tm
    n_out_tiles = n_out_tiles * ROW_TILE // tm
    sel = lambda i: (i >= n_lat_tiles).astype(jnp.int32)
    const = lambda i: (0, 0)
    row = lambda w: pl.BlockSpec((tm, w), lambda i: (i, 0))
    return pl.pallas_call(
        functools.partial(_final_kernel, tm=tm),
        grid=(n_out_tiles,),
        in_specs=[row(D_MODEL), pl.BlockSpec((tm * SUBLANES, 128), lambda i: (i, 0)), row(TOP_K),
                  pl.BlockSpec((1, SUBLANES, D_MODEL), lambda i: (sel(i), 0, 5)),
                  pl.BlockSpec((D_MODEL, EXPERT_F), const),
                  pl.BlockSpec((D_MODEL, EXPERT_F), const),
                  pl.BlockSpec((EXPERT_F, D_MODEL), const),
                  pl.BlockSpec((1, D_MODEL), const),
                  pl.BlockSpec((1, D_MODEL), const),
                  pl.BlockSpec(memory_space=pl.ANY),
                  pl.BlockSpec(memory_space=pl.ANY)],
        out_specs=row(D_MODEL),
        out_shape=jax.ShapeDtypeStruct((n_out_tiles * tm, D_MODEL), F32),
        scratch_shapes=[pltpu.SMEM((2 * TOP_K * tm,), jnp.int32),
                        pltpu.VMEM((2, TOP_K, tm * SUBLANES, 128), F32),
                        pltpu.SemaphoreType.DMA((2,)),
                        pltpu.SemaphoreType.DMA((2,))],
        compiler_params=_cparams(("arbitrary",)),
        name="final_ffn",
    )(x1, f3, gate_t, mod, p["sh_w_gate"], p["sh_w_up"], p["sh_w_down"], p["ln2_g"], p["ln2_b"], idx_tiles, ys)


def _route_kernel(lg_ref, bias_ref, tri_ref, e_ref, g_ref, r_ref, cnt_ref, carry_ref):
    @pl.when(pl.program_id(0) == 0)
    def _():
        carry_ref[...] = jnp.zeros_like(carry_ref)

    tm = lg_ref.shape[1]
    neg = -jnp.inf
    s = jax.nn.sigmoid(lg_ref[...])
    g3 = (s + bias_ref[...]).reshape(N_EXPERT_GROUPS, EPG, tm)
    io3 = lax.broadcasted_iota(jnp.int32, (N_EXPERT_GROUPS, EPG, tm), 1)
    m1 = jnp.max(g3, 1, keepdims=True)
    f1 = jnp.min(jnp.where(g3 == m1, io3, EPG), 1, keepdims=True)
    m2 = jnp.max(jnp.where(io3 == f1, neg, g3), 1, keepdims=True)
    gs = m1 + m2
    iog = lax.broadcasted_iota(jnp.int32, (N_EXPERT_GROUPS, 1, tm), 0)
    cur = gs
    kth = gs
    for _ in range(TOP_GROUPS):
        kth = jnp.max(cur, 0, keepdims=True)
        fi = jnp.min(jnp.where(cur == kth, iog, N_EXPERT_GROUPS), 0, keepdims=True)
        cur = jnp.where(iog == fi, neg, cur)
    cand = jnp.where(gs >= kth, g3, neg).reshape(N_EXPERTS, tm)
    io = lax.broadcasted_iota(jnp.int32, (N_EXPERTS, tm), 0)
    memb = jnp.zeros((N_EXPERTS, tm), F32)
    es, gates, hots = [], [], []
    for _ in range(TOP_K):
        mk = jnp.max(cand, 0, keepdims=True)
        ik = jnp.min(jnp.where(cand == mk, io, N_EXPERTS), 0, keepdims=True)
        oh = io == ik
        gates.append(jnp.sum(jnp.where(oh, s, 0.0), 0, keepdims=True))
        es.append(ik)
        hots.append(oh)
        cand = jnp.where(oh, neg, cand)
        memb = memb + jnp.where(oh, 1.0, 0.0)
    gsum = gates[0]
    for gk in gates[1:]:
        gsum = gsum + gk
    g_ref[...] = jnp.concatenate(gates, 0) / gsum * ROUTED_SCALE
    e_ref[...] = jnp.concatenate(es, 0)
    pref = jnp.dot(memb.astype(BF16), tri_ref[...], preferred_element_type=F32) + carry_ref[:, 0:1]
    ranks = [jnp.sum(jnp.where(oh, pref, 0.0), 0, keepdims=True) for oh in hots]
    r_ref[...] = jnp.concatenate(ranks, 0).astype(jnp.int32)
    total = carry_ref[...] + jnp.sum(memb, 1, keepdims=True)
    carry_ref[...] = total
    cnt_ref[...] = total


def _route(logits_t, router_bias):
    n = logits_t.shape[1]
    tm = ROW_TILE
    tri = (jnp.arange(tm)[:, None] < jnp.arange(tm)[None, :]).astype(BF16)
    const = lambda i: (0, 0)
    col = pl.BlockSpec((TOP_K, tm), lambda i: (0, i))
    top_e, gate, rank, cnt = pl.pallas_call(
        _route_kernel,
        grid=(n // tm,),
        in_specs=[pl.BlockSpec((N_EXPERTS, tm), lambda i: (0, i)),
                  pl.BlockSpec((N_EXPERTS, 1), const),
                  pl.BlockSpec((tm, tm), const)],
        out_specs=[col, col, col, pl.BlockSpec((N_EXPERTS, 128), const)],
        out_shape=[jax.ShapeDtypeStruct((TOP_K, n), jnp.int32),
                   jax.ShapeDtypeStruct((TOP_K, n), F32),
                   jax.ShapeDtypeStruct((TOP_K, n), jnp.int32),
                   jax.ShapeDtypeStruct((N_EXPERTS, 128), F32)],
        scratch_shapes=[pltpu.VMEM((N_EXPERTS, 128), F32)],
        compiler_params=_cparams(("arbitrary",)),
        name="moe_route",
    )(logits_t, router_bias.astype(F32).reshape(N_EXPERTS, 1), tri)
    return top_e, gate, rank, cnt[:, 0].astype(jnp.int32)


def _dispatch(top_e, rank, counts):
    n_tok = top_e.shape[1]
    tb = MOE_BLOCK
    padded = (counts + tb - 1) // tb * tb
    pad_end = jnp.cumsum(padded)
    pad_start = pad_end - padded
    hot = top_e[:, :, None] == jnp.arange(N_EXPERTS, dtype=top_e.dtype)
    dest = (jnp.sum(jnp.where(hot, pad_start.astype(jnp.int32), 0), -1) + rank).astype(jnp.int32)
    n_rows = -(-(n_tok * TOP_K + N_EXPERTS * (tb - 1)) // tb) * tb
    n_blocks = n_rows // tb
    blk_start = jnp.arange(n_blocks, dtype=jnp.int32) * tb
    blk_expert = jnp.minimum(jnp.sum((pad_end[None, :] <= blk_start[:, None]).astype(jnp.int32), 1),
                             N_EXPERTS - 1)
    tm = MOE_TILE
    idx_tiles = jnp.transpose(dest.reshape(TOP_K, n_tok // tm, tm), (1, 0, 2)).reshape(n_tok // tm, TOP_K * tm)
    return idx_tiles, blk_expert, pad_end.astype(jnp.int32), padded.astype(jnp.int32), n_rows


def _pad_w_in(w_in):
    s5u, qc, kvc, kr, mq, mk, mv, mo, gates = jnp.split(
        w_in, [256, 512, 640, 672, 928, 1184, 1440, 1696], axis=1)
    misc = jnp.zeros((D_MODEL, 128), w_in.dtype)
    misc = misc.at[:, MISC_GATES:MISC_GATES + 16].set(gates).at[:, MISC_ROPE:MISC_ROPE + MLA_ROPE].set(kr)
    return jnp.concatenate([s5u, qc, mq, mk, mv, mo, kvc, misc], axis=1).astype(BF16)


def _pad_heads(w, width):
    k = w.shape[0]
    w3 = w.reshape(k, MLA_H, width)
    return jnp.pad(w3, ((0, 0), (0, 0), (0, HEAD_PAD - width))).reshape(k, MLA_H * HEAD_PAD)


def _rope_tables(n_lat, n_ctx):
    t = jnp.arange(n_lat)
    half = ROPE_AXIS // 2
    inv_freq = ROPE_BASE ** (-jnp.arange(half, dtype=F32) / half)
    ang_r = (t // GRID_W).astype(F32)[:, None] * inv_freq
    ang_c = (t % GRID_W).astype(F32)[:, None] * inv_freq
    ang = jnp.concatenate([ang_r, ang_r, ang_c, ang_c], -1)
    ang = jnp.concatenate([ang, jnp.zeros((n_ctx, MLA_ROPE), F32)], 0)
    cos, sin = jnp.cos(ang), jnp.sin(ang)
    first = (jnp.arange(MLA_ROPE) % ROPE_AXIS) < half

    def place(v, fill):
        out = jnp.full((v.shape[0], HEAD_PAD), fill, F32)
        return out.at[:, MLA_NOPE:MLA_NOPE + MLA_ROPE].set(v)

    q_scale = MLA_SCALE * math.log2(math.e)
    c_q = place(cos, 1.0) * q_scale
    m_q = place(jnp.where(first, -sin, 0.0), 0.0) * q_scale
    p_q = place(jnp.where(first, 0.0, sin), 0.0) * q_scale
    c_k = place(cos, 0.0)
    m_k = place(jnp.where(first, -sin, 0.0), 0.0)
    p_k = place(jnp.where(first, 0.0, sin), 0.0)
    return jnp.stack([c_q, m_q, p_q, c_k, m_k, p_k], 0)


def _s5_params(lam_re, lam_im, log_step, b_re, b_im, c_re, c_im):
    lr, li = lam_re.astype(F32), lam_im.astype(F32)
    step = jnp.exp(log_step.astype(F32))[..., None]
    mag = jnp.exp(lr * step)
    ar, ai = mag * jnp.cos(li * step), mag * jnp.sin(li * step)
    nr, ni = ar - 1.0, ai
    den = lr * lr + li * li
    cr, ci = ((nr * lr + ni * li) / den)[..., None], ((ni * lr - nr * li) / den)[..., None]
    br, bi = b_re.astype(F32), b_im.astype(F32)
    bbr, bbi = cr * br - ci * bi, cr * bi + ci * br
    eye = jnp.eye(S5_G, dtype=F32)
    b_blk = lambda v: jnp.einsum("dgcp,gh->dgchp", jnp.transpose(v, (0, 1, 3, 2)), eye).reshape(2, S5_W, S5_STATE)
    bblk = jnp.concatenate([b_blk(bbr), b_blk(bbi)], -1)
    c_blk = lambda v: jnp.einsum("dgpc,gh->dgphc", jnp.transpose(v.astype(F32), (0, 1, 3, 2)), eye).reshape(
        2, S5_STATE, S5_W)
    cblk = jnp.concatenate([c_blk(c_re), -c_blk(c_im)], 1)
    a2 = jnp.stack([ar.reshape(2, S5_STATE), ai.reshape(2, S5_STATE)], 1)
    return bblk.astype(BF16), cblk.astype(BF16), a2


def kernel(x, c, ctx, c_ctx, ada_w, ada_b, w_in, s5_lambda_re, s5_lambda_im, s5_log_step, s5_b_re, s5_b_im, s5_c_re, s5_c_im, s5_d, s5_glu_w, s5_glu_b, mla_q_norm, mla_w_q_up, mla_kv_norm, mla_w_kv_up, ml_conv_w, ml_conv_b, ml_gate_b, ml_norm_g, w_out, ln1_g, ln1_b, ln2_g, ln2_b, router_w, router_bias, exp_w_gate, exp_w_up, exp_w_down, sh_w_gate, sh_w_up, sh_w_down):
    bsz, n_lat, d = x.shape
    n_ctx = ctx.shape[1]
    depth = ada_w.shape[0]
    assert bsz == SUBLANES and d == D_MODEL
    assert n_lat % ATT_TQ == 0 and n_lat % ML_CHUNK == 0 and n_ctx % ML_CHUNK == 0
    assert (n_lat * bsz) % ROW_TILE == 0 and (n_ctx * bsz) % ROW_TILE == 0
    n_steps = n_lat + n_ctx
    n_tok = n_steps * bsz
    n_lat_tiles = n_lat * bsz // ROW_TILE

    xs = jnp.concatenate([jnp.transpose(x, (1, 0, 2)), jnp.transpose(ctx, (1, 0, 2))], 0).reshape(n_tok, d)
    cc = jnp.concatenate([c, jnp.broadcast_to(c_ctx[None], (bsz, d))], 0)
    tabs = _rope_tables(n_lat, n_ctx)
    head_mean = jnp.kron(jnp.eye(ML_H, dtype=F32), jnp.full((ML_D, ML_D), 1.0 / ML_D, F32))

    for layer in range(depth):
        last = layer == depth - 1
        mod = _ada_mod(cc, ada_w[layer], ada_b[layer]).reshape(2, bsz, 6 * d)
        z = _inproj(xs, mod, _pad_w_in(w_in[layer]), n_lat_tiles)

        w_kv = mla_w_kv_up[layer].reshape(MLA_KVR, MLA_H, MLA_NOPE + MLA_V)
        wk_pad = _pad_heads(w_kv[:, :, :MLA_NOPE].reshape(MLA_KVR, MLA_H * MLA_NOPE), MLA_NOPE).astype(BF16)
        wv = _pad_heads(w_kv[:, :, MLA_NOPE:].reshape(MLA_KVR, MLA_W), MLA_V).astype(BF16)
        wq_pad = _pad_heads(mla_w_q_up[layer], MLA_NOPE + MLA_ROPE).astype(BF16)
        q, k, v = _mla_prep(z, tabs, mla_q_norm[layer].reshape(1, -1), wq_pad,
                            mla_kv_norm[layer].reshape(1, -1), wk_pad, wv)
        qt = q.reshape(n_steps, bsz * MLA_H * HEAD_PAD).T
        k2 = k.reshape(n_steps, bsz * MLA_H * HEAD_PAD)
        vt = v.reshape(n_steps, bsz * MLA_H * HEAD_PAD).T
        attn = _flash(qt, k2, vt, n_q_tiles=n_lat // ATT_TQ, tq=ATT_TQ, q_tile0=0, n_keys=n_steps, key_block=0)
        if last:
            attn_ctx = jnp.zeros((bsz * MLA_W, n_ctx), BF16)
        else:
            attn_ctx = _flash(qt, k2, vt, n_q_tiles=1, tq=n_ctx, q_tile0=n_lat // n_ctx, n_keys=n_ctx,
                              key_block=n_lat // n_ctx)
        attn = jnp.concatenate([attn, attn_ctx], 1).T.reshape(n_tok, MLA_W)

        bblk, cblk, a2 = _s5_params(s5_lambda_re[layer], s5_lambda_im[layer], s5_log_step[layer],
                                    s5_b_re[layer], s5_b_im[layer], s5_c_re[layer], s5_c_im[layer])
        y2 = _s5_scan(z, bblk, cblk, a2, n_lat, n_steps)

        gate_b_pad = jnp.zeros((1, 128), F32).at[0, :4 * ML_H].set(ml_gate_b[layer].reshape(4 * ML_H))
        cw, cb = ml_conv_w[layer], ml_conv_b[layer].reshape(1, -1)
        hf = _mlstm(z, cw, cb, gate_b_pad, rev=False, n_lat_steps=n_lat, n_steps=n_steps)
        hb = _mlstm(z, cw, cb, gate_b_pad, rev=True, n_lat_steps=n_lat, n_steps=n_steps)

        p = dict(s5_d=s5_d[layer].reshape(1, -1), glu_w=s5_glu_w[layer].astype(BF16),
                 glu_b=s5_glu_b[layer].reshape(1, -1), ml_norm_g=ml_norm_g[layer].reshape(1, -1),
                 head_mean=head_mean, w_out=w_out[layer].astype(BF16), ln1_g=ln1_g[layer].reshape(1, -1),
                 ln1_b=ln1_b[layer].reshape(1, -1), router_w=router_w[layer].T,
                 sh_w_gate=sh_w_gate[layer].astype(BF16), sh_w_up=sh_w_up[layer].astype(BF16),
                 sh_w_down=sh_w_down[layer].astype(BF16), ln2_g=ln2_g[layer].reshape(1, -1),
                 ln2_b=ln2_b[layer].reshape(1, -1))
        x1, f3, logits = _post(xs, mod, z, y2, attn, hf, hb, p, n_lat_tiles)

        top_e, gate, rank, counts = _route(logits, router_bias[layer])
        idx_tiles, blk_expert, pad_end, padded, n_rows = _dispatch(top_e, rank, counts)
        rows_in = _dispatch_rows(f3.reshape((n_tok,) + ROW_TILE_SHAPE), idx_tiles, pad_end, padded, n_rows)
        rows_out = _experts(blk_expert, rows_in.reshape(n_rows * SUBLANES, 128), exp_w_gate[layer].astype(BF16),
                            exp_w_up[layer].astype(BF16), exp_w_down[layer].astype(BF16))
        n_out_tiles = n_lat_tiles if last else n_tok // ROW_TILE
        xs = _final(x1, f3, gate.T, idx_tiles, rows_out.reshape((n_rows,) + ROW_TILE_SHAPE), mod, p,
                    n_lat_tiles, n_out_tiles)

    return jnp.transpose(xs[:n_lat * bsz].reshape(n_lat, bsz, d), (1, 0, 2))
```
